```python
import jax, jax.numpy as jnp
from jax import lax
import numpy as np

D_MODEL = 2048
BATCH = 2
SEQ = 8192
DEPTH = 2

GRID_W = 64
CTX_LEN = 256
RMS_EPS = 1e-6
ROPE_BASE = 10000.0
N_BRANCH = 3
A_WIDTH = 512
CONV_K = 3
GLA_HEADS = 6
GLA_DK = 64
GLA_DV = 128
GLA_KD = GLA_HEADS * GLA_DK
GLA_VD = GLA_HEADS * GLA_DV
GLA_LOWRANK = 16
GLA_TAU = 16.0
GLA_CHUNK = 64
NA_HEADS = 6
NA_DH = 128
NA_D = NA_HEADS * NA_DH
NA_WIN_H = 8
NA_WIN_W = 16
FFN_HIDDEN = -(-8 * D_MODEL // (3 * 256)) * 256
SPLIT_SIZES = (A_WIDTH, A_WIDTH, A_WIDTH,
               GLA_KD, GLA_KD, GLA_VD, GLA_VD, GLA_LOWRANK, GLA_LOWRANK,
               NA_D, NA_D, NA_D,
               N_BRANCH * D_MODEL)
IN_COLS = sum(SPLIT_SIZES)

kernel_name = "hybrid_conv_gla_natten_dit_block"


def rmsnorm(x, g):
    xf = x.astype(jnp.float32)
    y = xf * lax.rsqrt(jnp.mean(xf * xf, axis=-1, keepdims=True) + RMS_EPS)
    return (y * g.astype(jnp.float32)).astype(x.dtype)


def adaln(cvec, w, b):
    m = jax.nn.silu(cvec) @ w + b
    return [t[:, None, :] for t in jnp.split(m, 6, axis=-1)]


def split_columns(p):
    idx = np.cumsum(np.array(SPLIT_SIZES))[:-1].tolist()
    return jnp.split(p, idx, axis=-1)


def heads(t, h):
    return t.reshape(t.shape[0], t.shape[1], h, -1)


def axial_rope(T, dim, dtype):
    pos = jnp.arange(T, dtype=jnp.int32)
    row = (pos // GRID_W).astype(jnp.float32)
    col = (pos % GRID_W).astype(jnp.float32)
    per_axis = dim // 2
    inv = ROPE_BASE ** (-jnp.arange(0, per_axis, 2, dtype=jnp.float32) / per_axis)
    ang = jnp.concatenate([row[:, None] * inv, col[:, None] * inv], axis=-1)
    return jnp.cos(ang).astype(dtype), jnp.sin(ang).astype(dtype)


def apply_rope(x, cos, sin):
    half = x.shape[-1] // 2
    x1, x2 = x[..., :half], x[..., half:]
    cos, sin = cos[None, :, None, :], sin[None, :, None, :]
    return jnp.concatenate([x1 * cos - x2 * sin, x1 * sin + x2 * cos], axis=-1)


def gated_short_conv(a_in, a_b, a_c, w):
    u = a_c * a_in
    up = jnp.pad(u, ((0, 0), (1, 1), (0, 0)))
    conv = w[0] * up[:, :-2] + w[1] * up[:, 1:-1] + w[2] * up[:, 2:]
    return a_b * conv


def gla_chunked(q, k, v, log_a, s0):
    B, T, H, DK = q.shape
    C = GLA_CHUNK
    n = T // C

    def to_chunks(t):
        return t.reshape(B, n, C, H, -1).transpose(1, 0, 3, 2, 4)

    xs = tuple(to_chunks(t) for t in (q, k, v, log_a))
    mask = jnp.tril(jnp.ones((C, C), dtype=bool))[:, :, None]

    def step(s, inp):
        qi, ki, vi, gi = (t.astype(jnp.float32) for t in inp)
        b = jnp.cumsum(gi, axis=2)
        diff = b[:, :, :, None, :] - b[:, :, None, :, :]
        decay = jnp.exp(jnp.where(mask, diff, -jnp.inf))
        attn = jnp.einsum('bhtd,bhsd,bhtsd->bhts', qi, ki, decay)
        o = jnp.einsum('bhts,bhsv->bhtv', attn, vi) + jnp.einsum('bhtd,bhdv->bhtv', qi * jnp.exp(b), s)
        b_last = b[:, :, -1:, :]
        s_new = jnp.exp(b_last[:, :, 0, :, None]) * s + jnp.einsum('bhsd,bhsv->bhdv', ki * jnp.exp(b_last - b), vi)
        return s_new, o

    s_fin, o = lax.scan(step, s0, xs)
    o = o.transpose(1, 0, 3, 2, 4).reshape(B, T, H, -1).astype(v.dtype)
    return o, s_fin


def gla_bidir(q, k, v, la_f, la_b, s_f0, s_b0):
    o_f, s_f = gla_chunked(q, k, v, la_f, s_f0)
    flip = lambda t: jnp.flip(t, axis=1)
    o_b, s_b = gla_chunked(flip(q), flip(k), flip(v), flip(la_b), s_b0)
    return o_f + flip(o_b), s_f, s_b


def gla_inputs(g_q, g_k, g_v, g_af, g_ab, wa_f, ba_f, wa_b, ba_b, rope):
    B, T, _ = g_q.shape
    q = heads(g_q, GLA_HEADS) * (GLA_DK ** -0.5)
    k = heads(g_k, GLA_HEADS)
    v = heads(g_v, GLA_HEADS)
    if rope is not None:
        q = apply_rope(q, *rope)
        k = apply_rope(k, *rope)
    la_f = jax.nn.log_sigmoid((g_af @ wa_f + ba_f).astype(jnp.float32)) / GLA_TAU
    la_b = jax.nn.log_sigmoid((g_ab @ wa_b + ba_b).astype(jnp.float32)) / GLA_TAU
    return q, k, v, la_f.reshape(B, T, GLA_HEADS, GLA_DK), la_b.reshape(B, T, GLA_HEADS, GLA_DK)


def gla_output(o, g_r, g_norm):
    B, T = o.shape[:2]
    return rmsnorm(o, g_norm).reshape(B, T, GLA_VD) * jax.nn.silu(g_r)


def neighbourhood_attention(q, k, v, k_ctx, v_ctx, rpb):
    B, S, H, dh = q.shape
    rows = S // GRID_W
    kh = min(NA_WIN_H, rows)
    kw = NA_WIN_W
    qg = q.reshape(B, rows, GRID_W, H, dh)
    kg = k.reshape(B, rows, GRID_W, H, dh)
    vg = v.reshape(B, rows, GRID_W, H, dh)
    col = np.arange(GRID_W)
    col_start = np.clip(col - kw // 2, 0, GRID_W - kw)
    col_idx = col_start[:, None] + np.arange(kw)[None, :]
    col_off = col_idx - col[:, None] + (NA_WIN_W - 1)
    n_loc = kh * kw

    def row_block(r):
        r_start = jnp.clip(r - kh // 2, 0, rows - kh)
        q_r = lax.dynamic_index_in_dim(qg, r, axis=1, keepdims=False)
        k_rows = lax.dynamic_slice_in_dim(kg, r_start, kh, axis=1)
        v_rows = lax.dynamic_slice_in_dim(vg, r_start, kh, axis=1)
        k_win = k_rows[:, :, col_idx]
        v_win = v_rows[:, :, col_idx]
        s_loc = jnp.einsum('bqhd,biqjhd->bhqij', q_r, k_win).astype(jnp.float32)
        row_off = r_start + jnp.arange(kh) - r + (NA_WIN_H - 1)
        bias = rpb[:, row_off][:, :, col_off]
        s_loc = s_loc + bias.transpose(0, 2, 1, 3)[None].astype(jnp.float32)
        s_ctx = jnp.einsum('bqhd,bchd->bhqc', q_r, k_ctx).astype(jnp.float32)
        scores = jnp.concatenate([s_loc.reshape(B, H, GRID_W, n_loc), s_ctx], axis=-1)
        p = jax.nn.softmax(scores, axis=-1).astype(v.dtype)
        p_loc = p[..., :n_loc].reshape(B, H, GRID_W, kh, kw)
        p_ctx = p[..., n_loc:]
        return (jnp.einsum('bhqij,biqjhd->bqhd', p_loc, v_win)
                + jnp.einsum('bhqc,bchd->bqhd', p_ctx, v_ctx))

    o = lax.map(row_block, jnp.arange(rows))
    return o.transpose(1, 0, 2, 3, 4).reshape(B, S, H * dh)


def context_attention(q, k, v):
    B, T, H, dh = q.shape
    s = jnp.einsum('bqhd,bkhd->bhqk', q, k).astype(jnp.float32)
    p = jax.nn.softmax(s, axis=-1).astype(v.dtype)
    return jnp.einsum('bhqk,bkhd->bqhd', p, v).reshape(B, T, H * dh)


def merge_branches(a, g, n, gates, w_a_out, w_g_out, w_n_out, w_o):
    ga, gg, gn = jnp.split(jax.nn.sigmoid(gates), N_BRANCH, axis=-1)
    y = ga * (a @ w_a_out) + gg * (g @ w_g_out) + gn * (n @ w_n_out)
    return y @ w_o


def token_mixers(n, nc, w_in, conv_w, wa_f, ba_f, wa_b, ba_b, g_gla, rpb,
                 w_a_out, w_g_out, w_n_out, w_o, rope, need_ctx):
    B = n.shape[0]
    pl = split_columns(n @ w_in)
    pc = split_columns(nc @ w_in)
    qc_, kc_, vc_, lfc, lbc = gla_inputs(pc[3], pc[4], pc[5], pc[7], pc[8], wa_f, ba_f, wa_b, ba_b, None)
    s0 = jnp.zeros((B, GLA_HEADS, GLA_DK, GLA_DV), jnp.float32)
    o_gc, s_f, s_b = gla_bidir(qc_, kc_, vc_, lfc, lbc, s0, s0)
    q_, k_, v_, lf, lb = gla_inputs(pl[3], pl[4], pl[5], pl[7], pl[8], wa_f, ba_f, wa_b, ba_b, rope)
    o_g, _, _ = gla_bidir(q_, k_, v_, lf, lb, s_f, s_b)
    g_lat = gla_output(o_g, pl[6], g_gla)
    scale = NA_DH ** -0.5
    k_ctx = heads(pc[10], NA_HEADS)
    v_ctx = heads(pc[11], NA_HEADS)
    na_lat = neighbourhood_attention(heads(pl[9], NA_HEADS) * scale, heads(pl[10], NA_HEADS),
                                     heads(pl[11], NA_HEADS), k_ctx, v_ctx, rpb)
    a_lat = gated_short_conv(pl[0], pl[1], pl[2], conv_w)
    y = merge_branches(a_lat, g_lat, na_lat, pl[12], w_a_out, w_g_out, w_n_out, w_o)
    if not need_ctx:
        return y, None
    a_ctx = gated_short_conv(pc[0], pc[1], pc[2], conv_w)
    g_ctx = gla_output(o_gc, pc[6], g_gla)
    na_ctx = context_attention(heads(pc[9], NA_HEADS) * scale, k_ctx, v_ctx)
    yc = merge_branches(a_ctx, g_ctx, na_ctx, pc[12], w_a_out, w_g_out, w_n_out, w_o)
    return y, yc


def swiglu(n, w1, w3, w2):
    return (jax.nn.silu(n @ w1) * (n @ w3)) @ w2


def setup_inputs(seed: int = 0) -> dict:
    key = jax.random.key(seed)
    ks = jax.random.split(key, 25)

    def nrm(k, shape, scale):
        return jax.random.normal(k, shape, jnp.float32) * scale

    D = D_MODEL
    return {
        "x": nrm(ks[0], (BATCH, SEQ, D), 1.0),
        "c": nrm(ks[1], (BATCH, D), 1.0),
        "ctx": nrm(ks[2], (BATCH, CTX_LEN, D), 1.0),
        "c_ctx": nrm(ks[3], (D,), 1.0),
        "w_ada": nrm(ks[4], (DEPTH, D, 6 * D), 0.5 * D ** -0.5),
        "b_ada": nrm(ks[5], (DEPTH, 6 * D), 0.02),
        "g_mix": 1.0 + nrm(ks[6], (DEPTH, D), 0.05),
        "g_ffn": 1.0 + nrm(ks[7], (DEPTH, D), 0.05),
        "w_in": nrm(ks[8], (DEPTH, D, IN_COLS), D ** -0.5),
        "conv_w": nrm(ks[9], (DEPTH, CONV_K, A_WIDTH), CONV_K ** -0.5),
        "gla_wa_f": nrm(ks[10], (DEPTH, GLA_LOWRANK, GLA_KD), GLA_LOWRANK ** -0.5),
        "gla_ba_f": nrm(ks[11], (DEPTH, GLA_KD), 0.1),
        "gla_wa_b": nrm(ks[12], (DEPTH, GLA_LOWRANK, GLA_KD), GLA_LOWRANK ** -0.5),
        "gla_ba_b": nrm(ks[13], (DEPTH, GLA_KD), 0.1),
        "gla_g_norm": 1.0 + nrm(ks[14], (DEPTH, GLA_DV), 0.05),
        "na_rpb": nrm(ks[15], (DEPTH, NA_HEADS, 2 * NA_WIN_H - 1, 2 * NA_WIN_W - 1), 0.1),
        "w_a_out": nrm(ks[16], (DEPTH, A_WIDTH, D), A_WIDTH ** -0.5),
        "w_g_out": nrm(ks[17], (DEPTH, GLA_VD, D), GLA_VD ** -0.5),
        "w_n_out": nrm(ks[18], (DEPTH, NA_D, D), NA_D ** -0.5),
        "w_o": nrm(ks[19], (DEPTH, D, D), D ** -0.5),
        "w_ffn1": nrm(ks[20], (DEPTH, D, FFN_HIDDEN), D ** -0.5),
        "w_ffn3": nrm(ks[21], (DEPTH, D, FFN_HIDDEN), D ** -0.5),
        "w_ffn2": nrm(ks[22], (DEPTH, FFN_HIDDEN, D), FFN_HIDDEN ** -0.5),
        "g_final": 1.0 + nrm(ks[23], (D,), 0.05),
    }


def reference(x, c, ctx, c_ctx, w_ada, b_ada, g_mix, g_ffn, w_in, conv_w,
              gla_wa_f, gla_ba_f, gla_wa_b, gla_ba_b, gla_g_norm, na_rpb,
              w_a_out, w_g_out, w_n_out, w_o, w_ffn1, w_ffn3, w_ffn2, g_final):
    S = x.shape[1]
    rope = axial_rope(S, GLA_DK, x.dtype)
    h, hc = x, ctx
    for l in range(DEPTH):
        need_ctx = l < DEPTH - 1
        sm, scm, gm, sf, scf, gf = adaln(c, w_ada[l], b_ada[l])
        sm_c, scm_c, gm_c, sf_c, scf_c, gf_c = adaln(c_ctx[None, :], w_ada[l], b_ada[l])
        n = rmsnorm(h, g_mix[l]) * (1 + scm) + sm
        nc = rmsnorm(hc, g_mix[l]) * (1 + scm_c) + sm_c
        y, yc = token_mixers(n, nc, w_in[l], conv_w[l], gla_wa_f[l], gla_ba_f[l], gla_wa_b[l], gla_ba_b[l],
                             gla_g_norm[l], na_rpb[l], w_a_out[l], w_g_out[l], w_n_out[l], w_o[l],
                             rope, need_ctx)
        h = h + gm * y
        n = rmsnorm(h, g_ffn[l]) * (1 + scf) + sf
        h = h + gf * swiglu(n, w_ffn1[l], w_ffn3[l], w_ffn2[l])
        if need_ctx:
            hc = hc + gm_c * yc
            nc = rmsnorm(hc, g_ffn[l]) * (1 + scf_c) + sf_c
            hc = hc + gf_c * swiglu(nc, w_ffn1[l], w_ffn3[l], w_ffn2[l])
    return rmsnorm(h, g_final)
```

```python
import functools

import numpy as np
import jax
import jax.numpy as jnp
from jax import lax
from jax.experimental import pallas as pl
from jax.experimental.pallas import tpu as pltpu

GRID_W = 64
RMS_EPS = 1e-6
ROPE_BASE = 10000.0
N_BRANCH = 3
A_WIDTH = 512
GLA_HEADS = 6
GLA_DK = 64
GLA_DV = 128
GLA_KD = GLA_HEADS * GLA_DK
GLA_VD = GLA_HEADS * GLA_DV
GLA_LOWRANK = 16
GLA_TAU = 16.0
GLA_CHUNK = 64
GLA_PAIRS = GLA_HEADS // 2
NA_HEADS = 6
NA_DH = 128
NA_D = NA_HEADS * NA_DH
NA_WIN_H = 8
NA_WIN_W = 16
NA_ROWS_PER_STEP = 16
MASK_VALUE = -1e30

LANE = 128
SUBLANE = 8
VMEM_PHYSICAL = 64 * 1024 * 1024
VMEM_INTERNAL = 12 * 1024 * 1024
MOD_ROWS = 8

BF16 = jnp.bfloat16
F32 = jnp.float32


def _params(semantics, block_bytes):
    limit = min(int(block_bytes) + VMEM_INTERNAL, VMEM_PHYSICAL - 4 * 1024 * 1024)
    return pltpu.CompilerParams(dimension_semantics=semantics, vmem_limit_bytes=limit)


def _nbytes(shape, dtype):
    return int(np.prod(shape)) * jnp.dtype(dtype).itemsize


def _layout(d_model):
    off = {}
    pos = 0
    for name, width, block in (("gate", N_BRANCH * d_model, N_BRANCH * d_model),
                               ("a_in", A_WIDTH, A_WIDTH), ("a_b", A_WIDTH, A_WIDTH), ("a_c", A_WIDTH, A_WIDTH),
                               ("gq", GLA_KD, GLA_KD), ("gk", GLA_KD, GLA_KD), ("gv", GLA_VD, GLA_VD),
                               ("gr", GLA_VD, GLA_VD), ("gab", LANE, LANE),
                               ("nq", NA_D, NA_DH), ("nk", NA_D, NA_DH), ("nv", NA_D, NA_DH)):
        assert pos % block == 0, (name, pos, block)
        off[name] = pos
        pos += width
    return off, pos


def _adaln_kernel(c_ref, w_ref, b_ref, o_ref):
    cv = c_ref[...]
    s = cv * jax.nn.sigmoid(cv)
    o_ref[0] = jnp.dot(s, w_ref[0], preferred_element_type=F32) + b_ref[0]


def _adaln(cvec, w_ada, b_ada):
    depth, d, n = w_ada.shape
    tn = 1024 if n % 1024 == 0 else n
    blocks = 2 * (_nbytes((d, tn), F32) + 2 * _nbytes((MOD_ROWS, tn), F32)) + _nbytes((MOD_ROWS, d), F32)
    return pl.pallas_call(
        _adaln_kernel,
        grid=(depth, n // tn),
        in_specs=[pl.BlockSpec((MOD_ROWS, d), lambda l, j: (0, 0)),
                  pl.BlockSpec((1, d, tn), lambda l, j: (l, 0, j)),
                  pl.BlockSpec((1, 1, tn), lambda l, j: (l, 0, j))],
        out_specs=pl.BlockSpec((1, MOD_ROWS, tn), lambda l, j: (l, 0, j)),
        out_shape=jax.ShapeDtypeStruct((depth, MOD_ROWS, n), F32),
        compiler_params=_params(("arbitrary", "arbitrary"), blocks),
        name="adaln",
    )(cvec, w_ada, b_ada.reshape(depth, 1, n))


def _mod_norm(x, g, shift, scale):
    y = x * lax.rsqrt(jnp.mean(x * x, axis=-1, keepdims=True) + RMS_EPS)
    return (y * g) * (1.0 + scale) + shift


def _inproj_kernel(x_ref, mod_ref, g_ref, w_ref, o_ref, n_scr):
    @pl.when(pl.program_id(1) == 0)
    def _():
        m = mod_ref[0]
        n_scr[...] = _mod_norm(x_ref[...], g_ref[...], m[0:1], m[1:2]).astype(BF16)

    o_ref[...] = jnp.dot(n_scr[...], w_ref[...], preferred_element_type=F32)


def _inproj(h, mod_l, g, w, group_of_tile, tm, tn):
    m_rows, d = h.shape
    n = w.shape[1]
    blocks = (2 * (_nbytes((tm, d), F32) + _nbytes((d, tn), BF16) + _nbytes((tm, tn), F32))
              + _nbytes((tm, d), BF16) + 2 * _nbytes((8, d), F32))
    return pl.pallas_call(
        _inproj_kernel,
        grid=(m_rows // tm, n // tn),
        in_specs=[pl.BlockSpec((tm, d), lambda i, j: (i, 0)),
                  pl.BlockSpec((1, 6, d), lambda i, j: (group_of_tile(i), 0, 0)),
                  pl.BlockSpec((1, d), lambda i, j: (0, 0)),
                  pl.BlockSpec((d, tn), lambda i, j: (0, j))],
        out_specs=pl.BlockSpec((tm, tn), lambda i, j: (i, j)),
        out_shape=jax.ShapeDtypeStruct((m_rows, n), F32),
        scratch_shapes=[pltpu.VMEM((tm, d), BF16)],
        compiler_params=_params(("parallel", "arbitrary"), blocks),
        name="inproj",
    )(h, mod_l, g.reshape(1, d), w)


def _cumsum_rows(x, length, reverse):
    row = lax.broadcasted_iota(jnp.int32, x.shape, 0)
    step = 1
    while step < length:
        if reverse:
            x = x + jnp.where(row < length - step, pltpu.roll(x, length - step, axis=0), 0.0)
        else:
            x = x + jnp.where(row >= step, pltpu.roll(x, step, axis=0), 0.0)
        step *= 2
    return x


def _rope_pairs(x, cos, sin):
    lane = lax.broadcasted_iota(jnp.int32, cos.shape, 1)
    first_half = (lane % GLA_DK) < (GLA_DK // 2)
    out = []
    for p in range(GLA_PAIRS):
        xs = x[:, p * LANE:(p + 1) * LANE]
        swapped = jnp.where(first_half, pltpu.roll(xs, LANE - GLA_DK // 2, axis=1), pltpu.roll(xs, GLA_DK // 2, axis=1))
        out.append(xs * cos + swapped * sin)
    return jnp.concatenate(out, axis=1)


def _gla_direction(q, k, v, ab, wa, ba, cos, sin, s_scr, reverse):
    length = q.shape[0]
    z = jnp.dot(ab.astype(BF16), wa, preferred_element_type=F32) + ba
    la = (jnp.minimum(z, 0.0) - jnp.log1p(jnp.exp(-jnp.abs(z)))) * (1.0 / GLA_TAU)
    b = _cumsum_rows(la, length, reverse)
    if reverse:
        mid, last = b[length // 2:length // 2 + 1], b[0:1]
    else:
        mid, last = b[length // 2 - 1:length // 2], b[length - 1:length]
    q = q * (GLA_DK ** -0.5)
    if cos is not None:
        q = _rope_pairs(q, cos, sin)
        k = _rope_pairs(k, cos, sin)
    q_mid = q * jnp.exp(b - mid)
    k_mid = k * jnp.exp(mid - b)
    q_state = q * jnp.exp(b)
    k_last = k * jnp.exp(last - b)
    state_decay = jnp.exp(last)
    row = lax.broadcasted_iota(jnp.int32, (length, length), 0)
    col = lax.broadcasted_iota(jnp.int32, (length, length), 1)
    visible = (row <= col) if reverse else (row >= col)
    lane = lax.broadcasted_iota(jnp.int32, (length, LANE), 1)
    nt = (((1,), (1,)), ((), ()))
    tn = (((0,), (0,)), ((), ()))
    outs = []
    for p in range(GLA_PAIRS):
        sl = slice(p * LANE, (p + 1) * LANE)
        k_mid_b = k_mid[:, sl].astype(BF16)
        for hh in range(2):
            head_lanes = (lane < GLA_DK) if hh == 0 else (lane >= GLA_DK)
            head = 2 * p + hh
            q_mid_h = jnp.where(head_lanes, q_mid[:, sl], 0.0).astype(BF16)
            att = lax.dot_general(q_mid_h, k_mid_b, nt, preferred_element_type=F32)
            att = jnp.where(visible, att, 0.0)
            v_h = v[:, head * GLA_DV:(head + 1) * GLA_DV].astype(BF16)
            o = jnp.dot(att.astype(BF16), v_h, preferred_element_type=F32)
            q_state_h = jnp.where(head_lanes, q_state[:, sl], 0.0).astype(BF16)
            s_h = s_scr[p, hh * GLA_DV:(hh + 1) * GLA_DV, :].astype(BF16)
            o = o + lax.dot_general(q_state_h, s_h, nt, preferred_element_type=F32)
            outs.append(o)
        v_pair = v[:, 2 * p * GLA_DV:(2 * p + 2) * GLA_DV].astype(BF16)
        update = lax.dot_general(v_pair, k_last[:, sl].astype(BF16), tn, preferred_element_type=F32)
        s_scr[p] = s_scr[p] * state_decay[:, sl] + update
    return jnp.concatenate(outs, axis=1)


def _gla_kernel(*refs, use_rope):
    qf, kf, vf, abf, qb, kb, vb, abb = refs[:8]
    pos = 8
    if use_rope:
        cosf, sinf, cosb, sinb = refs[pos:pos + 4]
        pos += 4
    waf, baf, wab, bab, s0f, s0b, of_ref, ob_ref, sf_ref, sb_ref, sf_scr, sb_scr = refs[pos:]
    step = pl.program_id(1)

    @pl.when(step == 0)
    def _():
        sf_scr[...] = s0f[0]
        sb_scr[...] = s0b[0]

    cf = (cosf[...], sinf[...]) if use_rope else (None, None)
    cb = (cosb[...], sinb[...]) if use_rope else (None, None)
    of_ref[0] = _gla_direction(qf[0], kf[0], vf[0], abf[0], waf[...], baf[...], cf[0], cf[1], sf_scr, False)
    ob_ref[0] = _gla_direction(qb[0], kb[0], vb[0], abb[0], wab[...], bab[...], cb[0], cb[1], sb_scr, True)

    @pl.when(step == pl.num_programs(1) - 1)
    def _():
        sf_ref[0] = sf_scr[...]
        sb_ref[0] = sb_scr[...]


def _gla(p3, off, waf, baf, wab, bab, s0f, s0b, rope):
    bsz, t, _ = p3.shape
    length = GLA_CHUNK
    n = t // length
    fwd = lambda blk: (lambda b, i: (b, i, blk))
    bwd = lambda blk: (lambda b, i: (b, n - 1 - i, blk))
    q_blk, k_blk = off["gq"] // GLA_KD, off["gk"] // GLA_KD
    v_blk, ab_blk = off["gv"] // GLA_VD, off["gab"] // LANE
    seq_specs = lambda m: [pl.BlockSpec((1, length, GLA_KD), m(q_blk)), pl.BlockSpec((1, length, GLA_KD), m(k_blk)),
                           pl.BlockSpec((1, length, GLA_VD), m(v_blk)), pl.BlockSpec((1, length, LANE), m(ab_blk))]
    in_specs = seq_specs(fwd) + seq_specs(bwd)
    args = [p3] * 8
    if rope is not None:
        in_specs += [pl.BlockSpec((length, LANE), lambda b, i: (i, 0))] * 2
        in_specs += [pl.BlockSpec((length, LANE), lambda b, i: (n - 1 - i, 0))] * 2
        args += [rope[0], rope[1], rope[0], rope[1]]
    const2 = lambda b, i: (0, 0)
    state_spec = pl.BlockSpec((1, GLA_PAIRS, 2 * GLA_DV, LANE), lambda b, i: (b, 0, 0, 0))
    in_specs += [pl.BlockSpec((LANE, GLA_KD), const2), pl.BlockSpec((1, GLA_KD), const2),
                 pl.BlockSpec((LANE, GLA_KD), const2), pl.BlockSpec((1, GLA_KD), const2),
                 state_spec, state_spec]
    args += [waf, baf, wab, bab, s0f, s0b]
    state_shape = jax.ShapeDtypeStruct((bsz, GLA_PAIRS, 2 * GLA_DV, LANE), F32)
    out_shape = jax.ShapeDtypeStruct((bsz, t, GLA_VD), F32)
    blocks = 2 * (2 * (2 * _nbytes((length, GLA_KD), F32) + 2 * _nbytes((length, GLA_VD), F32) + 5 * _nbytes((length, LANE), F32))
                  + 4 * _nbytes((GLA_PAIRS, 2 * GLA_DV, LANE), F32) + 2 * _nbytes((LANE, GLA_KD), F32))
    blocks += 2 * _nbytes((GLA_PAIRS, 2 * GLA_DV, LANE), F32)
    return pl.pallas_call(
        functools.partial(_gla_kernel, use_rope=rope is not None),
        grid=(bsz, n),
        in_specs=in_specs,
        out_specs=[pl.BlockSpec((1, length, GLA_VD), fwd(0)), pl.BlockSpec((1, length, GLA_VD), bwd(0)),
                   state_spec, state_spec],
        out_shape=[out_shape, out_shape, state_shape, state_shape],
        scratch_shapes=[pltpu.VMEM((GLA_PAIRS, 2 * GLA_DV, LANE), F32)] * 2,
        compiler_params=_params(("arbitrary", "arbitrary"), blocks),
        name="gla",
    )(*args)


def _na_kernel(q_ref, k_ref, v_ref, kc_ref, vc_ref, bias_ref, o_ref, *, rows):
    nt = (((1,), (1,)), ((), ()))
    kc = kc_ref[0].astype(BF16)
    vc = vc_ref[0].astype(BF16)
    row0 = pl.program_id(2) * NA_ROWS_PER_STEP
    n_loc = NA_WIN_H * GRID_W

    def body(rr, carry):
        r = row0 + rr
        r_start = jnp.clip(r - NA_WIN_H // 2, 0, rows - NA_WIN_H)
        q_off = pl.multiple_of(rr * GRID_W, GRID_W)
        k_off = pl.multiple_of(r_start * GRID_W, GRID_W)
        q = (q_ref[0, pl.ds(q_off, GRID_W), :] * (NA_DH ** -0.5)).astype(BF16)
        kr = k_ref[0, pl.ds(k_off, n_loc), :].astype(BF16)
        vr = v_ref[0, pl.ds(k_off, n_loc), :].astype(BF16)
        s_loc = lax.dot_general(q, kr, nt, preferred_element_type=F32) + bias_ref[0, r - r_start]
        s_ctx = lax.dot_general(q, kc, nt, preferred_element_type=F32)
        m = jnp.maximum(jnp.max(s_loc, axis=-1, keepdims=True), jnp.max(s_ctx, axis=-1, keepdims=True))
        p_loc = jnp.exp(s_loc - m)
        p_ctx = jnp.exp(s_ctx - m)
        denom = jnp.sum(p_loc, axis=-1, keepdims=True) + jnp.sum(p_ctx, axis=-1, keepdims=True)
        o = (jnp.dot(p_loc.astype(BF16), vr, preferred_element_type=F32)
             + jnp.dot(p_ctx.astype(BF16), vc, preferred_element_type=F32))
        o_ref[0, pl.ds(q_off, GRID_W), :] = o / denom
        return carry

    lax.fori_loop(0, NA_ROWS_PER_STEP, body, 0)


def _na(p3, pc3, off, bias_tab):
    bsz, t, _ = p3.shape
    ct = pc3.shape[1]
    rows = t // GRID_W
    rb = NA_ROWS_PER_STEP
    tq = rb * GRID_W
    q0, k0, v0 = off["nq"] // NA_DH, off["nk"] // NA_DH, off["nv"] // NA_DH
    n_loc = NA_WIN_H * GRID_W
    blocks = 2 * (2 * _nbytes((tq, NA_DH), F32) + 2 * _nbytes((t, NA_DH), F32) + 2 * _nbytes((ct, NA_DH), F32)
                  + _nbytes((NA_WIN_H, GRID_W, n_loc), F32))
    return pl.pallas_call(
        functools.partial(_na_kernel, rows=rows),
        grid=(bsz, NA_HEADS, rows // rb),
        in_specs=[pl.BlockSpec((1, tq, NA_DH), lambda b, h, r: (b, r, q0 + h)),
                  pl.BlockSpec((1, t, NA_DH), lambda b, h, r: (b, 0, k0 + h)),
                  pl.BlockSpec((1, t, NA_DH), lambda b, h, r: (b, 0, v0 + h)),
                  pl.BlockSpec((1, ct, NA_DH), lambda b, h, r: (b, 0, k0 + h)),
                  pl.BlockSpec((1, ct, NA_DH), lambda b, h, r: (b, 0, v0 + h)),
                  pl.BlockSpec((1, NA_WIN_H, GRID_W, n_loc), lambda b, h, r: (h, 0, 0, 0))],
        out_specs=pl.BlockSpec((1, tq, NA_DH), lambda b, h, r: (b, r, h)),
        out_shape=jax.ShapeDtypeStruct((bsz, t, NA_D), F32),
        compiler_params=_params(("arbitrary", "arbitrary", "arbitrary"), blocks),
        name="natten",
    )(p3, p3, p3, pc3, pc3, bias_tab)


def _na_bias_table(rpb):
    col = np.arange(GRID_W)
    col_start = np.clip(col - NA_WIN_W // 2, 0, GRID_W - NA_WIN_W)
    key_col = np.arange(GRID_W)
    valid = (key_col[None, :] >= col_start[:, None]) & (key_col[None, :] < col_start[:, None] + NA_WIN_W)
    col_off = np.clip(key_col[None, :] - col[:, None] + (NA_WIN_W - 1), 0, 2 * NA_WIN_W - 2)
    dv = np.arange(NA_WIN_H)
    row_off = np.arange(NA_WIN_H)[None, :] - dv[:, None] + (NA_WIN_H - 1)
    gathered = rpb[:, row_off[:, None, :, None], col_off[None, :, None, :]]
    tab = jnp.where(valid[None, None, :, None, :], gathered.astype(F32), MASK_VALUE)
    return tab.reshape(rpb.shape[0], NA_WIN_H, GRID_W, NA_WIN_H * GRID_W)


def _ctx_attn_kernel(q_ref, k_ref, v_ref, o_ref):
    q = (q_ref[0] * (NA_DH ** -0.5)).astype(BF16)
    s = lax.dot_general(q, k_ref[0].astype(BF16), (((1,), (1,)), ((), ())), preferred_element_type=F32)
    p = jnp.exp(s - jnp.max(s, axis=-1, keepdims=True))
    o = jnp.dot(p.astype(BF16), v_ref[0].astype(BF16), preferred_element_type=F32)
    o_ref[0] = o / jnp.sum(p, axis=-1, keepdims=True)


def _ctx_attn(pc3, off):
    bsz, ct, _ = pc3.shape
    q0, k0, v0 = off["nq"] // NA_DH, off["nk"] // NA_DH, off["nv"] // NA_DH
    spec = lambda c0: pl.BlockSpec((1, ct, NA_DH), lambda b, h: (b, 0, c0 + h))
    return pl.pallas_call(
        _ctx_attn_kernel,
        grid=(bsz, NA_HEADS),
        in_specs=[spec(q0), spec(k0), spec(v0)],
        out_specs=spec(0),
        out_shape=jax.ShapeDtypeStruct((bsz, ct, NA_D), F32),
        compiler_params=_params(("arbitrary", "arbitrary"), 8 * _nbytes((ct, NA_DH), F32)),
        name="ctx_attn",
    )(pc3, pc3, pc3)


def _branch_kernel(gate_ref, ain_ref, ab_ref, ac_ref, ainp_ref, acp_ref, ainn_ref, acn_ref,
                   of_ref, ob_ref, gr_ref, na_ref, cw_ref, gn_ref, wa_ref, wg_ref, wn_ref, y_ref, *, tiles_per_seq):
    tm = ain_ref.shape[0]
    d = y_ref.shape[1]
    i = pl.program_id(0)
    tile_in_seq = i % tiles_per_seq
    u = ac_ref[...] * ain_ref[...]
    u_before = jnp.where(tile_in_seq == 0, 0.0, acp_ref[SUBLANE - 1:SUBLANE, :] * ainp_ref[SUBLANE - 1:SUBLANE, :])
    u_after = jnp.where(tile_in_seq == tiles_per_seq - 1, 0.0, acn_ref[0:1, :] * ainn_ref[0:1, :])
    row = lax.broadcasted_iota(jnp.int32, u.shape, 0)
    u_prev = jnp.where(row == 0, u_before, pltpu.roll(u, 1, axis=0))
    u_next = jnp.where(row == tm - 1, u_after, pltpu.roll(u, tm - 1, axis=0))
    cw = cw_ref[...]
    a = ab_ref[...] * (cw[0:1] * u_prev + cw[1:2] * u + cw[2:3] * u_next)
    o = of_ref[...] + ob_ref[...]
    gn = gn_ref[...]
    normed = []
    for h in range(GLA_HEADS):
        oh = o[:, h * GLA_DV:(h + 1) * GLA_DV]
        normed.append(oh * lax.rsqrt(jnp.mean(oh * oh, axis=-1, keepdims=True) + RMS_EPS) * gn)
    r = gr_ref[...]
    g = jnp.concatenate(normed, axis=1) * (r * jax.nn.sigmoid(r))
    y = jax.nn.sigmoid(gate_ref[:, 0:d]) * jnp.dot(a.astype(BF16), wa_ref[...], preferred_element_type=F32)
    y = y + jax.nn.sigmoid(gate_ref[:, d:2 * d]) * jnp.dot(g.astype(BF16), wg_ref[...], preferred_element_type=F32)
    y = y + jax.nn.sigmoid(gate_ref[:, 2 * d:3 * d]) * jnp.dot(na_ref[...].astype(BF16), wn_ref[...],
                                                                 preferred_element_type=F32)
    y_ref[...] = y.astype(BF16)


def _branch(p, off, og_f, og_b, na, conv_w, g_norm, wa, wg, wn, seq_len, tm):
    m_rows = p.shape[0]
    d = wa.shape[1]
    tiles_per_seq = seq_len // tm
    halo = tm // SUBLANE
    n_halo = m_rows // SUBLANE
    cur = lambda blk: (lambda i: (i, blk))
    prev = lambda blk: (lambda i: (jnp.maximum(i * halo - 1, 0), blk))
    nxt = lambda blk: (lambda i: (jnp.minimum((i + 1) * halo, n_halo - 1), blk))
    a_in, a_b, a_c = off["a_in"] // A_WIDTH, off["a_b"] // A_WIDTH, off["a_c"] // A_WIDTH
    const = lambda i: (0, 0)
    in_specs = [pl.BlockSpec((tm, N_BRANCH * d), cur(off["gate"] // (N_BRANCH * d))),
                pl.BlockSpec((tm, A_WIDTH), cur(a_in)), pl.BlockSpec((tm, A_WIDTH), cur(a_b)),
                pl.BlockSpec((tm, A_WIDTH), cur(a_c)),
                pl.BlockSpec((SUBLANE, A_WIDTH), prev(a_in)), pl.BlockSpec((SUBLANE, A_WIDTH), prev(a_c)),
                pl.BlockSpec((SUBLANE, A_WIDTH), nxt(a_in)), pl.BlockSpec((SUBLANE, A_WIDTH), nxt(a_c)),
                pl.BlockSpec((tm, GLA_VD), cur(0)), pl.BlockSpec((tm, GLA_VD), cur(0)),
                pl.BlockSpec((tm, GLA_VD), cur(off["gr"] // GLA_VD)),
                pl.BlockSpec((tm, NA_D), cur(0)),
                pl.BlockSpec((3, A_WIDTH), const), pl.BlockSpec((1, GLA_DV), const),
                pl.BlockSpec((A_WIDTH, d), const), pl.BlockSpec((GLA_VD, d), const), pl.BlockSpec((NA_D, d), const)]
    blocks = 2 * (_nbytes((tm, N_BRANCH * d), F32) + 3 * _nbytes((tm, A_WIDTH), F32) + 4 * _nbytes((tm, GLA_VD), F32)
                  + _nbytes((A_WIDTH + GLA_VD + NA_D, d), BF16) + _nbytes((tm, d), BF16)) + 6 * _nbytes((tm, d), F32)
    return pl.pallas_call(
        functools.partial(_branch_kernel, tiles_per_seq=tiles_per_seq),
        grid=(m_rows // tm,),
        in_specs=in_specs,
        out_specs=pl.BlockSpec((tm, d), cur(0)),
        out_shape=jax.ShapeDtypeStruct((m_rows, d), BF16),
        compiler_params=_params(("arbitrary",), blocks),
        name="branch_merge",
    )(p, p, p, p, p, p, p, p, og_f, og_b, p, na, conv_w, g_norm.reshape(1, GLA_DV), wa, wg, wn)


def _proj_kernel(y_ref, w_ref, h_ref, gm_ref, o_ref):
    o_ref[...] = h_ref[...] + gm_ref[0] * jnp.dot(y_ref[...], w_ref[...], preferred_element_type=F32)


def _proj_residual(y, w, h, gate, group_of_tile, tm, tn):
    m_rows, d = h.shape
    blocks = 2 * (_nbytes((tm, d), BF16) + _nbytes((d, tn), BF16) + 2 * _nbytes((tm, tn), F32)) + _nbytes((tm, tn), F32)
    return pl.pallas_call(
        _proj_kernel,
        grid=(m_rows // tm, d // tn),
        in_specs=[pl.BlockSpec((tm, d), lambda i, j: (i, 0)),
                  pl.BlockSpec((d, tn), lambda i, j: (0, j)),
                  pl.BlockSpec((tm, tn), lambda i, j: (i, j)),
                  pl.BlockSpec((1, 1, tn), lambda i, j: (group_of_tile(i), 0, j))],
        out_specs=pl.BlockSpec((tm, tn), lambda i, j: (i, j)),
        out_shape=jax.ShapeDtypeStruct((m_rows, d), F32),
        compiler_params=_params(("parallel", "arbitrary"), blocks),
        name="out_proj",
    )(y, w, h, gate)


def _ffn_kernel(x_ref, mod_ref, g_ref, w1_ref, w3_ref, w2_ref, gfin_ref, o_ref, n_scr, acc_scr, *, final_norm):
    j = pl.program_id(1)

    @pl.when(j == 0)
    def _():
        m = mod_ref[0]
        n_scr[...] = _mod_norm(x_ref[...], g_ref[...], m[3:4], m[4:5]).astype(BF16)
        acc_scr[...] = jnp.zeros_like(acc_scr)

    n = n_scr[...]
    a = jnp.dot(n, w1_ref[...], preferred_element_type=F32)
    b = jnp.dot(n, w3_ref[...], preferred_element_type=F32)
    u = (a * jax.nn.sigmoid(a)) * b
    acc_scr[...] += jnp.dot(u.astype(BF16), w2_ref[...], preferred_element_type=F32)

    @pl.when(j == pl.num_programs(1) - 1)
    def _():
        out = x_ref[...] + mod_ref[0][5:6] * acc_scr[...]
        if final_norm:
            out = out * lax.rsqrt(jnp.mean(out * out, axis=-1, keepdims=True) + RMS_EPS) * gfin_ref[...]
        o_ref[...] = out


def _ffn(h, mod_l, g, w1, w3, w2, g_final, group_of_tile, tm, tf, final_norm):
    m_rows, d = h.shape
    f = w1.shape[1]
    blocks = (4 * _nbytes((tm, d), F32) + 6 * _nbytes((d, tf), BF16) + _nbytes((tm, d), BF16) + _nbytes((tm, d), F32)
              + 4 * _nbytes((tm, tf), F32))
    return pl.pallas_call(
        functools.partial(_ffn_kernel, final_norm=final_norm),
        grid=(m_rows // tm, f // tf),
        in_specs=[pl.BlockSpec((tm, d), lambda i, j: (i, 0)),
                  pl.BlockSpec((1, 6, d), lambda i, j: (group_of_tile(i), 0, 0)),
                  pl.BlockSpec((1, d), lambda i, j: (0, 0)),
                  pl.BlockSpec((d, tf), lambda i, j: (0, j)),
                  pl.BlockSpec((d, tf), lambda i, j: (0, j)),
                  pl.BlockSpec((tf, d), lambda i, j: (j, 0)),
                  pl.BlockSpec((1, d), lambda i, j: (0, 0))],
        out_specs=pl.BlockSpec((tm, d), lambda i, j: (i, 0)),
        out_shape=jax.ShapeDtypeStruct((m_rows, d), F32),
        scratch_shapes=[pltpu.VMEM((tm, d), BF16), pltpu.VMEM((tm, d), F32)],
        compiler_params=_params(("parallel", "arbitrary"), blocks),
        name="ffn",
    )(h, mod_l, g.reshape(1, d), w1, w3, w2, g_final.reshape(1, d))


def _largest_tile(total, cap, align):
    t = min(total, cap)
    while total % t or t % align:
        t -= align
    return t


def _rope_tables(seq_len):
    pos = jnp.arange(seq_len, dtype=jnp.int32)
    rowf = (pos // GRID_W).astype(F32)
    colf = (pos % GRID_W).astype(F32)
    per_axis = GLA_DK // 2
    inv = ROPE_BASE ** (-jnp.arange(0, per_axis, 2, dtype=F32) / per_axis)
    ang = jnp.concatenate([rowf[:, None] * inv, colf[:, None] * inv], axis=-1)
    cos, sin = jnp.cos(ang), jnp.sin(ang)
    reps = LANE // GLA_DK
    return jnp.tile(cos, (1, 2 * reps)), jnp.tile(jnp.concatenate([-sin, sin], axis=-1), (1, reps))


def _pack_w_in(w_in_l, off, n_pad):
    d = w_in_l.shape[0]
    sizes = (A_WIDTH, A_WIDTH, A_WIDTH, GLA_KD, GLA_KD, GLA_VD, GLA_VD, GLA_LOWRANK, GLA_LOWRANK, NA_D, NA_D, NA_D,
             N_BRANCH * d)
    names = ("a_in", "a_b", "a_c", "gq", "gk", "gv", "gr", "gaf", "gab_", "nq", "nk", "nv", "gate")
    starts = np.concatenate([[0], np.cumsum(sizes)])
    out = jnp.zeros((d, n_pad), BF16)
    for name, s, width in zip(names, starts[:-1], sizes):
        dst = {"gaf": off["gab"], "gab_": off["gab"] + GLA_LOWRANK}.get(name, off.get(name))
        out = lax.dynamic_update_slice(out, w_in_l[:, int(s):int(s) + width].astype(BF16), (0, int(dst)))
    return out


def _pad_lowrank(wa, col0):
    return jnp.zeros((LANE, GLA_KD), BF16).at[col0:col0 + GLA_LOWRANK].set(wa.astype(BF16))


def kernel(x, c, ctx, c_ctx, w_ada, b_ada, g_mix, g_ffn, w_in, conv_w, gla_wa_f, gla_ba_f, gla_wa_b, gla_ba_b,
           gla_g_norm, na_rpb, w_a_out, w_g_out, w_n_out, w_o, w_ffn1, w_ffn3, w_ffn2, g_final):
    bsz, seq, d = x.shape
    ct = ctx.shape[1]
    depth = w_ada.shape[0]
    f = w_ffn1.shape[2]
    assert bsz + 1 <= MOD_ROWS and seq % GRID_W == 0 and seq % GLA_CHUNK == 0 and ct % GLA_CHUNK == 0
    assert (seq // GRID_W) % NA_ROWS_PER_STEP == 0
    off, n_used = _layout(d)
    tn_in = 10 * LANE
    n_pad = -(-n_used // tn_in) * tn_in

    tm_big = _largest_tile(seq, 1024, SUBLANE)
    tm_ffn = _largest_tile(seq, 512, SUBLANE)
    tm_branch = _largest_tile(seq, 256, SUBLANE)
    tn_proj = _largest_tile(d, 512, LANE)
    tf = _largest_tile(f, 512, LANE)
    lat_group = lambda tm: (lambda i: i // (seq // tm))
    ctx_group = lambda i: bsz

    cvec = jnp.zeros((MOD_ROWS, d), F32).at[:bsz].set(c).at[bsz].set(c_ctx)
    mod = _adaln(cvec, w_ada, b_ada).reshape(depth, MOD_ROWS, 6, d)
    rope = _rope_tables(seq)
    zero_state = jnp.zeros((bsz, GLA_PAIRS, 2 * GLA_DV, LANE), F32)

    h = x.reshape(bsz * seq, d)
    hc = ctx.reshape(bsz * ct, d)
    for l in range(depth):
        need_ctx = l < depth - 1
        last = l == depth - 1
        mod_l = mod[l]
        gm = mod_l[:, 2, :].reshape(MOD_ROWS, 1, d)
        w_in_l = _pack_w_in(w_in[l], off, n_pad)
        waf, wab = _pad_lowrank(gla_wa_f[l], 0), _pad_lowrank(gla_wa_b[l], GLA_LOWRANK)
        baf, bab = gla_ba_f[l].reshape(1, GLA_KD), gla_ba_b[l].reshape(1, GLA_KD)
        wa_o, wg_o, wn_o = w_a_out[l].astype(BF16), w_g_out[l].astype(BF16), w_n_out[l].astype(BF16)
        wo, w1, w3, w2 = w_o[l].astype(BF16), w_ffn1[l].astype(BF16), w_ffn3[l].astype(BF16), w_ffn2[l].astype(BF16)
        bias_tab = _na_bias_table(na_rpb[l])

        p = _inproj(h, mod_l, g_mix[l], w_in_l, lat_group(tm_big), tm_big, tn_in)
        pc = _inproj(hc, mod_l, g_mix[l], w_in_l, ctx_group, ct, tn_in)
        p3 = p.reshape(bsz, seq, n_pad)
        pc3 = pc.reshape(bsz, ct, n_pad)

        ogc_f, ogc_b, s_f, s_b = _gla(pc3, off, waf, baf, wab, bab, zero_state, zero_state, None)
        og_f, og_b, _, _ = _gla(p3, off, waf, baf, wab, bab, s_f, s_b, rope)
        na = _na(p3, pc3, off, bias_tab)

        y = _branch(p, off, og_f.reshape(bsz * seq, GLA_VD), og_b.reshape(bsz * seq, GLA_VD),
                    na.reshape(bsz * seq, NA_D), conv_w[l], gla_g_norm[l], wa_o, wg_o, wn_o, seq, tm_branch)
        h = _proj_residual(y, wo, h, gm, lat_group(tm_big), tm_big, tn_proj)
        h = _ffn(h, mod_l, g_ffn[l], w1, w3, w2, g_final, lat_group(tm_ffn), tm_ffn, tf, last)
        if need_ctx:
            na_c = _ctx_attn(pc3, off)
            yc = _branch(pc, off, ogc_f.reshape(bsz * ct, GLA_VD), ogc_b.reshape(bsz * ct, GLA_VD),
                         na_c.reshape(bsz * ct, NA_D), conv_w[l], gla_g_norm[l], wa_o, wg_o, wn_o, ct, ct)
            hc = _proj_residual(yc, wo, hc, gm, ctx_group, ct, tn_proj)
            hc = _ffn(hc, mod_l, g_ffn[l], w1, w3, w2, g_final, ctx_group, ct, tf, False)
    return h.reshape(bsz, seq, d)
```

```python
import functools

import numpy as np
import jax
import jax.numpy as jnp
from jax import lax
from jax.experimental import pallas as pl
from jax.experimental.pallas import tpu as pltpu

GRID_W = 64
RMS_EPS = 1e-6
ROPE_BASE = 10000.0
N_BRANCH = 3
A_WIDTH = 512
GLA_HEADS = 6
GLA_DK = 64
GLA_DV = 128
GLA_KD = GLA_HEADS * GLA_DK
GLA_VD = GLA_HEADS * GLA_DV
GLA_LOWRANK = 16
GLA_TAU = 16.0
GLA_CHUNK = 64
GLA_PAIRS = GLA_HEADS // 2
NA_HEADS = 6
NA_DH = 128
NA_D = NA_HEADS * NA_DH
NA_WIN_H = 8
NA_WIN_W = 16
NA_GROUP = 4
NA_KEY_ROWS = NA_GROUP + NA_WIN_H - 1
NA_GROUPS_PER_STEP = 4
MASK_VALUE = -1e30

LANE = 128
SUBLANE = 8
VMEM_PHYSICAL = 64 * 1024 * 1024
VMEM_INTERNAL = 12 * 1024 * 1024
MOD_ROWS = 8

BF16 = jnp.bfloat16
F32 = jnp.float32


def _params(semantics, block_bytes):
    limit = min(int(block_bytes) + VMEM_INTERNAL, VMEM_PHYSICAL - 4 * 1024 * 1024)
    return pltpu.CompilerParams(dimension_semantics=semantics, vmem_limit_bytes=limit)


def _nbytes(shape, dtype):
    return int(np.prod(shape)) * jnp.dtype(dtype).itemsize


def _layout(d_model):
    off = {}
    pos = 0
    for name, width, block in (("gate", N_BRANCH * d_model, N_BRANCH * d_model),
                               ("a_in", A_WIDTH, A_WIDTH), ("a_b", A_WIDTH, A_WIDTH), ("a_c", A_WIDTH, A_WIDTH),
                               ("gq", GLA_KD, GLA_KD), ("gk", GLA_KD, GLA_KD), ("gv", GLA_VD, GLA_VD),
                               ("gr", GLA_VD, GLA_VD), ("gab", LANE, LANE),
                               ("nq", NA_D, NA_DH), ("nk", NA_D, NA_DH), ("nv", NA_D, NA_DH)):
        assert pos % block == 0, (name, pos, block)
        off[name] = pos
        pos += width
    return off, pos


def _adaln_kernel(c_ref, w_ref, b_ref, o_ref):
    cv = c_ref[...]
    s = cv * jax.nn.sigmoid(cv)
    o_ref[0] = jnp.dot(s, w_ref[0], preferred_element_type=F32) + b_ref[0]


def _adaln(cvec, w_ada, b_ada):
    depth, d, n = w_ada.shape
    tn = 1024 if n % 1024 == 0 else n
    blocks = 2 * (_nbytes((d, tn), F32) + 2 * _nbytes((MOD_ROWS, tn), F32)) + _nbytes((MOD_ROWS, d), F32)
    return pl.pallas_call(
        _adaln_kernel,
        grid=(depth, n // tn),
        in_specs=[pl.BlockSpec((MOD_ROWS, d), lambda l, j: (0, 0)),
                  pl.BlockSpec((1, d, tn), lambda l, j: (l, 0, j)),
                  pl.BlockSpec((1, 1, tn), lambda l, j: (l, 0, j))],
        out_specs=pl.BlockSpec((1, MOD_ROWS, tn), lambda l, j: (l, 0, j)),
        out_shape=jax.ShapeDtypeStruct((depth, MOD_ROWS, n), F32),
        compiler_params=_params(("arbitrary", "arbitrary"), blocks),
        name="adaln",
    )(cvec, w_ada, b_ada.reshape(depth, 1, n))


def _mod_norm(x, g, shift, scale):
    y = x * lax.rsqrt(jnp.mean(x * x, axis=-1, keepdims=True) + RMS_EPS)
    return (y * g) * (1.0 + scale) + shift


def _inproj_kernel(x_ref, mod_ref, g_ref, w_ref, o_ref, n_scr):
    @pl.when(pl.program_id(1) == 0)
    def _():
        m = mod_ref[0]
        n_scr[...] = _mod_norm(x_ref[...], g_ref[...], m[0:1], m[1:2]).astype(BF16)

    o_ref[...] = jnp.dot(n_scr[...], w_ref[...], preferred_element_type=F32)


def _inproj(h, mod_l, g, w, group_of_tile, tm, tn):
    m_rows, d = h.shape
    n = w.shape[1]
    blocks = (2 * (_nbytes((tm, d), F32) + _nbytes((d, tn), BF16) + _nbytes((tm, tn), F32))
              + _nbytes((tm, d), BF16) + 2 * _nbytes((8, d), F32))
    return pl.pallas_call(
        _inproj_kernel,
        grid=(m_rows // tm, n // tn),
        in_specs=[pl.BlockSpec((tm, d), lambda i, j: (i, 0)),
                  pl.BlockSpec((1, 6, d), lambda i, j: (group_of_tile(i), 0, 0)),
                  pl.BlockSpec((1, d), lambda i, j: (0, 0)),
                  pl.BlockSpec((d, tn), lambda i, j: (0, j))],
        out_specs=pl.BlockSpec((tm, tn), lambda i, j: (i, j)),
        out_shape=jax.ShapeDtypeStruct((m_rows, n), F32),
        scratch_shapes=[pltpu.VMEM((tm, d), BF16)],
        compiler_params=_params(("parallel", "arbitrary"), blocks),
        name="inproj",
    )(h, mod_l, g.reshape(1, d), w)


def _cumsum_rows(x, length, reverse):
    row = lax.broadcasted_iota(jnp.int32, x.shape, 0)
    step = 1
    while step < length:
        if reverse:
            x = x + jnp.where(row < length - step, pltpu.roll(x, length - step, axis=0), 0.0)
        else:
            x = x + jnp.where(row >= step, pltpu.roll(x, step, axis=0), 0.0)
        step *= 2
    return x


def _rope_pairs(x, cos, sin):
    lane = lax.broadcasted_iota(jnp.int32, cos.shape, 1)
    first_half = (lane % GLA_DK) < (GLA_DK // 2)
    out = []
    for p in range(GLA_PAIRS):
        xs = x[:, p * LANE:(p + 1) * LANE]
        swapped = jnp.where(first_half, pltpu.roll(xs, LANE - GLA_DK // 2, axis=1), pltpu.roll(xs, GLA_DK // 2, axis=1))
        out.append(xs * cos + swapped * sin)
    return jnp.concatenate(out, axis=1)


def _gla_direction(q, k, v, ab, wa, ba, cos, sin, s_scr, reverse):
    length = q.shape[0]
    z = jnp.dot(ab.astype(BF16), wa, preferred_element_type=F32) + ba
    la = (jnp.minimum(z, 0.0) - jnp.log1p(jnp.exp(-jnp.abs(z)))) * (1.0 / GLA_TAU)
    b = _cumsum_rows(la, length, reverse)
    if reverse:
        mid, last = b[length // 2:length // 2 + 1], b[0:1]
    else:
        mid, last = b[length // 2 - 1:length // 2], b[length - 1:length]
    q = q * (GLA_DK ** -0.5)
    if cos is not None:
        q = _rope_pairs(q, cos, sin)
        k = _rope_pairs(k, cos, sin)
    q_mid = q * jnp.exp(b - mid)
    k_mid = k * jnp.exp(mid - b)
    q_state = q * jnp.exp(b)
    k_last = k * jnp.exp(last - b)
    state_decay = jnp.exp(last)
    row = lax.broadcasted_iota(jnp.int32, (length, length), 0)
    col = lax.broadcasted_iota(jnp.int32, (length, length), 1)
    visible = (row <= col) if reverse else (row >= col)
    lane = lax.broadcasted_iota(jnp.int32, (length, LANE), 1)
    nt = (((1,), (1,)), ((), ()))
    tn = (((0,), (0,)), ((), ()))
    outs = []
    for p in range(GLA_PAIRS):
        sl = slice(p * LANE, (p + 1) * LANE)
        k_mid_b = k_mid[:, sl].astype(BF16)
        for hh in range(2):
            head_lanes = (lane < GLA_DK) if hh == 0 else (lane >= GLA_DK)
            head = 2 * p + hh
            q_mid_h = jnp.where(head_lanes, q_mid[:, sl], 0.0).astype(BF16)
            att = lax.dot_general(q_mid_h, k_mid_b, nt, preferred_element_type=F32)
            att = jnp.where(visible, att, 0.0)
            v_h = v[:, head * GLA_DV:(head + 1) * GLA_DV].astype(BF16)
            o = jnp.dot(att.astype(BF16), v_h, preferred_element_type=F32)
            q_state_h = jnp.where(head_lanes, q_state[:, sl], 0.0).astype(BF16)
            s_h = s_scr[p, hh * GLA_DV:(hh + 1) * GLA_DV, :].astype(BF16)
            o = o + lax.dot_general(q_state_h, s_h, nt, preferred_element_type=F32)
            outs.append(o)
        v_pair = v[:, 2 * p * GLA_DV:(2 * p + 2) * GLA_DV].astype(BF16)
        update = lax.dot_general(v_pair, k_last[:, sl].astype(BF16), tn, preferred_element_type=F32)
        s_scr[p] = s_scr[p] * state_decay[:, sl] + update
    return jnp.concatenate(outs, axis=1)


def _gla_kernel(*refs, use_rope):
    qf, kf, vf, abf, qb, kb, vb, abb = refs[:8]
    pos = 8
    if use_rope:
        cosf, sinf, cosb, sinb = refs[pos:pos + 4]
        pos += 4
    waf, baf, wab, bab, s0f, s0b, of_ref, ob_ref, sf_ref, sb_ref, sf_scr, sb_scr = refs[pos:]
    step = pl.program_id(1)

    @pl.when(step == 0)
    def _():
        sf_scr[...] = s0f[0]
        sb_scr[...] = s0b[0]

    cf = (cosf[...], sinf[...]) if use_rope else (None, None)
    cb = (cosb[...], sinb[...]) if use_rope else (None, None)
    of_ref[0] = _gla_direction(qf[0], kf[0], vf[0], abf[0], waf[...], baf[...], cf[0], cf[1], sf_scr, False)
    ob_ref[0] = _gla_direction(qb[0], kb[0], vb[0], abb[0], wab[...], bab[...], cb[0], cb[1], sb_scr, True)

    @pl.when(step == pl.num_programs(1) - 1)
    def _():
        sf_ref[0] = sf_scr[...]
        sb_ref[0] = sb_scr[...]


def _gla(p3, off, waf, baf, wab, bab, s0f, s0b, rope):
    bsz, t, _ = p3.shape
    length = GLA_CHUNK
    n = t // length
    fwd = lambda blk: (lambda b, i: (b, i, blk))
    bwd = lambda blk: (lambda b, i: (b, n - 1 - i, blk))
    q_blk, k_blk = off["gq"] // GLA_KD, off["gk"] // GLA_KD
    v_blk, ab_blk = off["gv"] // GLA_VD, off["gab"] // LANE
    seq_specs = lambda m: [pl.BlockSpec((1, length, GLA_KD), m(q_blk)), pl.BlockSpec((1, length, GLA_KD), m(k_blk)),
                           pl.BlockSpec((1, length, GLA_VD), m(v_blk)), pl.BlockSpec((1, length, LANE), m(ab_blk))]
    in_specs = seq_specs(fwd) + seq_specs(bwd)
    args = [p3] * 8
    if rope is not None:
        in_specs += [pl.BlockSpec((length, LANE), lambda b, i: (i, 0))] * 2
        in_specs += [pl.BlockSpec((length, LANE), lambda b, i: (n - 1 - i, 0))] * 2
        args += [rope[0], rope[1], rope[0], rope[1]]
    const2 = lambda b, i: (0, 0)
    state_spec = pl.BlockSpec((1, GLA_PAIRS, 2 * GLA_DV, LANE), lambda b, i: (b, 0, 0, 0))
    in_specs += [pl.BlockSpec((LANE, GLA_KD), const2), pl.BlockSpec((1, GLA_KD), const2),
                 pl.BlockSpec((LANE, GLA_KD), const2), pl.BlockSpec((1, GLA_KD), const2),
                 state_spec, state_spec]
    args += [waf, baf, wab, bab, s0f, s0b]
    state_shape = jax.ShapeDtypeStruct((bsz, GLA_PAIRS, 2 * GLA_DV, LANE), F32)
    out_shape = jax.ShapeDtypeStruct((bsz, t, GLA_VD), F32)
    blocks = 2 * (2 * (2 * _nbytes((length, GLA_KD), F32) + 2 * _nbytes((length, GLA_VD), F32) + 5 * _nbytes((length, LANE), F32))
                  + 4 * _nbytes((GLA_PAIRS, 2 * GLA_DV, LANE), F32) + 2 * _nbytes((LANE, GLA_KD), F32))
    blocks += 2 * _nbytes((GLA_PAIRS, 2 * GLA_DV, LANE), F32)
    return pl.pallas_call(
        functools.partial(_gla_kernel, use_rope=rope is not None),
        grid=(bsz, n),
        in_specs=in_specs,
        out_specs=[pl.BlockSpec((1, length, GLA_VD), fwd(0)), pl.BlockSpec((1, length, GLA_VD), bwd(0)),
                   state_spec, state_spec],
        out_shape=[out_shape, out_shape, state_shape, state_shape],
        scratch_shapes=[pltpu.VMEM((GLA_PAIRS, 2 * GLA_DV, LANE), F32)] * 2,
        compiler_params=_params(("arbitrary", "arbitrary"), blocks),
        name="gla",
    )(*args)


def _na_kernel(q_ref, k_ref, v_ref, kc_ref, vc_ref, tab_ref, o_ref, k_scr, v_scr, *, rows):
    nt = (((1,), (1,)), ((), ()))
    step = pl.program_id(2)

    @pl.when(step == 0)
    def _():
        k_scr[...] = k_ref[0].astype(BF16)
        v_scr[...] = v_ref[0].astype(BF16)

    kc = kc_ref[0].astype(BF16)
    vc = vc_ref[0].astype(BF16)
    nq = NA_GROUP * GRID_W
    nk = NA_KEY_ROWS * GRID_W
    for g in range(NA_GROUPS_PER_STEP):
        r0 = (step * NA_GROUPS_PER_STEP + g) * NA_GROUP
        key_row0 = jnp.clip(r0 - NA_WIN_H // 2, 0, rows - NA_KEY_ROWS)
        k_off = pl.multiple_of(key_row0 * GRID_W, GRID_W)
        variant = jnp.where(r0 == 0, 0, jnp.where(r0 == rows - NA_GROUP, 2, 1))
        q = (q_ref[0, g * nq:(g + 1) * nq, :] * (NA_DH ** -0.5)).astype(BF16)
        kr = k_scr[pl.ds(k_off, nk), :]
        vr = v_scr[pl.ds(k_off, nk), :]
        s_loc = lax.dot_general(q, kr, nt, preferred_element_type=F32) + tab_ref[0, variant]
        s_ctx = lax.dot_general(q, kc, nt, preferred_element_type=F32)
        m = jnp.maximum(jnp.max(s_loc, axis=-1, keepdims=True), jnp.max(s_ctx, axis=-1, keepdims=True))
        p_loc = jnp.exp(s_loc - m)
        p_ctx = jnp.exp(s_ctx - m)
        denom = jnp.sum(p_loc, axis=-1, keepdims=True) + jnp.sum(p_ctx, axis=-1, keepdims=True)
        o = (jnp.dot(p_loc.astype(BF16), vr, preferred_element_type=F32)
             + jnp.dot(p_ctx.astype(BF16), vc, preferred_element_type=F32))
        o_ref[0, g * nq:(g + 1) * nq, :] = o / denom


def _na(p3, pc3, off, bias_tab):
    bsz, t, _ = p3.shape
    ct = pc3.shape[1]
    rows = t // GRID_W
    tq = NA_GROUPS_PER_STEP * NA_GROUP * GRID_W
    nq, nk = NA_GROUP * GRID_W, NA_KEY_ROWS * GRID_W
    q0, k0, v0 = off["nq"] // NA_DH, off["nk"] // NA_DH, off["nv"] // NA_DH
    blocks = (2 * (2 * _nbytes((tq, NA_DH), F32) + 2 * _nbytes((t, NA_DH), F32) + 2 * _nbytes((ct, NA_DH), F32)
                   + _nbytes((3, nq, nk), F32)) + 2 * _nbytes((t, NA_DH), BF16)
              + 3 * NA_GROUPS_PER_STEP * _nbytes((nq, nk + ct), F32))
    return pl.pallas_call(
        functools.partial(_na_kernel, rows=rows),
        grid=(bsz, NA_HEADS, rows // (NA_GROUPS_PER_STEP * NA_GROUP)),
        in_specs=[pl.BlockSpec((1, tq, NA_DH), lambda b, h, r: (b, r, q0 + h)),
                  pl.BlockSpec((1, t, NA_DH), lambda b, h, r: (b, 0, k0 + h)),
                  pl.BlockSpec((1, t, NA_DH), lambda b, h, r: (b, 0, v0 + h)),
                  pl.BlockSpec((1, ct, NA_DH), lambda b, h, r: (b, 0, k0 + h)),
                  pl.BlockSpec((1, ct, NA_DH), lambda b, h, r: (b, 0, v0 + h)),
                  pl.BlockSpec((1, 3, nq, nk), lambda b, h, r: (h, 0, 0, 0))],
        out_specs=pl.BlockSpec((1, tq, NA_DH), lambda b, h, r: (b, r, h)),
        out_shape=jax.ShapeDtypeStruct((bsz, t, NA_D), F32),
        scratch_shapes=[pltpu.VMEM((t, NA_DH), BF16), pltpu.VMEM((t, NA_DH), BF16)],
        compiler_params=_params(("arbitrary", "arbitrary", "arbitrary"), blocks),
        name="natten",
    )(p3, p3, p3, pc3, pc3, bias_tab)


def _na_bias_table(rpb, rows):
    w = GRID_W
    col = np.arange(w)
    col_start = np.clip(col - NA_WIN_W // 2, 0, w - NA_WIN_W)
    valid_col = (col[None, :] >= col_start[:, None]) & (col[None, :] < col_start[:, None] + NA_WIN_W)
    col_off = col[None, :] - col[:, None] + (NA_WIN_W - 1)
    col_sel = np.zeros((2 * NA_WIN_W - 1, w, w), np.float32)
    cc, jj = np.nonzero(valid_col)
    col_sel[col_off[cc, jj], cc, jj] = 1.0
    row_sel = np.zeros((3, NA_GROUP, NA_KEY_ROWS, 2 * NA_WIN_H - 1), np.float32)
    valid_row = np.zeros((3, NA_GROUP, NA_KEY_ROWS), bool)
    for variant, r0 in enumerate((0, NA_GROUP, rows - NA_GROUP)):
        key_row0 = int(np.clip(r0 - NA_WIN_H // 2, 0, rows - NA_KEY_ROWS))
        for rq in range(NA_GROUP):
            r = r0 + rq
            r_start = int(np.clip(r - NA_WIN_H // 2, 0, rows - NA_WIN_H))
            for ki in range(NA_KEY_ROWS):
                key_row = key_row0 + ki
                if r_start <= key_row < r_start + NA_WIN_H:
                    valid_row[variant, rq, ki] = True
                    row_sel[variant, rq, ki, key_row - r + (NA_WIN_H - 1)] = 1.0
    tab = jnp.einsum("vqkr,hrx,xcj->hvqckj", row_sel, rpb.astype(F32), col_sel, precision=lax.Precision.HIGHEST)
    valid = valid_row[:, :, None, :, None] & valid_col[None, None, :, None, :]
    tab = jnp.where(valid[None], tab, MASK_VALUE)
    return tab.reshape(rpb.shape[0], 3, NA_GROUP * w, NA_KEY_ROWS * w)


def _ctx_attn_kernel(q_ref, k_ref, v_ref, o_ref):
    q = (q_ref[0] * (NA_DH ** -0.5)).astype(BF16)
    s = lax.dot_general(q, k_ref[0].astype(BF16), (((1,), (1,)), ((), ())), preferred_element_type=F32)
    p = jnp.exp(s - jnp.max(s, axis=-1, keepdims=True))
    o = jnp.dot(p.astype(BF16), v_ref[0].astype(BF16), preferred_element_type=F32)
    o_ref[0] = o / jnp.sum(p, axis=-1, keepdims=True)


def _ctx_attn(pc3, off):
    bsz, ct, _ = pc3.shape
    q0, k0, v0 = off["nq"] // NA_DH, off["nk"] // NA_DH, off["nv"] // NA_DH
    spec = lambda c0: pl.BlockSpec((1, ct, NA_DH), lambda b, h: (b, 0, c0 + h))
    return pl.pallas_call(
        _ctx_attn_kernel,
        grid=(bsz, NA_HEADS),
        in_specs=[spec(q0), spec(k0), spec(v0)],
        out_specs=spec(0),
        out_shape=jax.ShapeDtypeStruct((bsz, ct, NA_D), F32),
        compiler_params=_params(("arbitrary", "arbitrary"), 8 * _nbytes((ct, NA_DH), F32)),
        name="ctx_attn",
    )(pc3, pc3, pc3)


def _branch_kernel(gate_ref, ain_ref, ab_ref, ac_ref, ainp_ref, acp_ref, ainn_ref, acn_ref,
                   of_ref, ob_ref, gr_ref, na_ref, cw_ref, gn_ref, wa_ref, wg_ref, wn_ref, y_ref, *, tiles_per_seq):
    tm = ain_ref.shape[0]
    d = y_ref.shape[1]
    i = pl.program_id(0)
    tile_in_seq = i % tiles_per_seq
    u = ac_ref[...] * ain_ref[...]
    u_before = jnp.where(tile_in_seq == 0, 0.0, acp_ref[SUBLANE - 1:SUBLANE, :] * ainp_ref[SUBLANE - 1:SUBLANE, :])
    u_after = jnp.where(tile_in_seq == tiles_per_seq - 1, 0.0, acn_ref[0:1, :] * ainn_ref[0:1, :])
    row = lax.broadcasted_iota(jnp.int32, u.shape, 0)
    u_prev = jnp.where(row == 0, u_before, pltpu.roll(u, 1, axis=0))
    u_next = jnp.where(row == tm - 1, u_after, pltpu.roll(u, tm - 1, axis=0))
    cw = cw_ref[...]
    a = ab_ref[...] * (cw[0:1] * u_prev + cw[1:2] * u + cw[2:3] * u_next)
    o = of_ref[...] + ob_ref[...]
    gn = gn_ref[...]
    normed = []
    for h in range(GLA_HEADS):
        oh = o[:, h * GLA_DV:(h + 1) * GLA_DV]
        normed.append(oh * lax.rsqrt(jnp.mean(oh * oh, axis=-1, keepdims=True) + RMS_EPS) * gn)
    r = gr_ref[...]
    g = jnp.concatenate(normed, axis=1) * (r * jax.nn.sigmoid(r))
    y = jax.nn.sigmoid(gate_ref[:, 0:d]) * jnp.dot(a.astype(BF16), wa_ref[...], preferred_element_type=F32)
    y = y + jax.nn.sigmoid(gate_ref[:, d:2 * d]) * jnp.dot(g.astype(BF16), wg_ref[...], preferred_element_type=F32)
    y = y + jax.nn.sigmoid(gate_ref[:, 2 * d:3 * d]) * jnp.dot(na_ref[...].astype(BF16), wn_ref[...],
                                                                 preferred_element_type=F32)
    y_ref[...] = y.astype(BF16)


def _branch(p, off, og_f, og_b, na, conv_w, g_norm, wa, wg, wn, seq_len, tm):
    m_rows = p.shape[0]
    d = wa.shape[1]
    tiles_per_seq = seq_len // tm
    halo = tm // SUBLANE
    n_halo = m_rows // SUBLANE
    cur = lambda blk: (lambda i: (i, blk))
    prev = lambda blk: (lambda i: (jnp.maximum(i * halo - 1, 0), blk))
    nxt = lambda blk: (lambda i: (jnp.minimum((i + 1) * halo, n_halo - 1), blk))
    a_in, a_b, a_c = off["a_in"] // A_WIDTH, off["a_b"] // A_WIDTH, off["a_c"] // A_WIDTH
    const = lambda i: (0, 0)
    in_specs = [pl.BlockSpec((tm, N_BRANCH * d), cur(off["gate"] // (N_BRANCH * d))),
                pl.BlockSpec((tm, A_WIDTH), cur(a_in)), pl.BlockSpec((tm, A_WIDTH), cur(a_b)),
                pl.BlockSpec((tm, A_WIDTH), cur(a_c)),
                pl.BlockSpec((SUBLANE, A_WIDTH), prev(a_in)), pl.BlockSpec((SUBLANE, A_WIDTH), prev(a_c)),
                pl.BlockSpec((SUBLANE, A_WIDTH), nxt(a_in)), pl.BlockSpec((SUBLANE, A_WIDTH), nxt(a_c)),
                pl.BlockSpec((tm, GLA_VD), cur(0)), pl.BlockSpec((tm, GLA_VD), cur(0)),
                pl.BlockSpec((tm, GLA_VD), cur(off["gr"] // GLA_VD)),
                pl.BlockSpec((tm, NA_D), cur(0)),
                pl.BlockSpec((3, A_WIDTH), const), pl.BlockSpec((1, GLA_DV), const),
                pl.BlockSpec((A_WIDTH, d), const), pl.BlockSpec((GLA_VD, d), const), pl.BlockSpec((NA_D, d), const)]
    blocks = 2 * (_nbytes((tm, N_BRANCH * d), F32) + 3 * _nbytes((tm, A_WIDTH), F32) + 4 * _nbytes((tm, GLA_VD), F32)
                  + _nbytes((A_WIDTH + GLA_VD + NA_D, d), BF16) + _nbytes((tm, d), BF16)) + 6 * _nbytes((tm, d), F32)
    return pl.pallas_call(
        functools.partial(_branch_kernel, tiles_per_seq=tiles_per_seq),
        grid=(m_rows // tm,),
        in_specs=in_specs,
        out_specs=pl.BlockSpec((tm, d), cur(0)),
        out_shape=jax.ShapeDtypeStruct((m_rows, d), BF16),
        compiler_params=_params(("arbitrary",), blocks),
        name="branch_merge",
    )(p, p, p, p, p, p, p, p, og_f, og_b, p, na, conv_w, g_norm.reshape(1, GLA_DV), wa, wg, wn)


def _proj_kernel(y_ref, w_ref, h_ref, gm_ref, o_ref):
    o_ref[...] = h_ref[...] + gm_ref[0] * jnp.dot(y_ref[...], w_ref[...], preferred_element_type=F32)


def _proj_residual(y, w, h, gate, group_of_tile, tm, tn):
    m_rows, d = h.shape
    blocks = 2 * (_nbytes((tm, d), BF16) + _nbytes((d, tn), BF16) + 2 * _nbytes((tm, tn), F32)) + _nbytes((tm, tn), F32)
    return pl.pallas_call(
        _proj_kernel,
        grid=(m_rows // tm, d // tn),
        in_specs=[pl.BlockSpec((tm, d), lambda i, j: (i, 0)),
                  pl.BlockSpec((d, tn), lambda i, j: (0, j)),
                  pl.BlockSpec((tm, tn), lambda i, j: (i, j)),
                  pl.BlockSpec((1, 1, tn), lambda i, j: (group_of_tile(i), 0, j))],
        out_specs=pl.BlockSpec((tm, tn), lambda i, j: (i, j)),
        out_shape=jax.ShapeDtypeStruct((m_rows, d), F32),
        compiler_params=_params(("parallel", "arbitrary"), blocks),
        name="out_proj",
    )(y, w, h, gate)


def _ffn_kernel(x_ref, mod_ref, g_ref, w1_ref, w3_ref, w2_ref, gfin_ref, o_ref, n_scr, acc_scr, *, final_norm):
    j = pl.program_id(1)

    @pl.when(j == 0)
    def _():
        m = mod_ref[0]
        n_scr[...] = _mod_norm(x_ref[...], g_ref[...], m[3:4], m[4:5]).astype(BF16)
        acc_scr[...] = jnp.zeros_like(acc_scr)

    n = n_scr[...]
    a = jnp.dot(n, w1_ref[...], preferred_element_type=F32)
    b = jnp.dot(n, w3_ref[...], preferred_element_type=F32)
    u = (a * jax.nn.sigmoid(a)) * b
    acc_scr[...] += jnp.dot(u.astype(BF16), w2_ref[...], preferred_element_type=F32)

    @pl.when(j == pl.num_programs(1) - 1)
    def _():
        out = x_ref[...] + mod_ref[0][5:6] * acc_scr[...]
        if final_norm:
            out = out * lax.rsqrt(jnp.mean(out * out, axis=-1, keepdims=True) + RMS_EPS) * gfin_ref[...]
        o_ref[...] = out


def _ffn(h, mod_l, g, w1, w3, w2, g_final, group_of_tile, tm, tf, final_norm):
    m_rows, d = h.shape
    f = w1.shape[1]
    blocks = (4 * _nbytes((tm, d), F32) + 6 * _nbytes((d, tf), BF16) + _nbytes((tm, d), BF16) + _nbytes((tm, d), F32)
              + 4 * _nbytes((tm, tf), F32))
    return pl.pallas_call(
        functools.partial(_ffn_kernel, final_norm=final_norm),
        grid=(m_rows // tm, f // tf),
        in_specs=[pl.BlockSpec((tm, d), lambda i, j: (i, 0)),
                  pl.BlockSpec((1, 6, d), lambda i, j: (group_of_tile(i), 0, 0)),
                  pl.BlockSpec((1, d), lambda i, j: (0, 0)),
                  pl.BlockSpec((d, tf), lambda i, j: (0, j)),
                  pl.BlockSpec((d, tf), lambda i, j: (0, j)),
                  pl.BlockSpec((tf, d), lambda i, j: (j, 0)),
                  pl.BlockSpec((1, d), lambda i, j: (0, 0))],
        out_specs=pl.BlockSpec((tm, d), lambda i, j: (i, 0)),
        out_shape=jax.ShapeDtypeStruct((m_rows, d), F32),
        scratch_shapes=[pltpu.VMEM((tm, d), BF16), pltpu.VMEM((tm, d), F32)],
        compiler_params=_params(("parallel", "arbitrary"), blocks),
        name="ffn",
    )(h, mod_l, g.reshape(1, d), w1, w3, w2, g_final.reshape(1, d))


def _largest_tile(total, cap, align):
    t = min(total, cap)
    while total % t or t % align:
        t -= align
    return t


def _rope_tables(seq_len):
    pos = jnp.arange(seq_len, dtype=jnp.int32)
    rowf = (pos // GRID_W).astype(F32)
    colf = (pos % GRID_W).astype(F32)
    per_axis = GLA_DK // 2
    inv = ROPE_BASE ** (-jnp.arange(0, per_axis, 2, dtype=F32) / per_axis)
    ang = jnp.concatenate([rowf[:, None] * inv, colf[:, None] * inv], axis=-1)
    cos, sin = jnp.cos(ang), jnp.sin(ang)
    reps = LANE // GLA_DK
    return jnp.tile(cos, (1, 2 * reps)), jnp.tile(jnp.concatenate([-sin, sin], axis=-1), (1, reps))


def _pack_w_in(w_in_l, off, n_pad):
    d = w_in_l.shape[0]
    sizes = (A_WIDTH, A_WIDTH, A_WIDTH, GLA_KD, GLA_KD, GLA_VD, GLA_VD, GLA_LOWRANK, GLA_LOWRANK, NA_D, NA_D, NA_D,
             N_BRANCH * d)
    names = ("a_in", "a_b", "a_c", "gq", "gk", "gv", "gr", "gaf", "gab_", "nq", "nk", "nv", "gate")
    starts = np.concatenate([[0], np.cumsum(sizes)])
    out = jnp.zeros((d, n_pad), BF16)
    for name, s, width in zip(names, starts[:-1], sizes):
        dst = {"gaf": off["gab"], "gab_": off["gab"] + GLA_LOWRANK}.get(name, off.get(name))
        out = lax.dynamic_update_slice(out, w_in_l[:, int(s):int(s) + width].astype(BF16), (0, int(dst)))
    return out


def _pad_lowrank(wa, col0):
    return jnp.zeros((LANE, GLA_KD), BF16).at[col0:col0 + GLA_LOWRANK].set(wa.astype(BF16))


def kernel(x, c, ctx, c_ctx, w_ada, b_ada, g_mix, g_ffn, w_in, conv_w, gla_wa_f, gla_ba_f, gla_wa_b, gla_ba_b,
           gla_g_norm, na_rpb, w_a_out, w_g_out, w_n_out, w_o, w_ffn1, w_ffn3, w_ffn2, g_final):
    bsz, seq, d = x.shape
    ct = ctx.shape[1]
    depth = w_ada.shape[0]
    f = w_ffn1.shape[2]
    assert bsz + 1 <= MOD_ROWS and seq % GRID_W == 0 and seq % GLA_CHUNK == 0 and ct % GLA_CHUNK == 0
    assert (seq // GRID_W) % (NA_GROUP * NA_GROUPS_PER_STEP) == 0 and seq // GRID_W > NA_KEY_ROWS
    off, n_used = _layout(d)
    tn_in = 10 * LANE
    n_pad = -(-n_used // tn_in) * tn_in

    tm_big = _largest_tile(seq, 1024, SUBLANE)
    tm_ffn = _largest_tile(seq, 512, SUBLANE)
    tm_branch = _largest_tile(seq, 256, SUBLANE)
    tn_proj = _largest_tile(d, 512, LANE)
    tf = _largest_tile(f, 512, LANE)
    lat_group = lambda tm: (lambda i: i // (seq // tm))
    ctx_group = lambda i: bsz

    cvec = jnp.zeros((MOD_ROWS, d), F32).at[:bsz].set(c).at[bsz].set(c_ctx)
    mod = _adaln(cvec, w_ada, b_ada).reshape(depth, MOD_ROWS, 6, d)
    rope = _rope_tables(seq)
    zero_state = jnp.zeros((bsz, GLA_PAIRS, 2 * GLA_DV, LANE), F32)

    h = x.reshape(bsz * seq, d)
    hc = ctx.reshape(bsz * ct, d)
    for l in range(depth):
        need_ctx = l < depth - 1
        last = l == depth - 1
        mod_l = mod[l]
        gm = mod_l[:, 2, :].reshape(MOD_ROWS, 1, d)
        w_in_l = _pack_w_in(w_in[l], off, n_pad)
        waf, wab = _pad_lowrank(gla_wa_f[l], 0), _pad_lowrank(gla_wa_b[l], GLA_LOWRANK)
        baf, bab = gla_ba_f[l].reshape(1, GLA_KD), gla_ba_b[l].reshape(1, GLA_KD)
        wa_o, wg_o, wn_o = w_a_out[l].astype(BF16), w_g_out[l].astype(BF16), w_n_out[l].astype(BF16)
        wo, w1, w3, w2 = w_o[l].astype(BF16), w_ffn1[l].astype(BF16), w_ffn3[l].astype(BF16), w_ffn2[l].astype(BF16)
        bias_tab = _na_bias_table(na_rpb[l], seq // GRID_W)

        p = _inproj(h, mod_l, g_mix[l], w_in_l, lat_group(tm_big), tm_big, tn_in)
        pc = _inproj(hc, mod_l, g_mix[l], w_in_l, ctx_group, ct, tn_in)
        p3 = p.reshape(bsz, seq, n_pad)
        pc3 = pc.reshape(bsz, ct, n_pad)

        ogc_f, ogc_b, s_f, s_b = _gla(pc3, off, waf, baf, wab, bab, zero_state, zero_state, None)
        og_f, og_b, _, _ = _gla(p3, off, waf, baf, wab, bab, s_f, s_b, rope)
        na = _na(p3, pc3, off, bias_tab)

        y = _branch(p, off, og_f.reshape(bsz * seq, GLA_VD), og_b.reshape(bsz * seq, GLA_VD),
                    na.reshape(bsz * seq, NA_D), conv_w[l], gla_g_norm[l], wa_o, wg_o, wn_o, seq, tm_branch)
        h = _proj_residual(y, wo, h, gm, lat_group(tm_big), tm_big, tn_proj)
        h = _ffn(h, mod_l, g_ffn[l], w1, w3, w2, g_final, lat_group(tm_ffn), tm_ffn, tf, last)
        if need_ctx:
            na_c = _ctx_attn(pc3, off)
            yc = _branch(pc, off, ogc_f.reshape(bsz * ct, GLA_VD), ogc_b.reshape(bsz * ct, GLA_VD),
                         na_c.reshape(bsz * ct, NA_D), conv_w[l], gla_g_norm[l], wa_o, wg_o, wn_o, ct, ct)
            hc = _proj_residual(yc, wo, hc, gm, ctx_group, ct, tn_proj)
            hc = _ffn(hc, mod_l, g_ffn[l], w1, w3, w2, g_final, ctx_group, ct, tf, False)
    return h.reshape(bsz, seq, d)
```

```python
import functools

import numpy as np
import jax
import jax.numpy as jnp
from jax import lax
from jax.experimental import pallas as pl
from jax.experimental.pallas import tpu as pltpu

GRID_W = 64
RMS_EPS = 1e-6
ROPE_BASE = 10000.0
N_BRANCH = 3
A_WIDTH = 512
GLA_HEADS = 6
GLA_DK = 64
GLA_DV = 128
GLA_KD = GLA_HEADS * GLA_DK
GLA_VD = GLA_HEADS * GLA_DV
GLA_LOWRANK = 16
GLA_TAU = 16.0
GLA_CHUNK = 64
GLA_CHUNKS_PER_STEP = 2
GLA_SAFE_EXPONENT = 80.0
GLA_PAIRS = GLA_HEADS // 2
NA_HEADS = 6
NA_DH = 128
NA_D = NA_HEADS * NA_DH
NA_WIN_H = 8
NA_WIN_W = 16
NA_GROUP = 4
NA_KEY_ROWS = NA_GROUP + NA_WIN_H - 1
NA_GROUPS_PER_STEP = 4
MASK_VALUE = -1e30

LANE = 128
SUBLANE = 8
VMEM_PHYSICAL = 64 * 1024 * 1024
VMEM_INTERNAL = 12 * 1024 * 1024
MOD_ROWS = 8

BF16 = jnp.bfloat16
F32 = jnp.float32


def _params(semantics, block_bytes):
    limit = min(int(block_bytes) + VMEM_INTERNAL, VMEM_PHYSICAL - 4 * 1024 * 1024)
    return pltpu.CompilerParams(dimension_semantics=semantics, vmem_limit_bytes=limit)


def _nbytes(shape, dtype):
    return int(np.prod(shape)) * jnp.dtype(dtype).itemsize


def _layout(d_model):
    off = {}
    pos = 0
    for name, width, block in (("gate", N_BRANCH * d_model, N_BRANCH * d_model),
                               ("a_in", A_WIDTH, A_WIDTH), ("a_b", A_WIDTH, A_WIDTH), ("a_c", A_WIDTH, A_WIDTH),
                               ("gq", GLA_KD, GLA_KD), ("gk", GLA_KD, GLA_KD), ("gv", GLA_VD, GLA_VD),
                               ("gr", GLA_VD, GLA_VD), ("gab", LANE, LANE),
                               ("nq", NA_D, NA_DH), ("nk", NA_D, NA_DH), ("nv", NA_D, NA_DH)):
        assert pos % block == 0, (name, pos, block)
        off[name] = pos
        pos += width
    return off, pos


def _adaln_kernel(c_ref, w_ref, b_ref, o_ref):
    cv = c_ref[...]
    s = cv * jax.nn.sigmoid(cv)
    o_ref[0] = jnp.dot(s, w_ref[0], preferred_element_type=F32) + b_ref[0]


def _adaln(cvec, w_ada, b_ada):
    depth, d, n = w_ada.shape
    tn = 1024 if n % 1024 == 0 else n
    blocks = 2 * (_nbytes((d, tn), F32) + 2 * _nbytes((MOD_ROWS, tn), F32)) + _nbytes((MOD_ROWS, d), F32)
    return pl.pallas_call(
        _adaln_kernel,
        grid=(depth, n // tn),
        in_specs=[pl.BlockSpec((MOD_ROWS, d), lambda l, j: (0, 0)),
                  pl.BlockSpec((1, d, tn), lambda l, j: (l, 0, j)),
                  pl.BlockSpec((1, 1, tn), lambda l, j: (l, 0, j))],
        out_specs=pl.BlockSpec((1, MOD_ROWS, tn), lambda l, j: (l, 0, j)),
        out_shape=jax.ShapeDtypeStruct((depth, MOD_ROWS, n), F32),
        compiler_params=_params(("arbitrary", "arbitrary"), blocks),
        name="adaln",
    )(cvec, w_ada, b_ada.reshape(depth, 1, n))


def _mod_norm(x, g, shift, scale):
    y = x * lax.rsqrt(jnp.mean(x * x, axis=-1, keepdims=True) + RMS_EPS)
    return (y * g) * (1.0 + scale) + shift


def _inproj_kernel(x_ref, mod_ref, g_ref, w_ref, o_ref, n_scr):
    @pl.when(pl.program_id(1) == 0)
    def _():
        m = mod_ref[0]
        n_scr[...] = _mod_norm(x_ref[...], g_ref[...], m[0:1], m[1:2]).astype(BF16)

    o_ref[...] = jnp.dot(n_scr[...], w_ref[...], preferred_element_type=F32)


def _inproj(h, mod_l, g, w, layer, group_of_tile, tm, tn):
    m_rows, d = h.shape
    n = w.shape[2]
    blocks = (2 * (_nbytes((tm, d), F32) + _nbytes((d, tn), BF16) + _nbytes((tm, tn), F32))
              + _nbytes((tm, d), BF16) + 2 * _nbytes((8, d), F32))
    return pl.pallas_call(
        _inproj_kernel,
        grid=(m_rows // tm, n // tn),
        in_specs=[pl.BlockSpec((tm, d), lambda i, j: (i, 0)),
                  pl.BlockSpec((1, 6, d), lambda i, j: (group_of_tile(i), 0, 0)),
                  pl.BlockSpec((1, d), lambda i, j: (0, 0)),
                  pl.BlockSpec((None, d, tn), lambda i, j: (layer, 0, j))],
        out_specs=pl.BlockSpec((tm, tn), lambda i, j: (i, j)),
        out_shape=jax.ShapeDtypeStruct((m_rows, n), F32),
        scratch_shapes=[pltpu.VMEM((tm, d), BF16)],
        compiler_params=_params(("parallel", "arbitrary"), blocks),
        name="inproj",
    )(h, mod_l, g.reshape(1, d), w)


def _rope_pairs(x, cos, sin):
    lane = lax.broadcasted_iota(jnp.int32, cos.shape, 1)
    first_half = (lane % GLA_DK) < (GLA_DK // 2)
    out = []
    for p in range(GLA_PAIRS):
        xs = x[:, p * LANE:(p + 1) * LANE]
        swapped = jnp.where(first_half, pltpu.roll(xs, LANE - GLA_DK // 2, axis=1), pltpu.roll(xs, GLA_DK // 2, axis=1))
        out.append(xs * cos + swapped * sin)
    return jnp.concatenate(out, axis=1)


def _per_chunk_rows(x, offset):
    rows = x.shape[0]
    chunk = lax.broadcasted_iota(jnp.int32, (rows, 1), 0) // GLA_CHUNK
    out = x[offset:offset + 1]
    for c in range(1, rows // GLA_CHUNK):
        out = jnp.where(chunk >= c, x[c * GLA_CHUNK + offset:c * GLA_CHUNK + offset + 1], out)
    return out


def _gla_prepare(q, k, ab, wa, ba, cos, sin, reverse):
    rows = q.shape[0]
    z = jnp.dot(ab.astype(BF16), wa, preferred_element_type=F32) + ba
    la = (jnp.minimum(z, 0.0) - jnp.log(1.0 + jnp.exp(-jnp.abs(z)))) * (1.0 / GLA_TAU)
    r = lax.broadcasted_iota(jnp.int32, (rows, rows), 0)
    c = lax.broadcasted_iota(jnp.int32, (rows, rows), 1)
    tri = ((r // GLA_CHUNK == c // GLA_CHUNK) & ((c >= r) if reverse else (c <= r))).astype(BF16)
    la_hi = la.astype(BF16)
    la_lo = (la - la_hi.astype(F32)).astype(BF16)
    b = jnp.dot(tri, la_hi, preferred_element_type=F32) + jnp.dot(tri, la_lo, preferred_element_type=F32)
    q = q * (GLA_DK ** -0.5)
    if cos is not None:
        q = _rope_pairs(q, cos, sin)
        k = _rope_pairs(k, cos, sin)
    return q, k, b


def _gla_block(q, k, b, v, s_scr, o_ref, inter_scr, reverse):
    rows = q.shape[0]
    n_chunks = rows // GLA_CHUNK
    half = GLA_CHUNK // 2
    rel = b - _per_chunk_rows(b, half if reverse else half - 1)
    last = _per_chunk_rows(b, 0 if reverse else GLA_CHUNK - 1)
    q_mid = (q * jnp.exp(rel))
    k_mid = (k * jnp.exp(-rel)).astype(BF16)
    q_state = (q * jnp.exp(b)).astype(BF16)
    k_last = (k * jnp.exp(last - b)).astype(BF16)
    state_decay = jnp.exp(last)
    v = v.astype(BF16)
    row = lax.broadcasted_iota(jnp.int32, (2 * GLA_CHUNK, GLA_CHUNK), 0) % GLA_CHUNK
    col = lax.broadcasted_iota(jnp.int32, (2 * GLA_CHUNK, GLA_CHUNK), 1)
    visible = (row <= col) if reverse else (row >= col)
    first_head = lax.broadcasted_iota(jnp.int32, (rows, LANE), 1) < GLA_DK
    blk_r = lax.broadcasted_iota(jnp.int32, (LANE, 2 * GLA_DV), 0) < GLA_DK
    blk_c = lax.broadcasted_iota(jnp.int32, (LANE, 2 * GLA_DV), 1) < GLA_DV
    same_head = blk_r == blk_c
    nt = (((1,), (1,)), ((), ()))
    tn = (((0,), (0,)), ((), ()))
    q_first = [jnp.where(first_head, q_mid[:, p * LANE:(p + 1) * LANE], 0.0).astype(BF16) for p in range(GLA_PAIRS)]
    q_second = [jnp.where(first_head, 0.0, q_mid[:, p * LANE:(p + 1) * LANE]).astype(BF16) for p in range(GLA_PAIRS)]
    order = range(n_chunks - 1, -1, -1) if reverse else range(n_chunks)
    for c in order:
        rs = slice(c * GLA_CHUNK, (c + 1) * GLA_CHUNK)
        intra, inter = [], []
        for p in range(GLA_PAIRS):
            sl = slice(p * LANE, (p + 1) * LANE)
            q_stack = jnp.concatenate([q_first[p][rs], q_second[p][rs]], axis=0)
            att = lax.dot_general(q_stack, k_mid[rs, sl], nt, preferred_element_type=F32)
            att = jnp.where(visible, att, 0.0).astype(BF16)
            v_pair = v[rs, 2 * p * GLA_DV:(2 * p + 2) * GLA_DV]
            intra.append(jnp.dot(att[:GLA_CHUNK], v_pair[:, :GLA_DV], preferred_element_type=F32))
            intra.append(jnp.dot(att[GLA_CHUNK:], v_pair[:, GLA_DV:], preferred_element_type=F32))
            s_p = s_scr[p]
            inter.append(jnp.dot(q_state[rs, sl], s_p.astype(BF16), preferred_element_type=F32))
            update = lax.dot_general(k_last[rs, sl], v_pair, tn, preferred_element_type=F32)
            decay_row = state_decay[c * GLA_CHUNK:c * GLA_CHUNK + 1, sl]
            decay_col = jnp.transpose(jnp.broadcast_to(decay_row, (SUBLANE, LANE)))[:, 0:1]
            s_scr[p] = s_p * decay_col + jnp.where(same_head, update, 0.0)
        inter = jnp.concatenate(inter, axis=1)
        o_ref[0, rs, :] = jnp.concatenate(intra, axis=1) + inter
        inter_scr[rs, :] = inter
    return jnp.max(jnp.abs(rel))


def _gla_chunk_exact(q_scr, k_scr, b_scr, v_ref, o_ref, row0, reverse):
    length = GLA_CHUNK
    rows = pl.ds(row0, length)
    q = q_scr[rows, :]
    b = b_scr[rows, :]
    row = lax.broadcasted_iota(jnp.int32, (length, 1), 0)
    first_head = lax.broadcasted_iota(jnp.int32, (length, LANE), 1) < GLA_DK

    def body(s, carry):
        src = pl.ds(row0 + s, 1)
        w = q * k_scr[src, :] * jnp.exp(jnp.minimum(b - b_scr[src, :], 0.0))
        w = jnp.where((row <= s) if reverse else (row >= s), w, 0.0)
        v_row = v_ref[0, src, :]
        for p in range(GLA_PAIRS):
            wp = w[:, p * LANE:(p + 1) * LANE]
            for hh in range(2):
                a = jnp.sum(jnp.where(first_head == (hh == 0), wp, 0.0), axis=1, keepdims=True)
                lanes = slice((2 * p + hh) * GLA_DV, (2 * p + hh + 1) * GLA_DV)
                o_ref[0, rows, lanes] += a * v_row[:, lanes]
        return carry

    lax.fori_loop(0, length, body, 0)


def _gla_kernel(*refs, use_rope):
    qf, kf, vf, abf, qb, kb, vb, abb = refs[:8]
    pos = 8
    if use_rope:
        cosf, sinf, cosb, sinb = refs[pos:pos + 4]
        pos += 4
    (waf, baf, wab, bab, s0f, s0b, of_ref, ob_ref, sf_ref, sb_ref,
     sf_scr, sb_scr, inter_f, inter_b, q_scr, k_scr, b_scr) = refs[pos:]
    step = pl.program_id(1)

    @pl.when(step == 0)
    def _():
        sf_scr[...] = s0f[0]
        sb_scr[...] = s0b[0]

    directions = ((False, (qf, kf, vf, abf), (cosf, sinf) if use_rope else None, waf, baf, sf_scr, of_ref, inter_f),
                  (True, (qb, kb, vb, abb), (cosb, sinb) if use_rope else None, wab, bab, sb_scr, ob_ref, inter_b))

    def prepared(reverse, refs4, rope_refs, w_ref, bias_ref):
        q_ref, k_ref, _, ab_ref = refs4
        cos = rope_refs[0][...] if use_rope else None
        sin = rope_refs[1][...] if use_rope else None
        return _gla_prepare(q_ref[0], k_ref[0], ab_ref[0], w_ref[...], bias_ref[...], cos, sin, reverse)

    margin = jnp.float32(0.0)
    for reverse, refs4, rope_refs, w_ref, bias_ref, s_scr, o_ref, inter_scr in directions:
        q, k, b = prepared(reverse, refs4, rope_refs, w_ref, bias_ref)
        m = _gla_block(q, k, b, refs4[2][0], s_scr, o_ref, inter_scr, reverse)
        margin = jnp.maximum(margin, m)

    @pl.when(margin > GLA_SAFE_EXPONENT)
    def _():
        for reverse, refs4, rope_refs, w_ref, bias_ref, s_scr, o_ref, inter_scr in directions:
            q, k, b = prepared(reverse, refs4, rope_refs, w_ref, bias_ref)
            q_scr[...] = q
            k_scr[...] = k
            b_scr[...] = b
            o_ref[0] = inter_scr[...]
            for c in range(GLA_CHUNKS_PER_STEP):
                _gla_chunk_exact(q_scr, k_scr, b_scr, refs4[2], o_ref, c * GLA_CHUNK, reverse)

    @pl.when(step == pl.num_programs(1) - 1)
    def _():
        sf_ref[0] = sf_scr[...]
        sb_ref[0] = sb_scr[...]


def _gla(p3, off, waf, baf, wab, bab, s0f, s0b, rope):
    bsz, t, _ = p3.shape
    length = GLA_CHUNK * GLA_CHUNKS_PER_STEP
    n = t // length
    fwd = lambda blk: (lambda b, i: (b, i, blk))
    bwd = lambda blk: (lambda b, i: (b, n - 1 - i, blk))
    q_blk, k_blk = off["gq"] // GLA_KD, off["gk"] // GLA_KD
    v_blk, ab_blk = off["gv"] // GLA_VD, off["gab"] // LANE
    seq_specs = lambda m: [pl.BlockSpec((1, length, GLA_KD), m(q_blk)), pl.BlockSpec((1, length, GLA_KD), m(k_blk)),
                           pl.BlockSpec((1, length, GLA_VD), m(v_blk)), pl.BlockSpec((1, length, LANE), m(ab_blk))]
    in_specs = seq_specs(fwd) + seq_specs(bwd)
    args = [p3] * 8
    if rope is not None:
        in_specs += [pl.BlockSpec((length, LANE), lambda b, i: (i, 0))] * 2
        in_specs += [pl.BlockSpec((length, LANE), lambda b, i: (n - 1 - i, 0))] * 2
        args += [rope[0], rope[1], rope[0], rope[1]]
    const2 = lambda b, i: (0, 0)
    state_dims = (GLA_PAIRS, LANE, 2 * GLA_DV)
    state_spec = pl.BlockSpec((1,) + state_dims, lambda b, i: (b, 0, 0, 0))
    in_specs += [pl.BlockSpec((LANE, GLA_KD), const2), pl.BlockSpec((1, GLA_KD), const2),
                 pl.BlockSpec((LANE, GLA_KD), const2), pl.BlockSpec((1, GLA_KD), const2),
                 state_spec, state_spec]
    args += [waf, baf, wab, bab, s0f, s0b]
    state_shape = jax.ShapeDtypeStruct((bsz,) + state_dims, F32)
    out_shape = jax.ShapeDtypeStruct((bsz, t, GLA_VD), F32)
    blocks = 2 * (2 * (2 * _nbytes((length, GLA_KD), F32) + 2 * _nbytes((length, GLA_VD), F32) + 5 * _nbytes((length, LANE), F32))
                  + 4 * _nbytes(state_dims, F32) + 2 * _nbytes((LANE, GLA_KD), F32))
    blocks += 2 * _nbytes(state_dims, F32) + 2 * _nbytes((length, GLA_VD), F32)
    return pl.pallas_call(
        functools.partial(_gla_kernel, use_rope=rope is not None),
        grid=(bsz, n),
        in_specs=in_specs,
        out_specs=[pl.BlockSpec((1, length, GLA_VD), fwd(0)), pl.BlockSpec((1, length, GLA_VD), bwd(0)),
                   state_spec, state_spec],
        out_shape=[out_shape, out_shape, state_shape, state_shape],
        scratch_shapes=[pltpu.VMEM(state_dims, F32), pltpu.VMEM(state_dims, F32),
                        pltpu.VMEM((length, GLA_VD), F32), pltpu.VMEM((length, GLA_VD), F32),
                        pltpu.VMEM((length, GLA_KD), F32), pltpu.VMEM((length, GLA_KD), F32),
                        pltpu.VMEM((length, GLA_KD), F32)],
        compiler_params=_params(("arbitrary", "arbitrary"), blocks),
        name="gla",
    )(*args)


def _na_kernel(q_ref, k_ref, v_ref, kc_ref, vc_ref, blk_ref, o_ref, k_scr, v_scr, tab_scr, *, rows):
    nt = (((1,), (1,)), ((), ()))
    step = pl.program_id(2)

    @pl.when(step == 0)
    def _():
        k_scr[...] = k_ref[0].astype(BF16)
        v_scr[...] = v_ref[0].astype(BF16)
        masked = jnp.full((GRID_W, GRID_W), MASK_VALUE, F32)
        for variant, per_query in enumerate(_na_row_offsets(rows)):
            for rq, offsets in enumerate(per_query):
                tab_scr[variant, rq * GRID_W:(rq + 1) * GRID_W, :] = jnp.concatenate(
                    [masked if r is None else blk_ref[0, r] for r in offsets], axis=1)

    kc = kc_ref[0].astype(BF16)
    vc = vc_ref[0].astype(BF16)
    nq = NA_GROUP * GRID_W
    nk = NA_KEY_ROWS * GRID_W
    for g in range(NA_GROUPS_PER_STEP):
        r0 = (step * NA_GROUPS_PER_STEP + g) * NA_GROUP
        key_row0 = jnp.clip(r0 - NA_WIN_H // 2, 0, rows - NA_KEY_ROWS)
        k_off = pl.multiple_of(key_row0 * GRID_W, GRID_W)
        variant = jnp.where(r0 == 0, 0, jnp.where(r0 == rows - NA_GROUP, 2, 1))
        q = (q_ref[0, g * nq:(g + 1) * nq, :] * (NA_DH ** -0.5)).astype(BF16)
        kr = k_scr[pl.ds(k_off, nk), :]
        vr = v_scr[pl.ds(k_off, nk), :]
        s_loc = lax.dot_general(q, kr, nt, preferred_element_type=F32) + tab_scr[variant]
        s_ctx = lax.dot_general(q, kc, nt, preferred_element_type=F32)
        m = jnp.maximum(jnp.max(s_loc, axis=-1, keepdims=True), jnp.max(s_ctx, axis=-1, keepdims=True))
        p_loc = jnp.exp(s_loc - m)
        p_ctx = jnp.exp(s_ctx - m)
        denom = jnp.sum(p_loc, axis=-1, keepdims=True) + jnp.sum(p_ctx, axis=-1, keepdims=True)
        o = (jnp.dot(p_loc.astype(BF16), vr, preferred_element_type=F32)
             + jnp.dot(p_ctx.astype(BF16), vc, preferred_element_type=F32))
        o_ref[0, g * nq:(g + 1) * nq, :] = o / denom


def _na(p3, pc3, off, bias_blocks, layer):
    bsz, t, _ = p3.shape
    ct = pc3.shape[1]
    rows = t // GRID_W
    tq = NA_GROUPS_PER_STEP * NA_GROUP * GRID_W
    nq, nk = NA_GROUP * GRID_W, NA_KEY_ROWS * GRID_W
    n_off = 2 * NA_WIN_H - 1
    q0, k0, v0 = off["nq"] // NA_DH, off["nk"] // NA_DH, off["nv"] // NA_DH
    blocks = (2 * (2 * _nbytes((tq, NA_DH), F32) + 2 * _nbytes((t, NA_DH), F32) + 2 * _nbytes((ct, NA_DH), F32)
                   + _nbytes((n_off, GRID_W, LANE), F32)) + 2 * _nbytes((t, NA_DH), BF16) + _nbytes((3, nq, nk), F32)
              + 3 * NA_GROUPS_PER_STEP * _nbytes((nq, nk + ct), F32))
    return pl.pallas_call(
        functools.partial(_na_kernel, rows=rows),
        grid=(bsz, NA_HEADS, rows // (NA_GROUPS_PER_STEP * NA_GROUP)),
        in_specs=[pl.BlockSpec((1, tq, NA_DH), lambda b, h, r: (b, r, q0 + h)),
                  pl.BlockSpec((1, t, NA_DH), lambda b, h, r: (b, 0, k0 + h)),
                  pl.BlockSpec((1, t, NA_DH), lambda b, h, r: (b, 0, v0 + h)),
                  pl.BlockSpec((1, ct, NA_DH), lambda b, h, r: (b, 0, k0 + h)),
                  pl.BlockSpec((1, ct, NA_DH), lambda b, h, r: (b, 0, v0 + h)),
                  pl.BlockSpec((None, 1, n_off, GRID_W, GRID_W), lambda b, h, r: (layer, h, 0, 0, 0))],
        out_specs=pl.BlockSpec((1, tq, NA_DH), lambda b, h, r: (b, r, h)),
        out_shape=jax.ShapeDtypeStruct((bsz, t, NA_D), F32),
        scratch_shapes=[pltpu.VMEM((t, NA_DH), BF16), pltpu.VMEM((t, NA_DH), BF16), pltpu.VMEM((3, nq, nk), F32)],
        compiler_params=_params(("arbitrary", "arbitrary", "arbitrary"), blocks),
        name="natten",
    )(p3, p3, p3, pc3, pc3, bias_blocks)


def _na_row_offsets(rows):
    table = []
    for r0 in (0, NA_GROUP, rows - NA_GROUP):
        key_row0 = int(np.clip(r0 - NA_WIN_H // 2, 0, rows - NA_KEY_ROWS))
        per_query = []
        for rq in range(NA_GROUP):
            r = r0 + rq
            r_start = int(np.clip(r - NA_WIN_H // 2, 0, rows - NA_WIN_H))
            per_query.append([key_row0 + ki - r + (NA_WIN_H - 1) if r_start <= key_row0 + ki < r_start + NA_WIN_H
                              else None for ki in range(NA_KEY_ROWS)])
        table.append(per_query)
    return table


def _na_bias_blocks(rpb):
    w = GRID_W
    col = np.arange(w)
    col_start = np.clip(col - NA_WIN_W // 2, 0, w - NA_WIN_W)
    valid_col = (col[None, :] >= col_start[:, None]) & (col[None, :] < col_start[:, None] + NA_WIN_W)
    col_off = col[None, :] - col[:, None] + (NA_WIN_W - 1)
    col_sel = np.zeros((2 * NA_WIN_W - 1, w, w), np.float32)
    cc, jj = np.nonzero(valid_col)
    col_sel[col_off[cc, jj], cc, jj] = 1.0
    blocks = jnp.einsum("lhrx,xcj->lhrcj", rpb.astype(F32), col_sel, precision=lax.Precision.HIGHEST)
    return jnp.where(valid_col, blocks, MASK_VALUE)


def _ctx_attn_kernel(q_ref, k_ref, v_ref, o_ref):
    q = (q_ref[0] * (NA_DH ** -0.5)).astype(BF16)
    s = lax.dot_general(q, k_ref[0].astype(BF16), (((1,), (1,)), ((), ())), preferred_element_type=F32)
    p = jnp.exp(s - jnp.max(s, axis=-1, keepdims=True))
    o = jnp.dot(p.astype(BF16), v_ref[0].astype(BF16), preferred_element_type=F32)
    o_ref[0] = o / jnp.sum(p, axis=-1, keepdims=True)


def _ctx_attn(pc3, off):
    bsz, ct, _ = pc3.shape
    q0, k0, v0 = off["nq"] // NA_DH, off["nk"] // NA_DH, off["nv"] // NA_DH
    spec = lambda c0: pl.BlockSpec((1, ct, NA_DH), lambda b, h: (b, 0, c0 + h))
    return pl.pallas_call(
        _ctx_attn_kernel,
        grid=(bsz, NA_HEADS),
        in_specs=[spec(q0), spec(k0), spec(v0)],
        out_specs=spec(0),
        out_shape=jax.ShapeDtypeStruct((bsz, ct, NA_D), F32),
        compiler_params=_params(("arbitrary", "arbitrary"), 8 * _nbytes((ct, NA_DH), F32)),
        name="ctx_attn",
    )(pc3, pc3, pc3)


def _branch_kernel(gate_ref, ain_ref, ab_ref, ac_ref, ainp_ref, acp_ref, ainn_ref, acn_ref,
                   of_ref, ob_ref, gr_ref, na_ref, cw_ref, gn_ref, wa_ref, wg_ref, wn_ref, y_ref, *, tiles_per_seq):
    tm = ain_ref.shape[0]
    d = y_ref.shape[1]
    i = pl.program_id(0)
    tile_in_seq = i % tiles_per_seq
    u = ac_ref[...] * ain_ref[...]
    u_before = jnp.where(tile_in_seq == 0, 0.0, acp_ref[SUBLANE - 1:SUBLANE, :] * ainp_ref[SUBLANE - 1:SUBLANE, :])
    u_after = jnp.where(tile_in_seq == tiles_per_seq - 1, 0.0, acn_ref[0:1, :] * ainn_ref[0:1, :])
    row = lax.broadcasted_iota(jnp.int32, u.shape, 0)
    u_prev = jnp.where(row == 0, u_before, pltpu.roll(u, 1, axis=0))
    u_next = jnp.where(row == tm - 1, u_after, pltpu.roll(u, tm - 1, axis=0))
    cw = cw_ref[...]
    a = ab_ref[...] * (cw[0:1] * u_prev + cw[1:2] * u + cw[2:3] * u_next)
    o = of_ref[...] + ob_ref[...]
    gn = gn_ref[...]
    normed = []
    for h in range(GLA_HEADS):
        oh = o[:, h * GLA_DV:(h + 1) * GLA_DV]
        normed.append(oh * lax.rsqrt(jnp.mean(oh * oh, axis=-1, keepdims=True) + RMS_EPS) * gn)
    r = gr_ref[...]
    g = jnp.concatenate(normed, axis=1) * (r * jax.nn.sigmoid(r))
    y = jax.nn.sigmoid(gate_ref[:, 0:d]) * jnp.dot(a.astype(BF16), wa_ref[...], preferred_element_type=F32)
    y = y + jax.nn.sigmoid(gate_ref[:, d:2 * d]) * jnp.dot(g.astype(BF16), wg_ref[...], preferred_element_type=F32)
    y = y + jax.nn.sigmoid(gate_ref[:, 2 * d:3 * d]) * jnp.dot(na_ref[...].astype(BF16), wn_ref[...],
                                                                 preferred_element_type=F32)
    y_ref[...] = y.astype(BF16)


def _branch(p, off, og_f, og_b, na, conv_w, g_norm, wa, wg, wn, layer, seq_len, tm):
    m_rows = p.shape[0]
    d = wa.shape[2]
    tiles_per_seq = seq_len // tm
    halo = tm // SUBLANE
    n_halo = m_rows // SUBLANE
    cur = lambda blk: (lambda i: (i, blk))
    prev = lambda blk: (lambda i: (jnp.maximum(i * halo - 1, 0), blk))
    nxt = lambda blk: (lambda i: (jnp.minimum((i + 1) * halo, n_halo - 1), blk))
    a_in, a_b, a_c = off["a_in"] // A_WIDTH, off["a_b"] // A_WIDTH, off["a_c"] // A_WIDTH
    const = lambda i: (0, 0)
    lay = lambda i: (layer, 0, 0)
    in_specs = [pl.BlockSpec((tm, N_BRANCH * d), cur(off["gate"] // (N_BRANCH * d))),
                pl.BlockSpec((tm, A_WIDTH), cur(a_in)), pl.BlockSpec((tm, A_WIDTH), cur(a_b)),
                pl.BlockSpec((tm, A_WIDTH), cur(a_c)),
                pl.BlockSpec((SUBLANE, A_WIDTH), prev(a_in)), pl.BlockSpec((SUBLANE, A_WIDTH), prev(a_c)),
                pl.BlockSpec((SUBLANE, A_WIDTH), nxt(a_in)), pl.BlockSpec((SUBLANE, A_WIDTH), nxt(a_c)),
                pl.BlockSpec((tm, GLA_VD), cur(0)), pl.BlockSpec((tm, GLA_VD), cur(0)),
                pl.BlockSpec((tm, GLA_VD), cur(off["gr"] // GLA_VD)),
                pl.BlockSpec((tm, NA_D), cur(0)),
                pl.BlockSpec((3, A_WIDTH), const), pl.BlockSpec((1, GLA_DV), const),
                pl.BlockSpec((None, A_WIDTH, d), lay), pl.BlockSpec((None, GLA_VD, d), lay),
                pl.BlockSpec((None, NA_D, d), lay)]
    blocks = 2 * (_nbytes((tm, N_BRANCH * d), F32) + 3 * _nbytes((tm, A_WIDTH), F32) + 4 * _nbytes((tm, GLA_VD), F32)
                  + _nbytes((A_WIDTH + GLA_VD + NA_D, d), BF16) + _nbytes((tm, d), BF16)) + 6 * _nbytes((tm, d), F32)
    return pl.pallas_call(
        functools.partial(_branch_kernel, tiles_per_seq=tiles_per_seq),
        grid=(m_rows // tm,),
        in_specs=in_specs,
        out_specs=pl.BlockSpec((tm, d), cur(0)),
        out_shape=jax.ShapeDtypeStruct((m_rows, d), BF16),
        compiler_params=_params(("arbitrary",), blocks),
        name="branch_merge",
    )(p, p, p, p, p, p, p, p, og_f, og_b, p, na, conv_w, g_norm.reshape(1, GLA_DV), wa, wg, wn)


def _proj_kernel(y_ref, w_ref, h_ref, gm_ref, o_ref):
    o_ref[...] = h_ref[...] + gm_ref[0] * jnp.dot(y_ref[...], w_ref[...], preferred_element_type=F32)


def _proj_residual(y, w, layer, h, gate, group_of_tile, tm, tn):
    m_rows, d = h.shape
    blocks = 2 * (_nbytes((tm, d), BF16) + _nbytes((d, tn), BF16) + 2 * _nbytes((tm, tn), F32)) + _nbytes((tm, tn), F32)
    return pl.pallas_call(
        _proj_kernel,
        grid=(m_rows // tm, d // tn),
        in_specs=[pl.BlockSpec((tm, d), lambda i, j: (i, 0)),
                  pl.BlockSpec((None, d, tn), lambda i, j: (layer, 0, j)),
                  pl.BlockSpec((tm, tn), lambda i, j: (i, j)),
                  pl.BlockSpec((1, 1, tn), lambda i, j: (group_of_tile(i), 0, j))],
        out_specs=pl.BlockSpec((tm, tn), lambda i, j: (i, j)),
        out_shape=jax.ShapeDtypeStruct((m_rows, d), F32),
        compiler_params=_params(("parallel", "arbitrary"), blocks),
        name="out_proj",
    )(y, w, h, gate)


def _ffn_kernel(x_ref, mod_ref, g_ref, w1_ref, w3_ref, w2_ref, gfin_ref, o_ref, n_scr, acc_scr, *, final_norm):
    j = pl.program_id(1)

    @pl.when(j == 0)
    def _():
        m = mod_ref[0]
        n_scr[...] = _mod_norm(x_ref[...], g_ref[...], m[3:4], m[4:5]).astype(BF16)
        acc_scr[...] = jnp.zeros_like(acc_scr)

    n = n_scr[...]
    a = jnp.dot(n, w1_ref[...], preferred_element_type=F32)
    b = jnp.dot(n, w3_ref[...], preferred_element_type=F32)
    u = (a * jax.nn.sigmoid(a)) * b
    acc_scr[...] += jnp.dot(u.astype(BF16), w2_ref[...], preferred_element_type=F32)

    @pl.when(j == pl.num_programs(1) - 1)
    def _():
        out = x_ref[...] + mod_ref[0][5:6] * acc_scr[...]
        if final_norm:
            out = out * lax.rsqrt(jnp.mean(out * out, axis=-1, keepdims=True) + RMS_EPS) * gfin_ref[...]
        o_ref[...] = out


def _ffn(h, mod_l, g, w1, w3, w2, layer, g_final, group_of_tile, tm, tf, final_norm):
    m_rows, d = h.shape
    f = w1.shape[2]
    blocks = (4 * _nbytes((tm, d), F32) + 6 * _nbytes((d, tf), BF16) + _nbytes((tm, d), BF16) + _nbytes((tm, d), F32)
              + 4 * _nbytes((tm, tf), F32))
    return pl.pallas_call(
        functools.partial(_ffn_kernel, final_norm=final_norm),
        grid=(m_rows // tm, f // tf),
        in_specs=[pl.BlockSpec((tm, d), lambda i, j: (i, 0)),
                  pl.BlockSpec((1, 6, d), lambda i, j: (group_of_tile(i), 0, 0)),
                  pl.BlockSpec((1, d), lambda i, j: (0, 0)),
                  pl.BlockSpec((None, d, tf), lambda i, j: (layer, 0, j)),
                  pl.BlockSpec((None, d, tf), lambda i, j: (layer, 0, j)),
                  pl.BlockSpec((None, tf, d), lambda i, j: (layer, j, 0)),
                  pl.BlockSpec((1, d), lambda i, j: (0, 0))],
        out_specs=pl.BlockSpec((tm, d), lambda i, j: (i, 0)),
        out_shape=jax.ShapeDtypeStruct((m_rows, d), F32),
        scratch_shapes=[pltpu.VMEM((tm, d), BF16), pltpu.VMEM((tm, d), F32)],
        compiler_params=_params(("parallel", "arbitrary"), blocks),
        name="ffn",
    )(h, mod_l, g.reshape(1, d), w1, w3, w2, g_final.reshape(1, d))


def _largest_tile(total, cap, align):
    t = min(total, cap)
    while total % t or t % align:
        t -= align
    return t


def _rope_tables(seq_len):
    pos = jnp.arange(seq_len, dtype=jnp.int32)
    rowf = (pos // GRID_W).astype(F32)
    colf = (pos % GRID_W).astype(F32)
    per_axis = GLA_DK // 2
    inv = ROPE_BASE ** (-jnp.arange(0, per_axis, 2, dtype=F32) / per_axis)
    ang = jnp.concatenate([rowf[:, None] * inv, colf[:, None] * inv], axis=-1)
    cos, sin = jnp.cos(ang), jnp.sin(ang)
    reps = LANE // GLA_DK
    return jnp.tile(cos, (1, 2 * reps)), jnp.tile(jnp.concatenate([-sin, sin], axis=-1), (1, reps))


def _pack_w_in(w_in, off, n_pad):
    depth, d, _ = w_in.shape
    sizes = (A_WIDTH, A_WIDTH, A_WIDTH, GLA_KD, GLA_KD, GLA_VD, GLA_VD, GLA_LOWRANK, GLA_LOWRANK, NA_D, NA_D, NA_D,
             N_BRANCH * d)
    names = ("a_in", "a_b", "a_c", "gq", "gk", "gv", "gr", "gaf", "gab", "nq", "nk", "nv", "gate")
    starts = np.concatenate([[0], np.cumsum(sizes)])
    src_cols = {name: (int(s), int(s) + width) for name, s, width in zip(names, starts[:-1], sizes)}
    zeros = lambda width: jnp.zeros((depth, d, width), BF16)
    pieces, pos = [], 0
    for name in ("gate", "a_in", "a_b", "a_c", "gq", "gk", "gv", "gr", "gaf", "gab", "nq", "nk", "nv"):
        dst = off["gab"] + GLA_LOWRANK if name == "gab" else off["gab"] if name == "gaf" else off[name]
        if dst > pos:
            pieces.append(zeros(dst - pos))
        lo, hi = src_cols[name]
        pieces.append(w_in[:, :, lo:hi].astype(BF16))
        pos = dst + hi - lo
        if name == "gab":
            pieces.append(zeros(off["gab"] + LANE - pos))
            pos = off["gab"] + LANE
    pieces.append(zeros(n_pad - pos))
    return jnp.concatenate(pieces, axis=-1)


def _pad_lowrank(wa, col0):
    return jnp.zeros((LANE, GLA_KD), BF16).at[col0:col0 + GLA_LOWRANK].set(wa.astype(BF16))


def kernel(x, c, ctx, c_ctx, w_ada, b_ada, g_mix, g_ffn, w_in, conv_w, gla_wa_f, gla_ba_f, gla_wa_b, gla_ba_b,
           gla_g_norm, na_rpb, w_a_out, w_g_out, w_n_out, w_o, w_ffn1, w_ffn3, w_ffn2, g_final):
    bsz, seq, d = x.shape
    ct = ctx.shape[1]
    depth = w_ada.shape[0]
    f = w_ffn1.shape[2]
    gla_block = GLA_CHUNK * GLA_CHUNKS_PER_STEP
    assert bsz + 1 <= MOD_ROWS and seq % GRID_W == 0 and seq % gla_block == 0 and ct % gla_block == 0
    assert (seq // GRID_W) % (NA_GROUP * NA_GROUPS_PER_STEP) == 0 and seq // GRID_W > NA_KEY_ROWS
    off, n_used = _layout(d)
    tn_in = 10 * LANE
    n_pad = -(-n_used // tn_in) * tn_in

    tm_big = _largest_tile(seq, 1024, SUBLANE)
    tm_ffn = _largest_tile(seq, 512, SUBLANE)
    tm_branch = _largest_tile(seq, 256, SUBLANE)
    tn_proj = _largest_tile(d, 1024, LANE)
    tf = _largest_tile(f, 512, LANE)
    lat_group = lambda tm: (lambda i: i // (seq // tm))
    ctx_group = lambda i: bsz

    cvec = jnp.zeros((MOD_ROWS, d), F32).at[:bsz].set(c).at[bsz].set(c_ctx)
    mod = _adaln(cvec, w_ada, b_ada).reshape(depth, MOD_ROWS, 6, d)
    rope = _rope_tables(seq)
    zero_state = jnp.zeros((bsz, GLA_PAIRS, LANE, 2 * GLA_DV), F32)

    w_in_p = _pack_w_in(w_in, off, n_pad)
    wa_o, wg_o, wn_o = w_a_out.astype(BF16), w_g_out.astype(BF16), w_n_out.astype(BF16)
    wo, w1, w3, w2 = w_o.astype(BF16), w_ffn1.astype(BF16), w_ffn3.astype(BF16), w_ffn2.astype(BF16)
    bias_blocks = _na_bias_blocks(na_rpb)

    h = x.reshape(bsz * seq, d)
    hc = ctx.reshape(bsz * ct, d)
    for l in range(depth):
        need_ctx = l < depth - 1
        last = l == depth - 1
        mod_l = mod[l]
        gm = mod_l[:, 2, :].reshape(MOD_ROWS, 1, d)
        waf, wab = _pad_lowrank(gla_wa_f[l], 0), _pad_lowrank(gla_wa_b[l], GLA_LOWRANK)
        baf, bab = gla_ba_f[l].reshape(1, GLA_KD), gla_ba_b[l].reshape(1, GLA_KD)

        p = _inproj(h, mod_l, g_mix[l], w_in_p, l, lat_group(tm_big), tm_big, tn_in)
        pc = _inproj(hc, mod_l, g_mix[l], w_in_p, l, ctx_group, ct, tn_in)
        p3 = p.reshape(bsz, seq, n_pad)
        pc3 = pc.reshape(bsz, ct, n_pad)

        ogc_f, ogc_b, s_f, s_b = _gla(pc3, off, waf, baf, wab, bab, zero_state, zero_state, None)
        og_f, og_b, _, _ = _gla(p3, off, waf, baf, wab, bab, s_f, s_b, rope)
        na = _na(p3, pc3, off, bias_blocks, l)

        y = _branch(p, off, og_f.reshape(bsz * seq, GLA_VD), og_b.reshape(bsz * seq, GLA_VD),
                    na.reshape(bsz * seq, NA_D), conv_w[l], gla_g_norm[l], wa_o, wg_o, wn_o, l, seq, tm_branch)
        h = _proj_residual(y, wo, l, h, gm, lat_group(tm_big), tm_big, tn_proj)
        h = _ffn(h, mod_l, g_ffn[l], w1, w3, w2, l, g_final, lat_group(tm_ffn), tm_ffn, tf, last)
        if need_ctx:
            na_c = _ctx_attn(pc3, off)
            yc = _branch(pc, off, ogc_f.reshape(bsz * ct, GLA_VD), ogc_b.reshape(bsz * ct, GLA_VD),
                         na_c.reshape(bsz * ct, NA_D), conv_w[l], gla_g_norm[l], wa_o, wg_o, wn_o, l, ct, ct)
            hc = _proj_residual(yc, wo, l, hc, gm, ctx_group, ct, tn_proj)
            hc = _ffn(hc, mod_l, g_ffn[l], w1, w3, w2, l, g_final, ctx_group, ct, tf, False)
    return h.reshape(bsz, seq, d)
```

```python
import functools

import numpy as np
import jax
import jax.numpy as jnp
from jax import lax
from jax.experimental import pallas as pl
from jax.experimental.pallas import tpu as pltpu

GRID_W = 64
RMS_EPS = 1e-6
ROPE_BASE = 10000.0
N_BRANCH = 3
A_WIDTH = 512
GLA_HEADS = 6
GLA_DK = 64
GLA_DV = 128
GLA_KD = GLA_HEADS * GLA_DK
GLA_VD = GLA_HEADS * GLA_DV
GLA_LOWRANK = 16
GLA_TAU = 16.0
GLA_CHUNK = 64
GLA_CHUNKS_PER_STEP = 2
GLA_SAFE_EXPONENT = 80.0
GLA_PAIRS = GLA_HEADS // 2
NA_HEADS = 6
NA_DH = 128
NA_D = NA_HEADS * NA_DH
NA_WIN_H = 8
NA_WIN_W = 16
NA_GROUP = 4
NA_KEY_ROWS = NA_GROUP + NA_WIN_H - 1
NA_GROUPS_PER_STEP = 4
MASK_VALUE = -1e30

LANE = 128
SUBLANE = 8
HALO_ROWS = 16
VMEM_PHYSICAL = 64 * 1024 * 1024
VMEM_INTERNAL = 12 * 1024 * 1024
MOD_ROWS = 8

BF16 = jnp.bfloat16
F32 = jnp.float32


def _params(semantics, block_bytes):
    limit = min(int(block_bytes) + VMEM_INTERNAL, VMEM_PHYSICAL - 4 * 1024 * 1024)
    return pltpu.CompilerParams(dimension_semantics=semantics, vmem_limit_bytes=limit)


def _nbytes(shape, dtype):
    return int(np.prod(shape)) * jnp.dtype(dtype).itemsize


def _layout(d_model):
    off = {}
    pos = 0
    for name, width, block in (("gate", N_BRANCH * d_model, N_BRANCH * d_model),
                               ("a_in", A_WIDTH, A_WIDTH), ("a_b", A_WIDTH, A_WIDTH), ("a_c", A_WIDTH, A_WIDTH),
                               ("gq", GLA_KD, GLA_KD), ("gk", GLA_KD, GLA_KD), ("gv", GLA_VD, GLA_VD),
                               ("gr", GLA_VD, GLA_VD), ("gab", LANE, LANE),
                               ("nq", NA_D, NA_DH), ("nk", NA_D, NA_DH), ("nv", NA_D, NA_DH)):
        assert pos % block == 0, (name, pos, block)
        off[name] = pos
        pos += width
    return off, pos


def _adaln_kernel(c_ref, w_ref, b_ref, o_ref):
    cv = c_ref[...]
    s = cv * jax.nn.sigmoid(cv)
    o_ref[0] = jnp.dot(s, w_ref[0], preferred_element_type=F32) + b_ref[0]


def _adaln(cvec, w_ada, b_ada):
    depth, d, n = w_ada.shape
    tn = 1024 if n % 1024 == 0 else n
    blocks = 2 * (_nbytes((d, tn), F32) + 2 * _nbytes((MOD_ROWS, tn), F32)) + _nbytes((MOD_ROWS, d), F32)
    return pl.pallas_call(
        _adaln_kernel,
        grid=(depth, n // tn),
        in_specs=[pl.BlockSpec((MOD_ROWS, d), lambda l, j: (0, 0)),
                  pl.BlockSpec((1, d, tn), lambda l, j: (l, 0, j)),
                  pl.BlockSpec((1, 1, tn), lambda l, j: (l, 0, j))],
        out_specs=pl.BlockSpec((1, MOD_ROWS, tn), lambda l, j: (l, 0, j)),
        out_shape=jax.ShapeDtypeStruct((depth, MOD_ROWS, n), F32),
        compiler_params=_params(("arbitrary", "arbitrary"), blocks),
        name="adaln",
    )(cvec, w_ada, b_ada.reshape(depth, 1, n))


def _mod_norm(x, g, shift, scale):
    y = x * lax.rsqrt(jnp.mean(x * x, axis=-1, keepdims=True) + RMS_EPS)
    return (y * g) * (1.0 + scale) + shift


def _inproj_kernel(x_ref, mod_ref, g_ref, w_ref, o_ref, n_scr):
    @pl.when(pl.program_id(1) == 0)
    def _():
        m = mod_ref[0]
        n_scr[...] = _mod_norm(x_ref[...], g_ref[...], m[0:1], m[1:2]).astype(BF16)

    o_ref[...] = jnp.dot(n_scr[...], w_ref[...], preferred_element_type=F32).astype(o_ref.dtype)


def _inproj(h, mod_l, g, w, layer, group_of_tile, tm, tn):
    m_rows, d = h.shape
    n = w.shape[2]
    blocks = (2 * (_nbytes((tm, d), F32) + _nbytes((d, tn), BF16) + _nbytes((tm, tn), BF16))
              + _nbytes((tm, d), BF16) + 2 * _nbytes((8, d), F32) + _nbytes((tm, tn), F32))
    return pl.pallas_call(
        _inproj_kernel,
        grid=(m_rows // tm, n // tn),
        in_specs=[pl.BlockSpec((tm, d), lambda i, j: (i, 0)),
                  pl.BlockSpec((1, 6, d), lambda i, j: (group_of_tile(i), 0, 0)),
                  pl.BlockSpec((1, d), lambda i, j: (0, 0)),
                  pl.BlockSpec((None, d, tn), lambda i, j: (layer, 0, j))],
        out_specs=pl.BlockSpec((tm, tn), lambda i, j: (i, j)),
        out_shape=jax.ShapeDtypeStruct((m_rows, n), BF16),
        scratch_shapes=[pltpu.VMEM((tm, d), BF16)],
        compiler_params=_params(("parallel", "arbitrary"), blocks),
        name="inproj",
    )(h, mod_l, g.reshape(1, d), w)


def _rope_pairs(x, cos, sin):
    lane = lax.broadcasted_iota(jnp.int32, cos.shape, 1)
    first_half = (lane % GLA_DK) < (GLA_DK // 2)
    out = []
    for p in range(GLA_PAIRS):
        xs = x[:, p * LANE:(p + 1) * LANE]
        swapped = jnp.where(first_half, pltpu.roll(xs, LANE - GLA_DK // 2, axis=1), pltpu.roll(xs, GLA_DK // 2, axis=1))
        out.append(xs * cos + swapped * sin)
    return jnp.concatenate(out, axis=1)


def _per_chunk_rows(x, offset):
    rows = x.shape[0]
    chunk = lax.broadcasted_iota(jnp.int32, (rows, 1), 0) // GLA_CHUNK
    out = x[offset:offset + 1]
    for c in range(1, rows // GLA_CHUNK):
        out = jnp.where(chunk >= c, x[c * GLA_CHUNK + offset:c * GLA_CHUNK + offset + 1], out)
    return out


def _gla_prepare(q, k, ab, wa, ba, cos, sin, reverse):
    rows = q.shape[0]
    z = jnp.dot(ab, wa, preferred_element_type=F32) + ba
    la = (jnp.minimum(z, 0.0) - jnp.log(1.0 + jnp.exp(-jnp.abs(z)))) * (1.0 / GLA_TAU)
    r = lax.broadcasted_iota(jnp.int32, (rows, rows), 0)
    c = lax.broadcasted_iota(jnp.int32, (rows, rows), 1)
    tri = ((r // GLA_CHUNK == c // GLA_CHUNK) & ((c >= r) if reverse else (c <= r))).astype(BF16)
    la_hi = la.astype(BF16)
    la_lo = (la - la_hi.astype(F32)).astype(BF16)
    b = jnp.dot(tri, la_hi, preferred_element_type=F32) + jnp.dot(tri, la_lo, preferred_element_type=F32)
    q = q.astype(F32)
    k = k.astype(F32)
    if cos is not None:
        q = _rope_pairs(q, cos, sin)
        k = _rope_pairs(k, cos, sin)
    return q, k, b


def _gla_block(q, k, b, v, s_scr, o_ref, inter_scr, reverse):
    rows = q.shape[0]
    n_chunks = rows // GLA_CHUNK
    half = GLA_CHUNK // 2
    rel = b - _per_chunk_rows(b, half if reverse else half - 1)
    last = _per_chunk_rows(b, 0 if reverse else GLA_CHUNK - 1)
    q_mid = (q * jnp.exp(rel))
    k_mid = (k * jnp.exp(-rel)).astype(BF16)
    q_state = (q * jnp.exp(b)).astype(BF16)
    k_last = (k * jnp.exp(last - b)).astype(BF16)
    state_decay = jnp.exp(last)
    row = lax.broadcasted_iota(jnp.int32, (2 * GLA_CHUNK, GLA_CHUNK), 0) % GLA_CHUNK
    col = lax.broadcasted_iota(jnp.int32, (2 * GLA_CHUNK, GLA_CHUNK), 1)
    visible = (row <= col) if reverse else (row >= col)
    first_head = lax.broadcasted_iota(jnp.int32, (rows, LANE), 1) < GLA_DK
    blk_r = lax.broadcasted_iota(jnp.int32, (LANE, 2 * GLA_DV), 0) < GLA_DK
    blk_c = lax.broadcasted_iota(jnp.int32, (LANE, 2 * GLA_DV), 1) < GLA_DV
    same_head = blk_r == blk_c
    nt = (((1,), (1,)), ((), ()))
    tn = (((0,), (0,)), ((), ()))
    q_first = [jnp.where(first_head, q_mid[:, p * LANE:(p + 1) * LANE], 0.0).astype(BF16) for p in range(GLA_PAIRS)]
    q_second = [jnp.where(first_head, 0.0, q_mid[:, p * LANE:(p + 1) * LANE]).astype(BF16) for p in range(GLA_PAIRS)]
    order = range(n_chunks - 1, -1, -1) if reverse else range(n_chunks)
    for c in order:
        rs = slice(c * GLA_CHUNK, (c + 1) * GLA_CHUNK)
        intra, inter = [], []
        for p in range(GLA_PAIRS):
            sl = slice(p * LANE, (p + 1) * LANE)
            q_stack = jnp.concatenate([q_first[p][rs], q_second[p][rs]], axis=0)
            att = lax.dot_general(q_stack, k_mid[rs, sl], nt, preferred_element_type=F32)
            att = jnp.where(visible, att, 0.0).astype(BF16)
            v_pair = v[rs, 2 * p * GLA_DV:(2 * p + 2) * GLA_DV]
            intra.append(jnp.dot(att[:GLA_CHUNK], v_pair[:, :GLA_DV], preferred_element_type=F32))
            intra.append(jnp.dot(att[GLA_CHUNK:], v_pair[:, GLA_DV:], preferred_element_type=F32))
            s_p = s_scr[p]
            inter.append(jnp.dot(q_state[rs, sl], s_p.astype(BF16), preferred_element_type=F32))
            update = lax.dot_general(k_last[rs, sl], v_pair, tn, preferred_element_type=F32)
            decay_row = state_decay[c * GLA_CHUNK:c * GLA_CHUNK + 1, sl]
            decay_col = jnp.transpose(jnp.broadcast_to(decay_row, (SUBLANE, LANE)))[:, 0:1]
            s_scr[p] = s_p * decay_col + jnp.where(same_head, update, 0.0)
        inter = jnp.concatenate(inter, axis=1)
        o_ref[0, rs, :] = jnp.concatenate(intra, axis=1) + inter
        inter_scr[rs, :] = inter
    return jnp.max(jnp.abs(rel))


def _gla_chunk_exact(q_scr, k_scr, b_scr, v_scr, o_ref, row0, reverse):
    length = GLA_CHUNK
    rows = pl.ds(row0, length)
    q = q_scr[rows, :]
    b = b_scr[rows, :]
    row = lax.broadcasted_iota(jnp.int32, (length, 1), 0)
    first_head = lax.broadcasted_iota(jnp.int32, (length, LANE), 1) < GLA_DK

    def body(s, carry):
        src = pl.ds(row0 + s, 1)
        w = q * k_scr[src, :] * jnp.exp(jnp.minimum(b - b_scr[src, :], 0.0))
        w = jnp.where((row <= s) if reverse else (row >= s), w, 0.0)
        v_row = v_scr[src, :]
        for p in range(GLA_PAIRS):
            wp = w[:, p * LANE:(p + 1) * LANE]
            for hh in range(2):
                a = jnp.sum(jnp.where(first_head == (hh == 0), wp, 0.0), axis=1, keepdims=True)
                lanes = slice((2 * p + hh) * GLA_DV, (2 * p + hh + 1) * GLA_DV)
                o_ref[0, rows, lanes] += a * v_row[:, lanes]
        return carry

    lax.fori_loop(0, length, body, 0)


def _gla_kernel(*refs, use_rope):
    qf, kf, vf, abf, qb, kb, vb, abb = refs[:8]
    pos = 8
    if use_rope:
        cosf, sinf, cosb, sinb = refs[pos:pos + 4]
        pos += 4
    (waf, baf, wab, bab, s0f, s0b, of_ref, ob_ref, sf_ref, sb_ref,
     sf_scr, sb_scr, inter_f, inter_b, q_scr, k_scr, b_scr, v_scr) = refs[pos:]
    step = pl.program_id(1)

    @pl.when(step == 0)
    def _():
        sf_scr[...] = s0f[0]
        sb_scr[...] = s0b[0]

    directions = ((False, (qf, kf, vf, abf), (cosf, sinf) if use_rope else None, waf, baf, sf_scr, of_ref, inter_f),
                  (True, (qb, kb, vb, abb), (cosb, sinb) if use_rope else None, wab, bab, sb_scr, ob_ref, inter_b))

    def prepared(reverse, refs4, rope_refs, w_ref, bias_ref):
        q_ref, k_ref, _, ab_ref = refs4
        cos = rope_refs[0][...] if use_rope else None
        sin = rope_refs[1][...] if use_rope else None
        return _gla_prepare(q_ref[0], k_ref[0], ab_ref[0], w_ref[...], bias_ref[...], cos, sin, reverse)

    margin = jnp.float32(0.0)
    for reverse, refs4, rope_refs, w_ref, bias_ref, s_scr, o_ref, inter_scr in directions:
        q, k, b = prepared(reverse, refs4, rope_refs, w_ref, bias_ref)
        m = _gla_block(q, k, b, refs4[2][0], s_scr, o_ref, inter_scr, reverse)
        margin = jnp.maximum(margin, m)

    @pl.when(margin > GLA_SAFE_EXPONENT)
    def _():
        for reverse, refs4, rope_refs, w_ref, bias_ref, s_scr, o_ref, inter_scr in directions:
            q, k, b = prepared(reverse, refs4, rope_refs, w_ref, bias_ref)
            q_scr[...] = q
            k_scr[...] = k
            b_scr[...] = b
            v_scr[...] = refs4[2][0].astype(F32)
            o_ref[0] = inter_scr[...]
            for c in range(GLA_CHUNKS_PER_STEP):
                _gla_chunk_exact(q_scr, k_scr, b_scr, v_scr, o_ref, c * GLA_CHUNK, reverse)

    @pl.when(step == pl.num_programs(1) - 1)
    def _():
        sf_ref[0] = sf_scr[...]
        sb_ref[0] = sb_scr[...]


def _gla(p3, off, waf, baf, wab, bab, s0f, s0b, rope):
    bsz, t, _ = p3.shape
    length = GLA_CHUNK * GLA_CHUNKS_PER_STEP
    n = t // length
    fwd = lambda blk: (lambda b, i: (b, i, blk))
    bwd = lambda blk: (lambda b, i: (b, n - 1 - i, blk))
    q_blk, k_blk = off["gq"] // GLA_KD, off["gk"] // GLA_KD
    v_blk, ab_blk = off["gv"] // GLA_VD, off["gab"] // LANE
    seq_specs = lambda m: [pl.BlockSpec((1, length, GLA_KD), m(q_blk)), pl.BlockSpec((1, length, GLA_KD), m(k_blk)),
                           pl.BlockSpec((1, length, GLA_VD), m(v_blk)), pl.BlockSpec((1, length, LANE), m(ab_blk))]
    in_specs = seq_specs(fwd) + seq_specs(bwd)
    args = [p3] * 8
    if rope is not None:
        in_specs += [pl.BlockSpec((length, LANE), lambda b, i: (i, 0))] * 2
        in_specs += [pl.BlockSpec((length, LANE), lambda b, i: (n - 1 - i, 0))] * 2
        args += [rope[0], rope[1], rope[0], rope[1]]
    const2 = lambda b, i: (0, 0)
    state_dims = (GLA_PAIRS, LANE, 2 * GLA_DV)
    state_spec = pl.BlockSpec((1,) + state_dims, lambda b, i: (b, 0, 0, 0))
    in_specs += [pl.BlockSpec((LANE, GLA_KD), const2), pl.BlockSpec((1, GLA_KD), const2),
                 pl.BlockSpec((LANE, GLA_KD), const2), pl.BlockSpec((1, GLA_KD), const2),
                 state_spec, state_spec]
    args += [waf, baf, wab, bab, s0f, s0b]
    state_shape = jax.ShapeDtypeStruct((bsz,) + state_dims, F32)
    out_shape = jax.ShapeDtypeStruct((bsz, t, GLA_VD), F32)
    blocks = 2 * (2 * (2 * _nbytes((length, GLA_KD), F32) + 2 * _nbytes((length, GLA_VD), F32) + 5 * _nbytes((length, LANE), F32))
                  + 4 * _nbytes(state_dims, F32) + 2 * _nbytes((LANE, GLA_KD), F32))
    blocks += 2 * _nbytes(state_dims, F32) + 2 * _nbytes((length, GLA_VD), F32)
    return pl.pallas_call(
        functools.partial(_gla_kernel, use_rope=rope is not None),
        grid=(bsz, n),
        in_specs=in_specs,
        out_specs=[pl.BlockSpec((1, length, GLA_VD), fwd(0)), pl.BlockSpec((1, length, GLA_VD), bwd(0)),
                   state_spec, state_spec],
        out_shape=[out_shape, out_shape, state_shape, state_shape],
        scratch_shapes=[pltpu.VMEM(state_dims, F32), pltpu.VMEM(state_dims, F32),
                        pltpu.VMEM((length, GLA_VD), F32), pltpu.VMEM((length, GLA_VD), F32),
                        pltpu.VMEM((length, GLA_KD), F32), pltpu.VMEM((length, GLA_KD), F32),
                        pltpu.VMEM((length, GLA_KD), F32), pltpu.VMEM((length, GLA_VD), F32)],
        compiler_params=_params(("arbitrary", "arbitrary"), blocks),
        name="gla",
    )(*args)


def _na_kernel(q_ref, k_ref, v_ref, kc_ref, vc_ref, blk_ref, o_ref, tab_scr, *, rows):
    nt = (((1,), (1,)), ((), ()))
    step = pl.program_id(2)

    @pl.when(step == 0)
    def _():
        masked = jnp.full((GRID_W, GRID_W), MASK_VALUE, F32)
        for variant, per_query in enumerate(_na_row_offsets(rows)):
            for rq, offsets in enumerate(per_query):
                tab_scr[variant, rq * GRID_W:(rq + 1) * GRID_W, :] = jnp.concatenate(
                    [masked if r is None else blk_ref[0, r] for r in offsets], axis=1)

    kc = kc_ref[0]
    vc = vc_ref[0]
    nq = NA_GROUP * GRID_W
    nk = NA_KEY_ROWS * GRID_W
    for g in range(NA_GROUPS_PER_STEP):
        r0 = (step * NA_GROUPS_PER_STEP + g) * NA_GROUP
        key_row0 = jnp.clip(r0 - NA_WIN_H // 2, 0, rows - NA_KEY_ROWS)
        k_off = pl.multiple_of(key_row0 * GRID_W, GRID_W)
        variant = jnp.where(r0 == 0, 0, jnp.where(r0 == rows - NA_GROUP, 2, 1))
        q = q_ref[0, g * nq:(g + 1) * nq, :]
        kr = k_ref[0, pl.ds(k_off, nk), :]
        vr = v_ref[0, pl.ds(k_off, nk), :]
        s_loc = lax.dot_general(q, kr, nt, preferred_element_type=F32) + tab_scr[variant]
        s_ctx = lax.dot_general(q, kc, nt, preferred_element_type=F32)
        m = jnp.maximum(jnp.max(s_loc, axis=-1, keepdims=True), jnp.max(s_ctx, axis=-1, keepdims=True))
        p_loc = jnp.exp(s_loc - m)
        p_ctx = jnp.exp(s_ctx - m)
        denom = jnp.sum(p_loc, axis=-1, keepdims=True) + jnp.sum(p_ctx, axis=-1, keepdims=True)
        o = (jnp.dot(p_loc.astype(BF16), vr, preferred_element_type=F32)
             + jnp.dot(p_ctx.astype(BF16), vc, preferred_element_type=F32))
        o_ref[0, g * nq:(g + 1) * nq, :] = (o / denom).astype(o_ref.dtype)


def _na(p3, pc3, off, bias_blocks, layer):
    bsz, t, _ = p3.shape
    ct = pc3.shape[1]
    rows = t // GRID_W
    tq = NA_GROUPS_PER_STEP * NA_GROUP * GRID_W
    nq, nk = NA_GROUP * GRID_W, NA_KEY_ROWS * GRID_W
    n_off = 2 * NA_WIN_H - 1
    q0, k0, v0 = off["nq"] // NA_DH, off["nk"] // NA_DH, off["nv"] // NA_DH
    blocks = (2 * (2 * _nbytes((tq, NA_DH), F32) + 2 * _nbytes((t, NA_DH), BF16) + 2 * _nbytes((ct, NA_DH), BF16)
                   + _nbytes((n_off, GRID_W, LANE), F32)) + _nbytes((3, nq, nk), F32)
              + 3 * NA_GROUPS_PER_STEP * _nbytes((nq, nk + ct), F32))
    return pl.pallas_call(
        functools.partial(_na_kernel, rows=rows),
        grid=(bsz, NA_HEADS, rows // (NA_GROUPS_PER_STEP * NA_GROUP)),
        in_specs=[pl.BlockSpec((1, tq, NA_DH), lambda b, h, r: (b, r, q0 + h)),
                  pl.BlockSpec((1, t, NA_DH), lambda b, h, r: (b, 0, k0 + h)),
                  pl.BlockSpec((1, t, NA_DH), lambda b, h, r: (b, 0, v0 + h)),
                  pl.BlockSpec((1, ct, NA_DH), lambda b, h, r: (b, 0, k0 + h)),
                  pl.BlockSpec((1, ct, NA_DH), lambda b, h, r: (b, 0, v0 + h)),
                  pl.BlockSpec((None, 1, n_off, GRID_W, GRID_W), lambda b, h, r: (layer, h, 0, 0, 0))],
        out_specs=pl.BlockSpec((1, tq, NA_DH), lambda b, h, r: (b, r, h)),
        out_shape=jax.ShapeDtypeStruct((bsz, t, NA_D), BF16),
        scratch_shapes=[pltpu.VMEM((3, nq, nk), F32)],
        compiler_params=_params(("arbitrary", "arbitrary", "arbitrary"), blocks),
        name="natten",
    )(p3, p3, p3, pc3, pc3, bias_blocks)


def _na_row_offsets(rows):
    table = []
    for r0 in (0, NA_GROUP, rows - NA_GROUP):
        key_row0 = int(np.clip(r0 - NA_WIN_H // 2, 0, rows - NA_KEY_ROWS))
        per_query = []
        for rq in range(NA_GROUP):
            r = r0 + rq
            r_start = int(np.clip(r - NA_WIN_H // 2, 0, rows - NA_WIN_H))
            per_query.append([key_row0 + ki - r + (NA_WIN_H - 1) if r_start <= key_row0 + ki < r_start + NA_WIN_H
                              else None for ki in range(NA_KEY_ROWS)])
        table.append(per_query)
    return table


def _na_bias_blocks(rpb):
    w = GRID_W
    col = np.arange(w)
    col_start = np.clip(col - NA_WIN_W // 2, 0, w - NA_WIN_W)
    valid_col = (col[None, :] >= col_start[:, None]) & (col[None, :] < col_start[:, None] + NA_WIN_W)
    col_off = col[None, :] - col[:, None] + (NA_WIN_W - 1)
    col_sel = np.zeros((2 * NA_WIN_W - 1, w, w), np.float32)
    cc, jj = np.nonzero(valid_col)
    col_sel[col_off[cc, jj], cc, jj] = 1.0
    blocks = jnp.einsum("lhrx,xcj->lhrcj", rpb.astype(F32), col_sel, precision=lax.Precision.HIGHEST)
    return jnp.where(valid_col, blocks, MASK_VALUE)


def _ctx_attn_kernel(q_ref, k_ref, v_ref, o_ref):
    s = lax.dot_general(q_ref[0], k_ref[0], (((1,), (1,)), ((), ())), preferred_element_type=F32)
    p = jnp.exp(s - jnp.max(s, axis=-1, keepdims=True))
    o = jnp.dot(p.astype(BF16), v_ref[0], preferred_element_type=F32)
    o_ref[0] = (o / jnp.sum(p, axis=-1, keepdims=True)).astype(o_ref.dtype)


def _ctx_attn(pc3, off):
    bsz, ct, _ = pc3.shape
    q0, k0, v0 = off["nq"] // NA_DH, off["nk"] // NA_DH, off["nv"] // NA_DH
    spec = lambda c0: pl.BlockSpec((1, ct, NA_DH), lambda b, h: (b, 0, c0 + h))
    return pl.pallas_call(
        _ctx_attn_kernel,
        grid=(bsz, NA_HEADS),
        in_specs=[spec(q0), spec(k0), spec(v0)],
        out_specs=spec(0),
        out_shape=jax.ShapeDtypeStruct((bsz, ct, NA_D), BF16),
        compiler_params=_params(("arbitrary", "arbitrary"), 8 * _nbytes((ct, NA_DH), F32)),
        name="ctx_attn",
    )(pc3, pc3, pc3)


def _branch_kernel(gate_ref, ain_ref, ab_ref, ac_ref, ainp_ref, acp_ref, ainn_ref, acn_ref,
                   of_ref, ob_ref, gr_ref, na_ref, cw_ref, gn_ref, wa_ref, wg_ref, wn_ref, y_ref, *, tiles_per_seq):
    tm = ain_ref.shape[0]
    d = y_ref.shape[1]
    i = pl.program_id(0)
    tile_in_seq = i % tiles_per_seq
    u = ac_ref[...].astype(F32) * ain_ref[...].astype(F32)
    u_before = jnp.where(tile_in_seq == 0, 0.0, (acp_ref[...].astype(F32) * ainp_ref[...].astype(F32))[HALO_ROWS - 1:HALO_ROWS])
    u_after = jnp.where(tile_in_seq == tiles_per_seq - 1, 0.0, (acn_ref[...].astype(F32) * ainn_ref[...].astype(F32))[0:1])
    row = lax.broadcasted_iota(jnp.int32, u.shape, 0)
    u_prev = jnp.where(row == 0, u_before, pltpu.roll(u, 1, axis=0))
    u_next = jnp.where(row == tm - 1, u_after, pltpu.roll(u, tm - 1, axis=0))
    cw = cw_ref[...]
    a = ab_ref[...].astype(F32) * (cw[0:1] * u_prev + cw[1:2] * u + cw[2:3] * u_next)
    o = of_ref[...] + ob_ref[...]
    gn = gn_ref[...]
    normed = []
    for h in range(GLA_HEADS):
        oh = o[:, h * GLA_DV:(h + 1) * GLA_DV]
        normed.append(oh * lax.rsqrt(jnp.mean(oh * oh, axis=-1, keepdims=True) + RMS_EPS) * gn)
    r = gr_ref[...].astype(F32)
    g = jnp.concatenate(normed, axis=1) * (r * jax.nn.sigmoid(r))
    y = jax.nn.sigmoid(gate_ref[:, 0:d].astype(F32)) * jnp.dot(a.astype(BF16), wa_ref[...], preferred_element_type=F32)
    y = y + jax.nn.sigmoid(gate_ref[:, d:2 * d].astype(F32)) * jnp.dot(g.astype(BF16), wg_ref[...], preferred_element_type=F32)
    y = y + jax.nn.sigmoid(gate_ref[:, 2 * d:3 * d].astype(F32)) * jnp.dot(na_ref[...], wn_ref[...],
                                                                             preferred_element_type=F32)
    y_ref[...] = y.astype(BF16)


def _branch(p, off, og_f, og_b, na, conv_w, g_norm, wa, wg, wn, layer, seq_len, tm):
    m_rows = p.shape[0]
    d = wa.shape[2]
    tiles_per_seq = seq_len // tm
    halo = tm // HALO_ROWS
    n_halo = m_rows // HALO_ROWS
    cur = lambda blk: (lambda i: (i, blk))
    prev = lambda blk: (lambda i: (jnp.maximum(i * halo - 1, 0), blk))
    nxt = lambda blk: (lambda i: (jnp.minimum((i + 1) * halo, n_halo - 1), blk))
    a_in, a_b, a_c = off["a_in"] // A_WIDTH, off["a_b"] // A_WIDTH, off["a_c"] // A_WIDTH
    const = lambda i: (0, 0)
    lay = lambda i: (layer, 0, 0)
    in_specs = [pl.BlockSpec((tm, N_BRANCH * d), cur(off["gate"] // (N_BRANCH * d))),
                pl.BlockSpec((tm, A_WIDTH), cur(a_in)), pl.BlockSpec((tm, A_WIDTH), cur(a_b)),
                pl.BlockSpec((tm, A_WIDTH), cur(a_c)),
                pl.BlockSpec((HALO_ROWS, A_WIDTH), prev(a_in)), pl.BlockSpec((HALO_ROWS, A_WIDTH), prev(a_c)),
                pl.BlockSpec((HALO_ROWS, A_WIDTH), nxt(a_in)), pl.BlockSpec((HALO_ROWS, A_WIDTH), nxt(a_c)),
                pl.BlockSpec((tm, GLA_VD), cur(0)), pl.BlockSpec((tm, GLA_VD), cur(0)),
                pl.BlockSpec((tm, GLA_VD), cur(off["gr"] // GLA_VD)),
                pl.BlockSpec((tm, NA_D), cur(0)),
                pl.BlockSpec((3, A_WIDTH), const), pl.BlockSpec((1, GLA_DV), const),
                pl.BlockSpec((None, A_WIDTH, d), lay), pl.BlockSpec((None, GLA_VD, d), lay),
                pl.BlockSpec((None, NA_D, d), lay)]
    blocks = 2 * (_nbytes((tm, N_BRANCH * d), BF16) + 3 * _nbytes((tm, A_WIDTH), BF16) + 4 * _nbytes((tm, GLA_VD), F32)
                  + _nbytes((A_WIDTH + GLA_VD + NA_D, d), BF16) + _nbytes((tm, d), BF16)) + 6 * _nbytes((tm, d), F32)
    return pl.pallas_call(
        functools.partial(_branch_kernel, tiles_per_seq=tiles_per_seq),
        grid=(m_rows // tm,),
        in_specs=in_specs,
        out_specs=pl.BlockSpec((tm, d), cur(0)),
        out_shape=jax.ShapeDtypeStruct((m_rows, d), BF16),
        compiler_params=_params(("arbitrary",), blocks),
        name="branch_merge",
    )(p, p, p, p, p, p, p, p, og_f, og_b, p, na, conv_w, g_norm.reshape(1, GLA_DV), wa, wg, wn)


def _proj_kernel(y_ref, w_ref, h_ref, gm_ref, o_ref):
    o_ref[...] = h_ref[...] + gm_ref[0] * jnp.dot(y_ref[...], w_ref[...], preferred_element_type=F32)


def _proj_residual(y, w, layer, h, gate, group_of_tile, tm, tn):
    m_rows, d = h.shape
    blocks = 2 * (_nbytes((tm, d), BF16) + _nbytes((d, tn), BF16) + 2 * _nbytes((tm, tn), F32)) + _nbytes((tm, tn), F32)
    return pl.pallas_call(
        _proj_kernel,
        grid=(m_rows // tm, d // tn),
        in_specs=[pl.BlockSpec((tm, d), lambda i, j: (i, 0)),
                  pl.BlockSpec((None, d, tn), lambda i, j: (layer, 0, j)),
                  pl.BlockSpec((tm, tn), lambda i, j: (i, j)),
                  pl.BlockSpec((1, 1, tn), lambda i, j: (group_of_tile(i), 0, j))],
        out_specs=pl.BlockSpec((tm, tn), lambda i, j: (i, j)),
        out_shape=jax.ShapeDtypeStruct((m_rows, d), F32),
        compiler_params=_params(("parallel", "arbitrary"), blocks),
        name="out_proj",
    )(y, w, h, gate)


def _ffn_kernel(x_ref, mod_ref, g_ref, w1_ref, w3_ref, w2_ref, gfin_ref, o_ref, n_scr, acc_scr, *, final_norm):
    j = pl.program_id(1)

    @pl.when(j == 0)
    def _():
        m = mod_ref[0]
        n_scr[...] = _mod_norm(x_ref[...], g_ref[...], m[3:4], m[4:5]).astype(BF16)
        acc_scr[...] = jnp.zeros_like(acc_scr)

    n = n_scr[...]
    a = jnp.dot(n, w1_ref[...], preferred_element_type=F32)
    b = jnp.dot(n, w3_ref[...], preferred_element_type=F32)
    u = (a * jax.nn.sigmoid(a)) * b
    acc_scr[...] += jnp.dot(u.astype(BF16), w2_ref[...], preferred_element_type=F32)

    @pl.when(j == pl.num_programs(1) - 1)
    def _():
        out = x_ref[...] + mod_ref[0][5:6] * acc_scr[...]
        if final_norm:
            out = out * lax.rsqrt(jnp.mean(out * out, axis=-1, keepdims=True) + RMS_EPS) * gfin_ref[...]
        o_ref[...] = out


def _ffn(h, mod_l, g, w1, w3, w2, layer, g_final, group_of_tile, tm, tf, final_norm):
    m_rows, d = h.shape
    f = w1.shape[2]
    blocks = (4 * _nbytes((tm, d), F32) + 6 * _nbytes((d, tf), BF16) + _nbytes((tm, d), BF16) + _nbytes((tm, d), F32)
              + 4 * _nbytes((tm, tf), F32))
    return pl.pallas_call(
        functools.partial(_ffn_kernel, final_norm=final_norm),
        grid=(m_rows // tm, f // tf),
        in_specs=[pl.BlockSpec((tm, d), lambda i, j: (i, 0)),
                  pl.BlockSpec((1, 6, d), lambda i, j: (group_of_tile(i), 0, 0)),
                  pl.BlockSpec((1, d), lambda i, j: (0, 0)),
                  pl.BlockSpec((None, d, tf), lambda i, j: (layer, 0, j)),
                  pl.BlockSpec((None, d, tf), lambda i, j: (layer, 0, j)),
                  pl.BlockSpec((None, tf, d), lambda i, j: (layer, j, 0)),
                  pl.BlockSpec((1, d), lambda i, j: (0, 0))],
        out_specs=pl.BlockSpec((tm, d), lambda i, j: (i, 0)),
        out_shape=jax.ShapeDtypeStruct((m_rows, d), F32),
        scratch_shapes=[pltpu.VMEM((tm, d), BF16), pltpu.VMEM((tm, d), F32)],
        compiler_params=_params(("parallel", "arbitrary"), blocks),
        name="ffn",
    )(h, mod_l, g.reshape(1, d), w1, w3, w2, g_final.reshape(1, d))


def _largest_tile(total, cap, align):
    t = min(total, cap)
    while total % t or t % align:
        t -= align
    return t


def _rope_tables(seq_len):
    pos = jnp.arange(seq_len, dtype=jnp.int32)
    rowf = (pos // GRID_W).astype(F32)
    colf = (pos % GRID_W).astype(F32)
    per_axis = GLA_DK // 2
    inv = ROPE_BASE ** (-jnp.arange(0, per_axis, 2, dtype=F32) / per_axis)
    ang = jnp.concatenate([rowf[:, None] * inv, colf[:, None] * inv], axis=-1)
    cos, sin = jnp.cos(ang), jnp.sin(ang)
    reps = LANE // GLA_DK
    return jnp.tile(cos, (1, 2 * reps)), jnp.tile(jnp.concatenate([-sin, sin], axis=-1), (1, reps))


def _pack_w_in(w_in, off, n_pad):
    depth, d, _ = w_in.shape
    sizes = (A_WIDTH, A_WIDTH, A_WIDTH, GLA_KD, GLA_KD, GLA_VD, GLA_VD, GLA_LOWRANK, GLA_LOWRANK, NA_D, NA_D, NA_D,
             N_BRANCH * d)
    names = ("a_in", "a_b", "a_c", "gq", "gk", "gv", "gr", "gaf", "gab", "nq", "nk", "nv", "gate")
    starts = np.concatenate([[0], np.cumsum(sizes)])
    src_cols = {name: (int(s), int(s) + width) for name, s, width in zip(names, starts[:-1], sizes)}
    zeros = lambda width: jnp.zeros((depth, d, width), BF16)
    pieces, pos = [], 0
    for name in ("gate", "a_in", "a_b", "a_c", "gq", "gk", "gv", "gr", "gaf", "gab", "nq", "nk", "nv"):
        dst = off["gab"] + GLA_LOWRANK if name == "gab" else off["gab"] if name == "gaf" else off[name]
        if dst > pos:
            pieces.append(zeros(dst - pos))
        lo, hi = src_cols[name]
        scale = {"gq": GLA_DK ** -0.5, "nq": NA_DH ** -0.5}.get(name)
        cols = w_in[:, :, lo:hi] if scale is None else w_in[:, :, lo:hi] * scale
        pieces.append(cols.astype(BF16))
        pos = dst + hi - lo
        if name == "gab":
            pieces.append(zeros(off["gab"] + LANE - pos))
            pos = off["gab"] + LANE
    pieces.append(zeros(n_pad - pos))
    return jnp.concatenate(pieces, axis=-1)


def _pad_lowrank(wa, col0):
    return jnp.zeros((LANE, GLA_KD), BF16).at[col0:col0 + GLA_LOWRANK].set(wa.astype(BF16))


def kernel(x, c, ctx, c_ctx, w_ada, b_ada, g_mix, g_ffn, w_in, conv_w, gla_wa_f, gla_ba_f, gla_wa_b, gla_ba_b,
           gla_g_norm, na_rpb, w_a_out, w_g_out, w_n_out, w_o, w_ffn1, w_ffn3, w_ffn2, g_final):
    bsz, seq, d = x.shape
    ct = ctx.shape[1]
    depth = w_ada.shape[0]
    f = w_ffn1.shape[2]
    gla_block = GLA_CHUNK * GLA_CHUNKS_PER_STEP
    assert bsz + 1 <= MOD_ROWS and seq % GRID_W == 0 and seq % gla_block == 0 and ct % gla_block == 0
    assert (seq // GRID_W) % (NA_GROUP * NA_GROUPS_PER_STEP) == 0 and seq // GRID_W > NA_KEY_ROWS
    off, n_used = _layout(d)
    tn_in = 10 * LANE
    n_pad = -(-n_used // tn_in) * tn_in

    tm_big = _largest_tile(seq, 1024, SUBLANE)
    tm_ffn = _largest_tile(seq, 512, SUBLANE)
    tm_branch = _largest_tile(seq, 256, SUBLANE)
    tn_proj = _largest_tile(d, 1024, LANE)
    tf = _largest_tile(f, 512, LANE)
    lat_group = lambda tm: (lambda i: i // (seq // tm))
    ctx_group = lambda i: bsz

    cvec = jnp.zeros((MOD_ROWS, d), F32).at[:bsz].set(c).at[bsz].set(c_ctx)
    mod = _adaln(cvec, w_ada, b_ada).reshape(depth, MOD_ROWS, 6, d)
    rope = _rope_tables(seq)
    zero_state = jnp.zeros((bsz, GLA_PAIRS, LANE, 2 * GLA_DV), F32)

    w_in_p = _pack_w_in(w_in, off, n_pad)
    wa_o, wg_o, wn_o = w_a_out.astype(BF16), w_g_out.astype(BF16), w_n_out.astype(BF16)
    wo, w1, w3, w2 = w_o.astype(BF16), w_ffn1.astype(BF16), w_ffn3.astype(BF16), w_ffn2.astype(BF16)
    bias_blocks = _na_bias_blocks(na_rpb)

    h = x.reshape(bsz * seq, d)
    hc = ctx.reshape(bsz * ct, d)
    for l in range(depth):
        need_ctx = l < depth - 1
        last = l == depth - 1
        mod_l = mod[l]
        gm = mod_l[:, 2, :].reshape(MOD_ROWS, 1, d)
        waf, wab = _pad_lowrank(gla_wa_f[l], 0), _pad_lowrank(gla_wa_b[l], GLA_LOWRANK)
        baf, bab = gla_ba_f[l].reshape(1, GLA_KD), gla_ba_b[l].reshape(1, GLA_KD)

        p = _inproj(h, mod_l, g_mix[l], w_in_p, l, lat_group(tm_big), tm_big, tn_in)
        pc = _inproj(hc, mod_l, g_mix[l], w_in_p, l, ctx_group, ct, tn_in)
        p3 = p.reshape(bsz, seq, n_pad)
        pc3 = pc.reshape(bsz, ct, n_pad)

        ogc_f, ogc_b, s_f, s_b = _gla(pc3, off, waf, baf, wab, bab, zero_state, zero_state, None)
        og_f, og_b, _, _ = _gla(p3, off, waf, baf, wab, bab, s_f, s_b, rope)
        na = _na(p3, pc3, off, bias_blocks, l)

        y = _branch(p, off, og_f.reshape(bsz * seq, GLA_VD), og_b.reshape(bsz * seq, GLA_VD),
                    na.reshape(bsz * seq, NA_D), conv_w[l], gla_g_norm[l], wa_o, wg_o, wn_o, l, seq, tm_branch)
        h = _proj_residual(y, wo, l, h, gm, lat_group(tm_big), tm_big, tn_proj)
        h = _ffn(h, mod_l, g_ffn[l], w1, w3, w2, l, g_final, lat_group(tm_ffn), tm_ffn, tf, last)
        if need_ctx:
            na_c = _ctx_attn(pc3, off)
            yc = _branch(pc, off, ogc_f.reshape(bsz * ct, GLA_VD), ogc_b.reshape(bsz * ct, GLA_VD),
                         na_c.reshape(bsz * ct, NA_D), conv_w[l], gla_g_norm[l], wa_o, wg_o, wn_o, l, ct, ct)
            hc = _proj_residual(yc, wo, l, hc, gm, ctx_group, ct, tn_proj)
            hc = _ffn(hc, mod_l, g_ffn[l], w1, w3, w2, l, g_final, ctx_group, ct, tf, False)
    return h.reshape(bsz, seq, d)
```

```python
import functools

import numpy as np
import jax
import jax.numpy as jnp
from jax import lax
from jax.experimental import pallas as pl
from jax.experimental.pallas import tpu as pltpu

GRID_W = 64
RMS_EPS = 1e-6
ROPE_BASE = 10000.0
N_BRANCH = 3
A_WIDTH = 512
GLA_HEADS = 6
GLA_DK = 64
GLA_DV = 128
GLA_KD = GLA_HEADS * GLA_DK
GLA_VD = GLA_HEADS * GLA_DV
GLA_LOWRANK = 16
GLA_TAU = 16.0
GLA_CHUNK = 64
GLA_CHUNKS_PER_STEP = 4
GLA_SAFE_EXPONENT = 80.0
GLA_PAIRS = GLA_HEADS // 2
NA_HEADS = 6
NA_DH = 128
NA_D = NA_HEADS * NA_DH
NA_WIN_H = 8
NA_WIN_W = 16
NA_GROUP = 4
NA_KEY_ROWS = NA_GROUP + NA_WIN_H - 1
NA_GROUPS_PER_STEP = 4
MASK_VALUE = -1e30

LANE = 128
SUBLANE = 8
HALO_ROWS = 16
VMEM_PHYSICAL = 64 * 1024 * 1024
VMEM_INTERNAL = 12 * 1024 * 1024
MOD_ROWS = 8

BF16 = jnp.bfloat16
F32 = jnp.float32


def _params(semantics, block_bytes):
    limit = min(int(block_bytes) + VMEM_INTERNAL, VMEM_PHYSICAL - 4 * 1024 * 1024)
    return pltpu.CompilerParams(dimension_semantics=semantics, vmem_limit_bytes=limit)


def _nbytes(shape, dtype):
    return int(np.prod(shape)) * jnp.dtype(dtype).itemsize


def _serpentine(i, j, n):
    return jnp.where(i % 2 == 0, j, n - 1 - j)


def _layout(d_model):
    off = {}
    pos = 0
    for name, width, block in (("gate", N_BRANCH * d_model, N_BRANCH * d_model),
                               ("a_in", A_WIDTH, A_WIDTH), ("a_b", A_WIDTH, A_WIDTH), ("a_c", A_WIDTH, A_WIDTH),
                               ("gq", GLA_KD, GLA_KD), ("gk", GLA_KD, GLA_KD), ("gv", GLA_VD, GLA_VD),
                               ("gr", GLA_VD, GLA_VD), ("gab", LANE, LANE),
                               ("nq", NA_D, NA_DH), ("nk", NA_D, NA_DH), ("nv", NA_D, NA_DH)):
        assert pos % block == 0, (name, pos, block)
        off[name] = pos
        pos += width
    return off, pos


def _adaln_kernel(c_ref, w_ref, b_ref, o_ref):
    cv = c_ref[...]
    s = cv * jax.nn.sigmoid(cv)
    o_ref[0] = jnp.dot(s, w_ref[0], preferred_element_type=F32) + b_ref[0]


def _adaln(cvec, w_ada, b_ada):
    depth, d, n = w_ada.shape
    tn = 1024 if n % 1024 == 0 else n
    blocks = 2 * (_nbytes((d, tn), F32) + 2 * _nbytes((MOD_ROWS, tn), F32)) + _nbytes((MOD_ROWS, d), F32)
    return pl.pallas_call(
        _adaln_kernel,
        grid=(depth, n // tn),
        in_specs=[pl.BlockSpec((MOD_ROWS, d), lambda l, j: (0, 0)),
                  pl.BlockSpec((1, d, tn), lambda l, j: (l, 0, j)),
                  pl.BlockSpec((1, 1, tn), lambda l, j: (l, 0, j))],
        out_specs=pl.BlockSpec((1, MOD_ROWS, tn), lambda l, j: (l, 0, j)),
        out_shape=jax.ShapeDtypeStruct((depth, MOD_ROWS, n), F32),
        compiler_params=_params(("arbitrary", "arbitrary"), blocks),
        name="adaln",
    )(cvec, w_ada, b_ada.reshape(depth, 1, n))


def _mod_norm(x, g, shift, scale):
    y = x * lax.rsqrt(jnp.mean(x * x, axis=-1, keepdims=True) + RMS_EPS)
    return (y * g) * (1.0 + scale) + shift


def _inproj_kernel(x_ref, mod_ref, g_ref, w_ref, o_ref, n_scr):
    @pl.when(pl.program_id(1) == 0)
    def _():
        m = mod_ref[0]
        n_scr[...] = _mod_norm(x_ref[...], g_ref[...], m[0:1], m[1:2]).astype(BF16)

    o_ref[...] = lax.dot_general(n_scr[...], w_ref[...], (((1,), (1,)), ((), ())),
                                 preferred_element_type=F32).astype(o_ref.dtype)


def _inproj(h, mod_l, g, w, layer, group_of_tile, tm, tn):
    m_rows, d = h.shape
    n = w.shape[1]
    blocks = (2 * (_nbytes((tm, d), F32) + _nbytes((d, tn), BF16) + _nbytes((tm, tn), BF16))
              + _nbytes((tm, d), BF16) + 2 * _nbytes((8, d), F32) + _nbytes((tm, tn), F32))
    return pl.pallas_call(
        _inproj_kernel,
        grid=(m_rows // tm, n // tn),
        in_specs=[pl.BlockSpec((tm, d), lambda i, j: (i, 0)),
                  pl.BlockSpec((1, 6, d), lambda i, j: (group_of_tile(i), 0, 0)),
                  pl.BlockSpec((1, d), lambda i, j: (0, 0)),
                  pl.BlockSpec((None, tn, d), lambda i, j: (layer, _serpentine(i, j, n // tn), 0))],
        out_specs=pl.BlockSpec((tm, tn), lambda i, j: (i, _serpentine(i, j, n // tn))),
        out_shape=jax.ShapeDtypeStruct((m_rows, n), BF16),
        scratch_shapes=[pltpu.VMEM((tm, d), BF16)],
        compiler_params=_params(("parallel", "arbitrary"), blocks),
        name="inproj",
    )(h, mod_l, g.reshape(1, d), w)


def _rope_pairs(x, cos, sin):
    lane = lax.broadcasted_iota(jnp.int32, cos.shape, 1)
    first_half = (lane % GLA_DK) < (GLA_DK // 2)
    out = []
    for p in range(GLA_PAIRS):
        xs = x[:, p * LANE:(p + 1) * LANE]
        swapped = jnp.where(first_half, pltpu.roll(xs, LANE - GLA_DK // 2, axis=1), pltpu.roll(xs, GLA_DK // 2, axis=1))
        out.append(xs * cos + swapped * sin)
    return jnp.concatenate(out, axis=1)


def _per_chunk_rows(x, offset):
    rows = x.shape[0]
    chunk = lax.broadcasted_iota(jnp.int32, (rows, 1), 0) // GLA_CHUNK
    out = x[offset:offset + 1]
    for c in range(1, rows // GLA_CHUNK):
        out = jnp.where(chunk >= c, x[c * GLA_CHUNK + offset:c * GLA_CHUNK + offset + 1], out)
    return out


def _gla_prepare(q, k, ab, wa, ba, cos, sin, reverse):
    rows = q.shape[0]
    z = jnp.dot(ab, wa, preferred_element_type=F32) + ba
    la = (jnp.minimum(z, 0.0) - jnp.log(1.0 + jnp.exp(-jnp.abs(z)))) * (1.0 / GLA_TAU)
    r = lax.broadcasted_iota(jnp.int32, (rows, rows), 0)
    c = lax.broadcasted_iota(jnp.int32, (rows, rows), 1)
    tri = ((r // GLA_CHUNK == c // GLA_CHUNK) & ((c >= r) if reverse else (c <= r))).astype(BF16)
    la_hi = la.astype(BF16)
    la_lo = (la - la_hi.astype(F32)).astype(BF16)
    b = jnp.dot(tri, la_hi, preferred_element_type=F32) + jnp.dot(tri, la_lo, preferred_element_type=F32)
    q = q.astype(F32)
    k = k.astype(F32)
    if cos is not None:
        q = _rope_pairs(q, cos, sin)
        k = _rope_pairs(k, cos, sin)
    return q, k, b


def _gla_block(q, k, b, v, s_scr, o_ref, inter_scr, reverse):
    rows = q.shape[0]
    n_chunks = rows // GLA_CHUNK
    half = GLA_CHUNK // 2
    rel = b - _per_chunk_rows(b, half if reverse else half - 1)
    last = _per_chunk_rows(b, 0 if reverse else GLA_CHUNK - 1)
    q_mid = (q * jnp.exp(rel))
    k_mid = (k * jnp.exp(-rel)).astype(BF16)
    q_state = (q * jnp.exp(b)).astype(BF16)
    k_last = (k * jnp.exp(last - b)).astype(BF16)
    state_decay = jnp.exp(last)
    row = lax.broadcasted_iota(jnp.int32, (2 * GLA_CHUNK, GLA_CHUNK), 0) % GLA_CHUNK
    col = lax.broadcasted_iota(jnp.int32, (2 * GLA_CHUNK, GLA_CHUNK), 1)
    visible = (row <= col) if reverse else (row >= col)
    first_head = lax.broadcasted_iota(jnp.int32, (rows, LANE), 1) < GLA_DK
    blk_r = lax.broadcasted_iota(jnp.int32, (LANE, 2 * GLA_DV), 0) < GLA_DK
    blk_c = lax.broadcasted_iota(jnp.int32, (LANE, 2 * GLA_DV), 1) < GLA_DV
    same_head = blk_r == blk_c
    nt = (((1,), (1,)), ((), ()))
    tn = (((0,), (0,)), ((), ()))
    q_first = [jnp.where(first_head, q_mid[:, p * LANE:(p + 1) * LANE], 0.0).astype(BF16) for p in range(GLA_PAIRS)]
    q_second = [jnp.where(first_head, 0.0, q_mid[:, p * LANE:(p + 1) * LANE]).astype(BF16) for p in range(GLA_PAIRS)]
    order = range(n_chunks - 1, -1, -1) if reverse else range(n_chunks)
    for c in order:
        rs = slice(c * GLA_CHUNK, (c + 1) * GLA_CHUNK)
        intra, inter = [], []
        for p in range(GLA_PAIRS):
            sl = slice(p * LANE, (p + 1) * LANE)
            q_stack = jnp.concatenate([q_first[p][rs], q_second[p][rs]], axis=0)
            att = lax.dot_general(q_stack, k_mid[rs, sl], nt, preferred_element_type=F32)
            att = jnp.where(visible, att, 0.0).astype(BF16)
            v_pair = v[rs, 2 * p * GLA_DV:(2 * p + 2) * GLA_DV]
            intra.append(jnp.dot(att[:GLA_CHUNK], v_pair[:, :GLA_DV], preferred_element_type=F32))
            intra.append(jnp.dot(att[GLA_CHUNK:], v_pair[:, GLA_DV:], preferred_element_type=F32))
            s_p = s_scr[p]
            inter.append(jnp.dot(q_state[rs, sl], s_p.astype(BF16), preferred_element_type=F32))
            update = lax.dot_general(k_last[rs, sl], v_pair, tn, preferred_element_type=F32)
            decay_row = state_decay[c * GLA_CHUNK:c * GLA_CHUNK + 1, sl]
            decay_col = jnp.transpose(jnp.broadcast_to(decay_row, (SUBLANE, LANE)))[:, 0:1]
            s_scr[p] = s_p * decay_col + jnp.where(same_head, update, 0.0)
        inter = jnp.concatenate(inter, axis=1)
        o_ref[0, rs, :] = jnp.concatenate(intra, axis=1) + inter
        inter_scr[rs, :] = inter
    return jnp.max(jnp.abs(rel))


def _gla_chunk_exact(q_scr, k_scr, b_scr, v_scr, o_ref, row0, reverse):
    length = GLA_CHUNK
    rows = pl.ds(row0, length)
    q = q_scr[rows, :]
    b = b_scr[rows, :]
    row = lax.broadcasted_iota(jnp.int32, (length, 1), 0)
    first_head = lax.broadcasted_iota(jnp.int32, (length, LANE), 1) < GLA_DK

    def body(s, carry):
        src = pl.ds(row0 + s, 1)
        w = q * k_scr[src, :] * jnp.exp(jnp.minimum(b - b_scr[src, :], 0.0))
        w = jnp.where((row <= s) if reverse else (row >= s), w, 0.0)
        v_row = v_scr[src, :]
        for p in range(GLA_PAIRS):
            wp = w[:, p * LANE:(p + 1) * LANE]
            for hh in range(2):
                a = jnp.sum(jnp.where(first_head == (hh == 0), wp, 0.0), axis=1, keepdims=True)
                lanes = slice((2 * p + hh) * GLA_DV, (2 * p + hh + 1) * GLA_DV)
                o_ref[0, rows, lanes] += a * v_row[:, lanes]
        return carry

    lax.fori_loop(0, length, body, 0)


def _gla_kernel(*refs, use_rope):
    qf, kf, vf, abf, qb, kb, vb, abb = refs[:8]
    pos = 8
    if use_rope:
        cosf, sinf, cosb, sinb = refs[pos:pos + 4]
        pos += 4
    (waf, baf, wab, bab, s0f, s0b, of_ref, ob_ref, sf_ref, sb_ref,
     sf_scr, sb_scr, inter_f, inter_b, q_scr, k_scr, b_scr, v_scr) = refs[pos:]
    step = pl.program_id(1)

    @pl.when(step == 0)
    def _():
        sf_scr[...] = s0f[0]
        sb_scr[...] = s0b[0]

    directions = ((False, (qf, kf, vf, abf), (cosf, sinf) if use_rope else None, waf, baf, sf_scr, of_ref, inter_f),
                  (True, (qb, kb, vb, abb), (cosb, sinb) if use_rope else None, wab, bab, sb_scr, ob_ref, inter_b))

    def prepared(reverse, refs4, rope_refs, w_ref, bias_ref):
        q_ref, k_ref, _, ab_ref = refs4
        cos = rope_refs[0][...] if use_rope else None
        sin = rope_refs[1][...] if use_rope else None
        return _gla_prepare(q_ref[0], k_ref[0], ab_ref[0], w_ref[...], bias_ref[...], cos, sin, reverse)

    margin = jnp.float32(0.0)
    for reverse, refs4, rope_refs, w_ref, bias_ref, s_scr, o_ref, inter_scr in directions:
        q, k, b = prepared(reverse, refs4, rope_refs, w_ref, bias_ref)
        m = _gla_block(q, k, b, refs4[2][0], s_scr, o_ref, inter_scr, reverse)
        margin = jnp.maximum(margin, m)

    @pl.when(margin > GLA_SAFE_EXPONENT)
    def _():
        for reverse, refs4, rope_refs, w_ref, bias_ref, s_scr, o_ref, inter_scr in directions:
            q, k, b = prepared(reverse, refs4, rope_refs, w_ref, bias_ref)
            q_scr[...] = q
            k_scr[...] = k
            b_scr[...] = b
            v_scr[...] = refs4[2][0].astype(F32)
            o_ref[0] = inter_scr[...]
            for c in range(GLA_CHUNKS_PER_STEP):
                _gla_chunk_exact(q_scr, k_scr, b_scr, v_scr, o_ref, c * GLA_CHUNK, reverse)

    @pl.when(step == pl.num_programs(1) - 1)
    def _():
        sf_ref[0] = sf_scr[...]
        sb_ref[0] = sb_scr[...]


def _gla(p3, off, waf, baf, wab, bab, s0f, s0b, rope):
    bsz, t, _ = p3.shape
    length = GLA_CHUNK * GLA_CHUNKS_PER_STEP
    n = t // length
    fwd = lambda blk: (lambda b, i: (b, i, blk))
    bwd = lambda blk: (lambda b, i: (b, n - 1 - i, blk))
    q_blk, k_blk = off["gq"] // GLA_KD, off["gk"] // GLA_KD
    v_blk, ab_blk = off["gv"] // GLA_VD, off["gab"] // LANE
    seq_specs = lambda m: [pl.BlockSpec((1, length, GLA_KD), m(q_blk)), pl.BlockSpec((1, length, GLA_KD), m(k_blk)),
                           pl.BlockSpec((1, length, GLA_VD), m(v_blk)), pl.BlockSpec((1, length, LANE), m(ab_blk))]
    in_specs = seq_specs(fwd) + seq_specs(bwd)
    args = [p3] * 8
    if rope is not None:
        in_specs += [pl.BlockSpec((length, LANE), lambda b, i: (i, 0))] * 2
        in_specs += [pl.BlockSpec((length, LANE), lambda b, i: (n - 1 - i, 0))] * 2
        args += [rope[0], rope[1], rope[0], rope[1]]
    const2 = lambda b, i: (0, 0)
    state_dims = (GLA_PAIRS, LANE, 2 * GLA_DV)
    state_spec = pl.BlockSpec((1,) + state_dims, lambda b, i: (b, 0, 0, 0))
    in_specs += [pl.BlockSpec((LANE, GLA_KD), const2), pl.BlockSpec((1, GLA_KD), const2),
                 pl.BlockSpec((LANE, GLA_KD), const2), pl.BlockSpec((1, GLA_KD), const2),
                 state_spec, state_spec]
    args += [waf, baf, wab, bab, s0f, s0b]
    state_shape = jax.ShapeDtypeStruct((bsz,) + state_dims, F32)
    out_shape = jax.ShapeDtypeStruct((bsz, t, GLA_VD), F32)
    blocks = 2 * (2 * (2 * _nbytes((length, GLA_KD), F32) + 2 * _nbytes((length, GLA_VD), F32) + 5 * _nbytes((length, LANE), F32))
                  + 4 * _nbytes(state_dims, F32) + 2 * _nbytes((LANE, GLA_KD), F32))
    blocks += 2 * _nbytes(state_dims, F32) + 2 * _nbytes((length, GLA_VD), F32)
    return pl.pallas_call(
        functools.partial(_gla_kernel, use_rope=rope is not None),
        grid=(bsz, n),
        in_specs=in_specs,
        out_specs=[pl.BlockSpec((1, length, GLA_VD), fwd(0)), pl.BlockSpec((1, length, GLA_VD), bwd(0)),
                   state_spec, state_spec],
        out_shape=[out_shape, out_shape, state_shape, state_shape],
        scratch_shapes=[pltpu.VMEM(state_dims, F32), pltpu.VMEM(state_dims, F32),
                        pltpu.VMEM((length, GLA_VD), F32), pltpu.VMEM((length, GLA_VD), F32),
                        pltpu.VMEM((length, GLA_KD), F32), pltpu.VMEM((length, GLA_KD), F32),
                        pltpu.VMEM((length, GLA_KD), F32), pltpu.VMEM((length, GLA_VD), F32)],
        compiler_params=_params(("arbitrary", "arbitrary"), blocks),
        name="gla",
    )(*args)


def _na_kernel(q_ref, k_ref, v_ref, kc_ref, vc_ref, blk_ref, o_ref, tab_scr, *, rows):
    nt = (((1,), (1,)), ((), ()))
    step = pl.program_id(2)

    @pl.when(step == 0)
    def _():
        masked = jnp.full((GRID_W, GRID_W), MASK_VALUE, F32)
        for variant, per_query in enumerate(_na_row_offsets(rows)):
            for rq, offsets in enumerate(per_query):
                tab_scr[variant, rq * GRID_W:(rq + 1) * GRID_W, :] = jnp.concatenate(
                    [masked if r is None else blk_ref[0, r] for r in offsets], axis=1)

    kc = kc_ref[0]
    vc = vc_ref[0]
    nq = NA_GROUP * GRID_W
    nk = NA_KEY_ROWS * GRID_W
    for g in range(NA_GROUPS_PER_STEP):
        r0 = (step * NA_GROUPS_PER_STEP + g) * NA_GROUP
        key_row0 = jnp.clip(r0 - NA_WIN_H // 2, 0, rows - NA_KEY_ROWS)
        k_off = pl.multiple_of(key_row0 * GRID_W, GRID_W)
        variant = jnp.where(r0 == 0, 0, jnp.where(r0 == rows - NA_GROUP, 2, 1))
        q = q_ref[0, g * nq:(g + 1) * nq, :]
        kr = k_ref[0, pl.ds(k_off, nk), :]
        vr = v_ref[0, pl.ds(k_off, nk), :]
        s_loc = lax.dot_general(q, kr, nt, preferred_element_type=F32) + tab_scr[variant]
        s_ctx = lax.dot_general(q, kc, nt, preferred_element_type=F32)
        m = jnp.maximum(jnp.max(s_loc, axis=-1, keepdims=True), jnp.max(s_ctx, axis=-1, keepdims=True))
        p_loc = jnp.exp(s_loc - m)
        p_ctx = jnp.exp(s_ctx - m)
        denom = jnp.sum(p_loc, axis=-1, keepdims=True) + jnp.sum(p_ctx, axis=-1, keepdims=True)
        o = (jnp.dot(p_loc.astype(BF16), vr, preferred_element_type=F32)
             + jnp.dot(p_ctx.astype(BF16), vc, preferred_element_type=F32))
        o_ref[0, g * nq:(g + 1) * nq, :] = (o / denom).astype(o_ref.dtype)


def _na(p3, pc3, off, bias_blocks, layer):
    bsz, t, _ = p3.shape
    ct = pc3.shape[1]
    rows = t // GRID_W
    tq = NA_GROUPS_PER_STEP * NA_GROUP * GRID_W
    nq, nk = NA_GROUP * GRID_W, NA_KEY_ROWS * GRID_W
    n_off = 2 * NA_WIN_H - 1
    q0, k0, v0 = off["nq"] // NA_DH, off["nk"] // NA_DH, off["nv"] // NA_DH
    blocks = (2 * (2 * _nbytes((tq, NA_DH), F32) + 2 * _nbytes((t, NA_DH), BF16) + 2 * _nbytes((ct, NA_DH), BF16)
                   + _nbytes((n_off, GRID_W, LANE), F32)) + _nbytes((3, nq, nk), F32)
              + 3 * NA_GROUPS_PER_STEP * _nbytes((nq, nk + ct), F32))
    return pl.pallas_call(
        functools.partial(_na_kernel, rows=rows),
        grid=(bsz, NA_HEADS, rows // (NA_GROUPS_PER_STEP * NA_GROUP)),
        in_specs=[pl.BlockSpec((1, tq, NA_DH), lambda b, h, r: (b, r, q0 + h)),
                  pl.BlockSpec((1, t, NA_DH), lambda b, h, r: (b, 0, k0 + h)),
                  pl.BlockSpec((1, t, NA_DH), lambda b, h, r: (b, 0, v0 + h)),
                  pl.BlockSpec((1, ct, NA_DH), lambda b, h, r: (b, 0, k0 + h)),
                  pl.BlockSpec((1, ct, NA_DH), lambda b, h, r: (b, 0, v0 + h)),
                  pl.BlockSpec((None, 1, n_off, GRID_W, GRID_W), lambda b, h, r: (layer, h, 0, 0, 0))],
        out_specs=pl.BlockSpec((1, tq, NA_DH), lambda b, h, r: (b, r, h)),
        out_shape=jax.ShapeDtypeStruct((bsz, t, NA_D), BF16),
        scratch_shapes=[pltpu.VMEM((3, nq, nk), F32)],
        compiler_params=_params(("arbitrary", "arbitrary", "arbitrary"), blocks),
        name="natten",
    )(p3, p3, p3, pc3, pc3, bias_blocks)


def _na_row_offsets(rows):
    table = []
    for r0 in (0, NA_GROUP, rows - NA_GROUP):
        key_row0 = int(np.clip(r0 - NA_WIN_H // 2, 0, rows - NA_KEY_ROWS))
        per_query = []
        for rq in range(NA_GROUP):
            r = r0 + rq
            r_start = int(np.clip(r - NA_WIN_H // 2, 0, rows - NA_WIN_H))
            per_query.append([key_row0 + ki - r + (NA_WIN_H - 1) if r_start <= key_row0 + ki < r_start + NA_WIN_H
                              else None for ki in range(NA_KEY_ROWS)])
        table.append(per_query)
    return table


def _na_bias_blocks(rpb):
    w = GRID_W
    col = np.arange(w)
    col_start = np.clip(col - NA_WIN_W // 2, 0, w - NA_WIN_W)
    valid_col = (col[None, :] >= col_start[:, None]) & (col[None, :] < col_start[:, None] + NA_WIN_W)
    col_off = col[None, :] - col[:, None] + (NA_WIN_W - 1)
    col_sel = np.zeros((2 * NA_WIN_W - 1, w, w), np.float32)
    cc, jj = np.nonzero(valid_col)
    col_sel[col_off[cc, jj], cc, jj] = 1.0
    blocks = jnp.einsum("lhrx,xcj->lhrcj", rpb.astype(F32), col_sel, precision=lax.Precision.HIGHEST)
    return jnp.where(valid_col, blocks, MASK_VALUE)


def _ctx_attn_kernel(q_ref, k_ref, v_ref, o_ref):
    s = lax.dot_general(q_ref[0], k_ref[0], (((1,), (1,)), ((), ())), preferred_element_type=F32)
    p = jnp.exp(s - jnp.max(s, axis=-1, keepdims=True))
    o = jnp.dot(p.astype(BF16), v_ref[0], preferred_element_type=F32)
    o_ref[0] = (o / jnp.sum(p, axis=-1, keepdims=True)).astype(o_ref.dtype)


def _ctx_attn(pc3, off):
    bsz, ct, _ = pc3.shape
    q0, k0, v0 = off["nq"] // NA_DH, off["nk"] // NA_DH, off["nv"] // NA_DH
    spec = lambda c0: pl.BlockSpec((1, ct, NA_DH), lambda b, h: (b, 0, c0 + h))
    return pl.pallas_call(
        _ctx_attn_kernel,
        grid=(bsz, NA_HEADS),
        in_specs=[spec(q0), spec(k0), spec(v0)],
        out_specs=spec(0),
        out_shape=jax.ShapeDtypeStruct((bsz, ct, NA_D), BF16),
        compiler_params=_params(("arbitrary", "arbitrary"), 8 * _nbytes((ct, NA_DH), F32)),
        name="ctx_attn",
    )(pc3, pc3, pc3)


def _branch_kernel(gate_ref, ain_ref, ab_ref, ac_ref, ainp_ref, acp_ref, ainn_ref, acn_ref,
                   of_ref, ob_ref, gr_ref, na_ref, cw_ref, gn_ref, wa_ref, wg_ref, wn_ref, y_ref, *, tiles_per_seq):
    tm = ain_ref.shape[0]
    d = y_ref.shape[1]
    i = pl.program_id(0)
    tile_in_seq = i % tiles_per_seq
    u = ac_ref[...].astype(F32) * ain_ref[...].astype(F32)
    u_before = jnp.where(tile_in_seq == 0, 0.0, (acp_ref[...].astype(F32) * ainp_ref[...].astype(F32))[HALO_ROWS - 1:HALO_ROWS])
    u_after = jnp.where(tile_in_seq == tiles_per_seq - 1, 0.0, (acn_ref[...].astype(F32) * ainn_ref[...].astype(F32))[0:1])
    row = lax.broadcasted_iota(jnp.int32, u.shape, 0)
    u_prev = jnp.where(row == 0, u_before, pltpu.roll(u, 1, axis=0))
    u_next = jnp.where(row == tm - 1, u_after, pltpu.roll(u, tm - 1, axis=0))
    cw = cw_ref[...]
    a = ab_ref[...].astype(F32) * (cw[0:1] * u_prev + cw[1:2] * u + cw[2:3] * u_next)
    o = of_ref[...] + ob_ref[...]
    gn = gn_ref[...]
    normed = []
    for h in range(GLA_HEADS):
        oh = o[:, h * GLA_DV:(h + 1) * GLA_DV]
        normed.append(oh * lax.rsqrt(jnp.mean(oh * oh, axis=-1, keepdims=True) + RMS_EPS) * gn)
    r = gr_ref[...].astype(F32)
    g = jnp.concatenate(normed, axis=1) * (r * jax.nn.sigmoid(r))
    y = jax.nn.sigmoid(gate_ref[:, 0:d].astype(F32)) * jnp.dot(a.astype(BF16), wa_ref[...], preferred_element_type=F32)
    y = y + jax.nn.sigmoid(gate_ref[:, d:2 * d].astype(F32)) * jnp.dot(g.astype(BF16), wg_ref[...], preferred_element_type=F32)
    y = y + jax.nn.sigmoid(gate_ref[:, 2 * d:3 * d].astype(F32)) * jnp.dot(na_ref[...], wn_ref[...],
                                                                             preferred_element_type=F32)
    y_ref[...] = y.astype(BF16)


def _branch(p, off, og_f, og_b, na, conv_w, g_norm, wa, wg, wn, layer, seq_len, tm):
    m_rows = p.shape[0]
    d = wa.shape[2]
    tiles_per_seq = seq_len // tm
    halo = tm // HALO_ROWS
    n_halo = m_rows // HALO_ROWS
    cur = lambda blk: (lambda i: (i, blk))
    prev = lambda blk: (lambda i: (jnp.maximum(i * halo - 1, 0), blk))
    nxt = lambda blk: (lambda i: (jnp.minimum((i + 1) * halo, n_halo - 1), blk))
    a_in, a_b, a_c = off["a_in"] // A_WIDTH, off["a_b"] // A_WIDTH, off["a_c"] // A_WIDTH
    const = lambda i: (0, 0)
    lay = lambda i: (layer, 0, 0)
    in_specs = [pl.BlockSpec((tm, N_BRANCH * d), cur(off["gate"] // (N_BRANCH * d))),
                pl.BlockSpec((tm, A_WIDTH), cur(a_in)), pl.BlockSpec((tm, A_WIDTH), cur(a_b)),
                pl.BlockSpec((tm, A_WIDTH), cur(a_c)),
                pl.BlockSpec((HALO_ROWS, A_WIDTH), prev(a_in)), pl.BlockSpec((HALO_ROWS, A_WIDTH), prev(a_c)),
                pl.BlockSpec((HALO_ROWS, A_WIDTH), nxt(a_in)), pl.BlockSpec((HALO_ROWS, A_WIDTH), nxt(a_c)),
                pl.BlockSpec((tm, GLA_VD), cur(0)), pl.BlockSpec((tm, GLA_VD), cur(0)),
                pl.BlockSpec((tm, GLA_VD), cur(off["gr"] // GLA_VD)),
                pl.BlockSpec((tm, NA_D), cur(0)),
                pl.BlockSpec((3, A_WIDTH), const), pl.BlockSpec((1, GLA_DV), const),
                pl.BlockSpec((None, A_WIDTH, d), lay), pl.BlockSpec((None, GLA_VD, d), lay),
                pl.BlockSpec((None, NA_D, d), lay)]
    blocks = 2 * (_nbytes((tm, N_BRANCH * d), BF16) + 3 * _nbytes((tm, A_WIDTH), BF16) + 4 * _nbytes((tm, GLA_VD), F32)
                  + _nbytes((A_WIDTH + GLA_VD + NA_D, d), BF16) + _nbytes((tm, d), BF16)) + 6 * _nbytes((tm, d), F32)
    return pl.pallas_call(
        functools.partial(_branch_kernel, tiles_per_seq=tiles_per_seq),
        grid=(m_rows // tm,),
        in_specs=in_specs,
        out_specs=pl.BlockSpec((tm, d), cur(0)),
        out_shape=jax.ShapeDtypeStruct((m_rows, d), BF16),
        compiler_params=_params(("arbitrary",), blocks),
        name="branch_merge",
    )(p, p, p, p, p, p, p, p, og_f, og_b, p, na, conv_w, g_norm.reshape(1, GLA_DV), wa, wg, wn)


def _proj_kernel(y_ref, w_ref, h_ref, gm_ref, o_ref):
    o_ref[...] = h_ref[...] + gm_ref[0] * jnp.dot(y_ref[...], w_ref[...], preferred_element_type=F32)


def _proj_residual(y, w, layer, h, gate, group_of_tile, tm, tn):
    m_rows, d = h.shape
    blocks = 2 * (_nbytes((tm, d), BF16) + _nbytes((d, tn), BF16) + 2 * _nbytes((tm, tn), F32)) + _nbytes((tm, tn), F32)
    return pl.pallas_call(
        _proj_kernel,
        grid=(m_rows // tm, d // tn),
        in_specs=[pl.BlockSpec((tm, d), lambda i, j: (i, 0)),
                  pl.BlockSpec((None, d, tn), lambda i, j: (layer, 0, _serpentine(i, j, d // tn))),
                  pl.BlockSpec((tm, tn), lambda i, j: (i, _serpentine(i, j, d // tn))),
                  pl.BlockSpec((1, 1, tn), lambda i, j: (group_of_tile(i), 0, _serpentine(i, j, d // tn)))],
        out_specs=pl.BlockSpec((tm, tn), lambda i, j: (i, _serpentine(i, j, d // tn))),
        out_shape=jax.ShapeDtypeStruct((m_rows, d), F32),
        compiler_params=_params(("parallel", "arbitrary"), blocks),
        name="out_proj",
    )(y, w, h, gate)


def _ffn_kernel(x_ref, mod_ref, g_ref, w1_ref, w3_ref, w2_ref, gfin_ref, o_ref, n_scr, acc_scr, *, final_norm):
    j = pl.program_id(1)

    @pl.when(j == 0)
    def _():
        m = mod_ref[0]
        n_scr[...] = _mod_norm(x_ref[...], g_ref[...], m[3:4], m[4:5]).astype(BF16)
        acc_scr[...] = jnp.zeros_like(acc_scr)

    n = n_scr[...]
    a = jnp.dot(n, w1_ref[...], preferred_element_type=F32)
    b = jnp.dot(n, w3_ref[...], preferred_element_type=F32)
    u = (a * jax.nn.sigmoid(a)) * b
    acc_scr[...] += jnp.dot(u.astype(BF16), w2_ref[...], preferred_element_type=F32)

    @pl.when(j == pl.num_programs(1) - 1)
    def _():
        out = x_ref[...] + mod_ref[0][5:6] * acc_scr[...]
        if final_norm:
            out = out * lax.rsqrt(jnp.mean(out * out, axis=-1, keepdims=True) + RMS_EPS) * gfin_ref[...]
        o_ref[...] = out


def _ffn(h, mod_l, g, w1, w3, w2, layer, g_final, group_of_tile, tm, tf, final_norm):
    m_rows, d = h.shape
    f = w1.shape[2]
    blocks = (4 * _nbytes((tm, d), F32) + 6 * _nbytes((d, tf), BF16) + _nbytes((tm, d), BF16) + _nbytes((tm, d), F32)
              + 4 * _nbytes((tm, tf), F32))
    return pl.pallas_call(
        functools.partial(_ffn_kernel, final_norm=final_norm),
        grid=(m_rows // tm, f // tf),
        in_specs=[pl.BlockSpec((tm, d), lambda i, j: (i, 0)),
                  pl.BlockSpec((1, 6, d), lambda i, j: (group_of_tile(i), 0, 0)),
                  pl.BlockSpec((1, d), lambda i, j: (0, 0)),
                  pl.BlockSpec((None, d, tf), lambda i, j: (layer, 0, _serpentine(i, j, f // tf))),
                  pl.BlockSpec((None, d, tf), lambda i, j: (layer, 0, _serpentine(i, j, f // tf))),
                  pl.BlockSpec((None, tf, d), lambda i, j: (layer, _serpentine(i, j, f // tf), 0)),
                  pl.BlockSpec((1, d), lambda i, j: (0, 0))],
        out_specs=pl.BlockSpec((tm, d), lambda i, j: (i, 0)),
        out_shape=jax.ShapeDtypeStruct((m_rows, d), F32),
        scratch_shapes=[pltpu.VMEM((tm, d), BF16), pltpu.VMEM((tm, d), F32)],
        compiler_params=_params(("parallel", "arbitrary"), blocks),
        name="ffn",
    )(h, mod_l, g.reshape(1, d), w1, w3, w2, g_final.reshape(1, d))


def _largest_tile(total, cap, align):
    t = min(total, cap)
    while total % t or t % align:
        t -= align
    return t


def _rope_tables(seq_len):
    pos = jnp.arange(seq_len, dtype=jnp.int32)
    rowf = (pos // GRID_W).astype(F32)
    colf = (pos % GRID_W).astype(F32)
    per_axis = GLA_DK // 2
    inv = ROPE_BASE ** (-jnp.arange(0, per_axis, 2, dtype=F32) / per_axis)
    ang = jnp.concatenate([rowf[:, None] * inv, colf[:, None] * inv], axis=-1)
    cos, sin = jnp.cos(ang), jnp.sin(ang)
    reps = LANE // GLA_DK
    return jnp.tile(cos, (1, 2 * reps)), jnp.tile(jnp.concatenate([-sin, sin], axis=-1), (1, reps))


def _pack_kernel(w_ref, o_ref, *, d_model, n_pad):
    sizes = (A_WIDTH, A_WIDTH, A_WIDTH, GLA_KD, GLA_KD, GLA_VD, GLA_VD, GLA_LOWRANK, GLA_LOWRANK, NA_D, NA_D, NA_D)
    start = dict(zip(("a_in", "a_b", "a_c", "gq", "gk", "gv", "gr", "gaf", "gab", "nq", "nk", "nv"),
                     np.concatenate([[0], np.cumsum(sizes)[:-1]]).tolist()))
    gate0 = sum(sizes)
    cols = w_ref.shape[2]
    pieces = [w_ref[0, gate0:gate0 + N_BRANCH * d_model, :],
              w_ref[0, 0:start["gq"], :],
              w_ref[0, start["gq"]:start["gk"], :] * (GLA_DK ** -0.5),
              w_ref[0, start["gk"]:start["nq"], :],
              jnp.zeros((LANE - 2 * GLA_LOWRANK, cols), F32),
              w_ref[0, start["nq"]:start["nk"], :] * (NA_DH ** -0.5),
              w_ref[0, start["nk"]:gate0, :]]
    used = sum(p.shape[0] for p in pieces)
    pieces.append(jnp.zeros((n_pad - used, cols), F32))
    o_ref[0] = jnp.concatenate(pieces, axis=0).astype(BF16)


def _pack_w_in(w_in, n_pad):
    depth, d, n_in = w_in.shape
    w_t = jnp.swapaxes(w_in, 1, 2)
    tc = LANE
    blocks = 2 * (_nbytes((n_in, tc), F32) + _nbytes((n_pad, tc), BF16)) + 2 * _nbytes((n_pad, tc), F32)
    return pl.pallas_call(
        functools.partial(_pack_kernel, d_model=d, n_pad=n_pad),
        grid=(depth, d // tc),
        in_specs=[pl.BlockSpec((1, n_in, tc), lambda l, i: (l, 0, i))],
        out_specs=pl.BlockSpec((1, n_pad, tc), lambda l, i: (l, 0, i)),
        out_shape=jax.ShapeDtypeStruct((depth, n_pad, d), BF16),
        compiler_params=_params(("arbitrary", "arbitrary"), blocks),
        name="pack_w_in",
    )(w_t)


def _pad_lowrank(wa, col0):
    return jnp.zeros((LANE, GLA_KD), BF16).at[col0:col0 + GLA_LOWRANK].set(wa.astype(BF16))


def kernel(x, c, ctx, c_ctx, w_ada, b_ada, g_mix, g_ffn, w_in, conv_w, gla_wa_f, gla_ba_f, gla_wa_b, gla_ba_b,
           gla_g_norm, na_rpb, w_a_out, w_g_out, w_n_out, w_o, w_ffn1, w_ffn3, w_ffn2, g_final):
    bsz, seq, d = x.shape
    ct = ctx.shape[1]
    depth = w_ada.shape[0]
    f = w_ffn1.shape[2]
    gla_block = GLA_CHUNK * GLA_CHUNKS_PER_STEP
    assert bsz + 1 <= MOD_ROWS and seq % GRID_W == 0 and seq % gla_block == 0 and ct % gla_block == 0
    assert (seq // GRID_W) % (NA_GROUP * NA_GROUPS_PER_STEP) == 0 and seq // GRID_W > NA_KEY_ROWS
    off, n_used = _layout(d)
    tn_in = 10 * LANE
    n_pad = -(-n_used // tn_in) * tn_in

    tm_big = _largest_tile(seq, 1024, SUBLANE)
    tm_ffn = _largest_tile(seq, 512, SUBLANE)
    tm_branch = _largest_tile(seq, 256, SUBLANE)
    tn_proj = _largest_tile(d, 1024, LANE)
    tf = _largest_tile(f, 512, LANE)
    lat_group = lambda tm: (lambda i: i // (seq // tm))
    ctx_group = lambda i: bsz

    cvec = jnp.zeros((MOD_ROWS, d), F32).at[:bsz].set(c).at[bsz].set(c_ctx)
    mod = _adaln(cvec, w_ada, b_ada).reshape(depth, MOD_ROWS, 6, d)
    rope = _rope_tables(seq)
    zero_state = jnp.zeros((bsz, GLA_PAIRS, LANE, 2 * GLA_DV), F32)

    w_in_p = _pack_w_in(w_in, n_pad)
    wa_o, wg_o, wn_o = w_a_out.astype(BF16), w_g_out.astype(BF16), w_n_out.astype(BF16)
    wo, w1, w3, w2 = w_o.astype(BF16), w_ffn1.astype(BF16), w_ffn3.astype(BF16), w_ffn2.astype(BF16)
    bias_blocks = _na_bias_blocks(na_rpb)

    h = x.reshape(bsz * seq, d)
    hc = ctx.reshape(bsz * ct, d)
    for l in range(depth):
        need_ctx = l < depth - 1
        last = l == depth - 1
        mod_l = mod[l]
        gm = mod_l[:, 2, :].reshape(MOD_ROWS, 1, d)
        waf, wab = _pad_lowrank(gla_wa_f[l], 0), _pad_lowrank(gla_wa_b[l], GLA_LOWRANK)
        baf, bab = gla_ba_f[l].reshape(1, GLA_KD), gla_ba_b[l].reshape(1, GLA_KD)

        p = _inproj(h, mod_l, g_mix[l], w_in_p, l, lat_group(tm_big), tm_big, tn_in)
        pc = _inproj(hc, mod_l, g_mix[l], w_in_p, l, ctx_group, ct, tn_in)
        p3 = p.reshape(bsz, seq, n_pad)
        pc3 = pc.reshape(bsz, ct, n_pad)

        ogc_f, ogc_b, s_f, s_b = _gla(pc3, off, waf, baf, wab, bab, zero_state, zero_state, None)
        og_f, og_b, _, _ = _gla(p3, off, waf, baf, wab, bab, s_f, s_b, rope)
        na = _na(p3, pc3, off, bias_blocks, l)

        y = _branch(p, off, og_f.reshape(bsz * seq, GLA_VD), og_b.reshape(bsz * seq, GLA_VD),
                    na.reshape(bsz * seq, NA_D), conv_w[l], gla_g_norm[l], wa_o, wg_o, wn_o, l, seq, tm_branch)
        h = _proj_residual(y, wo, l, h, gm, lat_group(tm_big), tm_big, tn_proj)
        h = _ffn(h, mod_l, g_ffn[l], w1, w3, w2, l, g_final, lat_group(tm_ffn), tm_ffn, tf, last)
        if need_ctx:
            na_c = _ctx_attn(pc3, off)
            yc = _branch(pc, off, ogc_f.reshape(bsz * ct, GLA_VD), ogc_b.reshape(bsz * ct, GLA_VD),
                         na_c.reshape(bsz * ct, NA_D), conv_w[l], gla_g_norm[l], wa_o, wg_o, wn_o, l, ct, ct)
            hc = _proj_residual(yc, wo, l, hc, gm, ctx_group, ct, tn_proj)
            hc = _ffn(hc, mod_l, g_ffn[l], w1, w3, w2, l, g_final, ctx_group, ct, tf, False)
    return h.reshape(bsz, seq, d)
```

```python
import functools

import numpy as np
import jax
import jax.numpy as jnp
from jax import lax
from jax.experimental import pallas as pl
from jax.experimental.pallas import tpu as pltpu

GRID_W = 64
RMS_EPS = 1e-6
ROPE_BASE = 10000.0
N_BRANCH = 3
A_WIDTH = 512
GLA_HEADS = 6
GLA_DK = 64
GLA_DV = 128
GLA_KD = GLA_HEADS * GLA_DK
GLA_VD = GLA_HEADS * GLA_DV
GLA_LOWRANK = 16
GLA_TAU = 16.0
GLA_CHUNK = 64
GLA_CHUNKS_PER_STEP = 4
GLA_SAFE_EXPONENT = 80.0
GLA_PAIRS = GLA_HEADS // 2
NA_HEADS = 6
NA_DH = 128
NA_D = NA_HEADS * NA_DH
NA_WIN_H = 8
NA_WIN_W = 16
NA_GROUP = 4
NA_KEY_ROWS = NA_GROUP + NA_WIN_H - 1
NA_GROUPS_PER_STEP = 4
MASK_VALUE = -1e30

LANE = 128
SUBLANE = 8
HALO_ROWS = 16
VMEM_PHYSICAL = 64 * 1024 * 1024
VMEM_INTERNAL = 12 * 1024 * 1024
MOD_ROWS = 8

BF16 = jnp.bfloat16
F32 = jnp.float32


def _params(semantics, block_bytes):
    limit = min(int(block_bytes) + VMEM_INTERNAL, VMEM_PHYSICAL - 4 * 1024 * 1024)
    return pltpu.CompilerParams(dimension_semantics=semantics, vmem_limit_bytes=limit)


def _nbytes(shape, dtype):
    return int(np.prod(shape)) * jnp.dtype(dtype).itemsize


def _sigmoid(x):
    return 0.5 * jnp.tanh(0.5 * x) + 0.5


def _serpentine(i, j, n):
    return jnp.where(i % 2 == 0, j, n - 1 - j)


def _layout(d_model):
    off = {}
    pos = 0
    for name, width, block in (("gate", N_BRANCH * d_model, N_BRANCH * d_model),
                               ("a_in", A_WIDTH, A_WIDTH), ("a_b", A_WIDTH, A_WIDTH), ("a_c", A_WIDTH, A_WIDTH),
                               ("gq", GLA_KD, GLA_KD), ("gk", GLA_KD, GLA_KD), ("gv", GLA_VD, GLA_VD),
                               ("gr", GLA_VD, GLA_VD), ("gab", LANE, LANE),
                               ("nq", NA_D, NA_DH), ("nk", NA_D, NA_DH), ("nv", NA_D, NA_DH)):
        assert pos % block == 0, (name, pos, block)
        off[name] = pos
        pos += width
    return off, pos


def _adaln_kernel(c_ref, w_ref, b_ref, o_ref):
    cv = c_ref[...]
    s = cv * jax.nn.sigmoid(cv)
    o_ref[0] = jnp.dot(s, w_ref[0], preferred_element_type=F32) + b_ref[0]


def _adaln(cvec, w_ada, b_ada):
    depth, d, n = w_ada.shape
    tn = 1024 if n % 1024 == 0 else n
    blocks = 2 * (_nbytes((d, tn), F32) + 2 * _nbytes((MOD_ROWS, tn), F32)) + _nbytes((MOD_ROWS, d), F32)
    return pl.pallas_call(
        _adaln_kernel,
        grid=(depth, n // tn),
        in_specs=[pl.BlockSpec((MOD_ROWS, d), lambda l, j: (0, 0)),
                  pl.BlockSpec((1, d, tn), lambda l, j: (l, 0, j)),
                  pl.BlockSpec((1, 1, tn), lambda l, j: (l, 0, j))],
        out_specs=pl.BlockSpec((1, MOD_ROWS, tn), lambda l, j: (l, 0, j)),
        out_shape=jax.ShapeDtypeStruct((depth, MOD_ROWS, n), F32),
        compiler_params=_params(("arbitrary", "arbitrary"), blocks),
        name="adaln",
    )(cvec, w_ada, b_ada.reshape(depth, 1, n))


def _mod_norm(x, g, shift, scale):
    y = x * lax.rsqrt(jnp.mean(x * x, axis=-1, keepdims=True) + RMS_EPS)
    return (y * g) * (1.0 + scale) + shift


def _inproj_kernel(x_ref, mod_ref, g_ref, w_ref, o_ref, n_scr):
    def project(n):
        o_ref[...] = lax.dot_general(n, w_ref[...], (((1,), (1,)), ((), ())),
                                     preferred_element_type=F32).astype(o_ref.dtype)

    @pl.when(pl.program_id(1) == 0)
    def _():
        m = mod_ref[0]
        n = _mod_norm(x_ref[...], g_ref[...], m[0:1], m[1:2]).astype(BF16)
        n_scr[...] = n
        project(n)

    @pl.when(pl.program_id(1) != 0)
    def _():
        project(n_scr[...])


def _inproj(h, mod_l, g, w, layer, group_of_tile, tm, tn):
    m_rows, d = h.shape
    n = w.shape[1]
    blocks = (2 * (_nbytes((tm, d), F32) + _nbytes((d, tn), BF16) + _nbytes((tm, tn), BF16))
              + _nbytes((tm, d), BF16) + 2 * _nbytes((8, d), F32) + _nbytes((tm, tn), F32))
    return pl.pallas_call(
        _inproj_kernel,
        grid=(m_rows // tm, n // tn),
        in_specs=[pl.BlockSpec((tm, d), lambda i, j: (i, 0)),
                  pl.BlockSpec((1, 6, d), lambda i, j: (group_of_tile(i), 0, 0)),
                  pl.BlockSpec((1, d), lambda i, j: (0, 0)),
                  pl.BlockSpec((None, tn, d), lambda i, j: (layer, _serpentine(i, j, n // tn), 0))],
        out_specs=pl.BlockSpec((tm, tn), lambda i, j: (i, _serpentine(i, j, n // tn))),
        out_shape=jax.ShapeDtypeStruct((m_rows, n), BF16),
        scratch_shapes=[pltpu.VMEM((tm, d), BF16)],
        compiler_params=_params(("parallel", "arbitrary"), blocks),
        name="inproj",
    )(h, mod_l, g.reshape(1, d), w)


def _rope_pairs(x, cos, sin):
    lane = lax.broadcasted_iota(jnp.int32, cos.shape, 1)
    first_half = (lane % GLA_DK) < (GLA_DK // 2)
    out = []
    for p in range(GLA_PAIRS):
        xs = x[:, p * LANE:(p + 1) * LANE]
        swapped = jnp.where(first_half, pltpu.roll(xs, LANE - GLA_DK // 2, axis=1), pltpu.roll(xs, GLA_DK // 2, axis=1))
        out.append(xs * cos + swapped * sin)
    return jnp.concatenate(out, axis=1)


def _minus_chunk_row(x, offset, sign=1.0):
    parts = []
    for c in range(x.shape[0] // GLA_CHUNK):
        chunk = x[c * GLA_CHUNK:(c + 1) * GLA_CHUNK]
        ref_row = x[c * GLA_CHUNK + offset:c * GLA_CHUNK + offset + 1]
        parts.append(chunk - ref_row if sign > 0 else ref_row - chunk)
    return jnp.concatenate(parts, axis=0)


def _gla_prepare(q, k, ab, wa, ba, cos, sin, reverse):
    rows = q.shape[0]
    z = jnp.dot(ab, wa, preferred_element_type=F32) + ba
    la = (jnp.minimum(z, 0.0) - jnp.log(1.0 + jnp.exp(-jnp.abs(z)))) * (1.0 / GLA_TAU)
    r = lax.broadcasted_iota(jnp.int32, (rows, rows), 0)
    c = lax.broadcasted_iota(jnp.int32, (rows, rows), 1)
    tri = ((r // GLA_CHUNK == c // GLA_CHUNK) & ((c >= r) if reverse else (c <= r))).astype(BF16)
    la_hi = la.astype(BF16)
    la_lo = (la - la_hi.astype(F32)).astype(BF16)
    b = jnp.dot(tri, la_hi, preferred_element_type=F32) + jnp.dot(tri, la_lo, preferred_element_type=F32)
    q = q.astype(F32)
    k = k.astype(F32)
    if cos is not None:
        q = _rope_pairs(q, cos, sin)
        k = _rope_pairs(k, cos, sin)
    return q, k, b


def _gla_block(q, k, b, v, s_scr, o_ref, inter_scr, reverse):
    rows = q.shape[0]
    n_chunks = rows // GLA_CHUNK
    half = GLA_CHUNK // 2
    last_row = 0 if reverse else GLA_CHUNK - 1
    rel = _minus_chunk_row(b, half if reverse else half - 1)
    q_mid = (q * jnp.exp(rel))
    k_mid = (k * jnp.exp(-rel)).astype(BF16)
    q_state = (q * jnp.exp(b)).astype(BF16)
    k_last = (k * jnp.exp(_minus_chunk_row(b, last_row, sign=-1.0))).astype(BF16)
    row = lax.broadcasted_iota(jnp.int32, (2 * GLA_CHUNK, GLA_CHUNK), 0) % GLA_CHUNK
    col = lax.broadcasted_iota(jnp.int32, (2 * GLA_CHUNK, GLA_CHUNK), 1)
    visible = (row <= col) if reverse else (row >= col)
    first_head = lax.broadcasted_iota(jnp.int32, (rows, LANE), 1) < GLA_DK
    blk_r = lax.broadcasted_iota(jnp.int32, (LANE, 2 * GLA_DV), 0) < GLA_DK
    blk_c = lax.broadcasted_iota(jnp.int32, (LANE, 2 * GLA_DV), 1) < GLA_DV
    same_head = blk_r == blk_c
    nt = (((1,), (1,)), ((), ()))
    tn = (((0,), (0,)), ((), ()))
    q_first = [jnp.where(first_head, q_mid[:, p * LANE:(p + 1) * LANE], 0.0).astype(BF16) for p in range(GLA_PAIRS)]
    q_second = [jnp.where(first_head, 0.0, q_mid[:, p * LANE:(p + 1) * LANE]).astype(BF16) for p in range(GLA_PAIRS)]
    order = range(n_chunks - 1, -1, -1) if reverse else range(n_chunks)
    for c in order:
        rs = slice(c * GLA_CHUNK, (c + 1) * GLA_CHUNK)
        intra, inter = [], []
        for p in range(GLA_PAIRS):
            sl = slice(p * LANE, (p + 1) * LANE)
            q_stack = jnp.concatenate([q_first[p][rs], q_second[p][rs]], axis=0)
            att = lax.dot_general(q_stack, k_mid[rs, sl], nt, preferred_element_type=F32)
            att = jnp.where(visible, att, 0.0).astype(BF16)
            v_pair = v[rs, 2 * p * GLA_DV:(2 * p + 2) * GLA_DV]
            intra.append(jnp.dot(att[:GLA_CHUNK], v_pair[:, :GLA_DV], preferred_element_type=F32))
            intra.append(jnp.dot(att[GLA_CHUNK:], v_pair[:, GLA_DV:], preferred_element_type=F32))
            s_p = s_scr[p]
            inter.append(jnp.dot(q_state[rs, sl], s_p.astype(BF16), preferred_element_type=F32))
            update = lax.dot_general(k_last[rs, sl], v_pair, tn, preferred_element_type=F32)
            decay_row = jnp.exp(b[c * GLA_CHUNK + last_row:c * GLA_CHUNK + last_row + 1, sl])
            decay_col = jnp.transpose(jnp.broadcast_to(decay_row, (SUBLANE, LANE)))[:, 0:1]
            s_scr[p] = s_p * decay_col + jnp.where(same_head, update, 0.0)
        inter = jnp.concatenate(inter, axis=1)
        o_ref[0, rs, :] = jnp.concatenate(intra, axis=1) + inter
        inter_scr[rs, :] = inter
    return jnp.max(jnp.abs(rel))


def _gla_chunk_exact(q_scr, k_scr, b_scr, v_scr, o_ref, row0, reverse):
    length = GLA_CHUNK
    rows = pl.ds(row0, length)
    q = q_scr[rows, :]
    b = b_scr[rows, :]
    row = lax.broadcasted_iota(jnp.int32, (length, 1), 0)
    first_head = lax.broadcasted_iota(jnp.int32, (length, LANE), 1) < GLA_DK

    def body(s, carry):
        src = pl.ds(row0 + s, 1)
        w = q * k_scr[src, :] * jnp.exp(jnp.minimum(b - b_scr[src, :], 0.0))
        w = jnp.where((row <= s) if reverse else (row >= s), w, 0.0)
        v_row = v_scr[src, :]
        for p in range(GLA_PAIRS):
            wp = w[:, p * LANE:(p + 1) * LANE]
            for hh in range(2):
                a = jnp.sum(jnp.where(first_head == (hh == 0), wp, 0.0), axis=1, keepdims=True)
                lanes = slice((2 * p + hh) * GLA_DV, (2 * p + hh + 1) * GLA_DV)
                o_ref[0, rows, lanes] += a * v_row[:, lanes]
        return carry

    lax.fori_loop(0, length, body, 0)


def _gla_kernel(*refs, use_rope):
    qf, kf, vf, abf, qb, kb, vb, abb = refs[:8]
    pos = 8
    if use_rope:
        cosf, sinf, cosb, sinb = refs[pos:pos + 4]
        pos += 4
    (waf, baf, wab, bab, s0f, s0b, of_ref, ob_ref, sf_ref, sb_ref,
     sf_scr, sb_scr, inter_f, inter_b, q_scr, k_scr, b_scr, v_scr) = refs[pos:]
    step = pl.program_id(1)

    @pl.when(step == 0)
    def _():
        sf_scr[...] = s0f[0]
        sb_scr[...] = s0b[0]

    directions = ((False, (qf, kf, vf, abf), (cosf, sinf) if use_rope else None, waf, baf, sf_scr, of_ref, inter_f),
                  (True, (qb, kb, vb, abb), (cosb, sinb) if use_rope else None, wab, bab, sb_scr, ob_ref, inter_b))

    def prepared(reverse, refs4, rope_refs, w_ref, bias_ref):
        q_ref, k_ref, _, ab_ref = refs4
        cos = rope_refs[0][...] if use_rope else None
        sin = rope_refs[1][...] if use_rope else None
        return _gla_prepare(q_ref[0], k_ref[0], ab_ref[0], w_ref[...], bias_ref[...], cos, sin, reverse)

    margin = jnp.float32(0.0)
    for reverse, refs4, rope_refs, w_ref, bias_ref, s_scr, o_ref, inter_scr in directions:
        q, k, b = prepared(reverse, refs4, rope_refs, w_ref, bias_ref)
        m = _gla_block(q, k, b, refs4[2][0], s_scr, o_ref, inter_scr, reverse)
        margin = jnp.maximum(margin, m)

    @pl.when(margin > GLA_SAFE_EXPONENT)
    def _():
        for reverse, refs4, rope_refs, w_ref, bias_ref, s_scr, o_ref, inter_scr in directions:
            q, k, b = prepared(reverse, refs4, rope_refs, w_ref, bias_ref)
            q_scr[...] = q
            k_scr[...] = k
            b_scr[...] = b
            v_scr[...] = refs4[2][0].astype(F32)
            o_ref[0] = inter_scr[...]
            for c in range(GLA_CHUNKS_PER_STEP):
                _gla_chunk_exact(q_scr, k_scr, b_scr, v_scr, o_ref, c * GLA_CHUNK, reverse)

    @pl.when(step == pl.num_programs(1) - 1)
    def _():
        sf_ref[0] = sf_scr[...]
        sb_ref[0] = sb_scr[...]


def _gla(p3, off, waf, baf, wab, bab, s0f, s0b, rope):
    bsz, t, _ = p3.shape
    length = GLA_CHUNK * GLA_CHUNKS_PER_STEP
    n = t // length
    fwd = lambda blk: (lambda b, i: (b, i, blk))
    bwd = lambda blk: (lambda b, i: (b, n - 1 - i, blk))
    q_blk, k_blk = off["gq"] // GLA_KD, off["gk"] // GLA_KD
    v_blk, ab_blk = off["gv"] // GLA_VD, off["gab"] // LANE
    seq_specs = lambda m: [pl.BlockSpec((1, length, GLA_KD), m(q_blk)), pl.BlockSpec((1, length, GLA_KD), m(k_blk)),
                           pl.BlockSpec((1, length, GLA_VD), m(v_blk)), pl.BlockSpec((1, length, LANE), m(ab_blk))]
    in_specs = seq_specs(fwd) + seq_specs(bwd)
    args = [p3] * 8
    if rope is not None:
        in_specs += [pl.BlockSpec((length, LANE), lambda b, i: (i, 0))] * 2
        in_specs += [pl.BlockSpec((length, LANE), lambda b, i: (n - 1 - i, 0))] * 2
        args += [rope[0], rope[1], rope[0], rope[1]]
    const2 = lambda b, i: (0, 0)
    state_dims = (GLA_PAIRS, LANE, 2 * GLA_DV)
    state_spec = pl.BlockSpec((1,) + state_dims, lambda b, i: (b, 0, 0, 0))
    in_specs += [pl.BlockSpec((LANE, GLA_KD), const2), pl.BlockSpec((1, GLA_KD), const2),
                 pl.BlockSpec((LANE, GLA_KD), const2), pl.BlockSpec((1, GLA_KD), const2),
                 state_spec, state_spec]
    args += [waf, baf, wab, bab, s0f, s0b]
    state_shape = jax.ShapeDtypeStruct((bsz,) + state_dims, F32)
    out_shape = jax.ShapeDtypeStruct((bsz, t, GLA_VD), F32)
    blocks = 2 * (2 * (2 * _nbytes((length, GLA_KD), F32) + 2 * _nbytes((length, GLA_VD), F32) + 5 * _nbytes((length, LANE), F32))
                  + 4 * _nbytes(state_dims, F32) + 2 * _nbytes((LANE, GLA_KD), F32))
    blocks += 2 * _nbytes(state_dims, F32) + 2 * _nbytes((length, GLA_VD), F32)
    return pl.pallas_call(
        functools.partial(_gla_kernel, use_rope=rope is not None),
        grid=(bsz, n),
        in_specs=in_specs,
        out_specs=[pl.BlockSpec((1, length, GLA_VD), fwd(0)), pl.BlockSpec((1, length, GLA_VD), bwd(0)),
                   state_spec, state_spec],
        out_shape=[out_shape, out_shape, state_shape, state_shape],
        scratch_shapes=[pltpu.VMEM(state_dims, F32), pltpu.VMEM(state_dims, F32),
                        pltpu.VMEM((length, GLA_VD), F32), pltpu.VMEM((length, GLA_VD), F32),
                        pltpu.VMEM((length, GLA_KD), F32), pltpu.VMEM((length, GLA_KD), F32),
                        pltpu.VMEM((length, GLA_KD), F32), pltpu.VMEM((length, GLA_VD), F32)],
        compiler_params=_params(("arbitrary", "arbitrary"), blocks),
        name="gla",
    )(*args)


def _na_kernel(q_ref, k_ref, v_ref, kc_ref, vc_ref, blk_ref, o_ref, tab_scr, *, rows):
    nt = (((1,), (1,)), ((), ()))
    step = pl.program_id(2)

    @pl.when(step == 0)
    def _():
        masked = jnp.full((GRID_W, GRID_W), MASK_VALUE, F32)
        for variant, per_query in enumerate(_na_row_offsets(rows)):
            for rq, offsets in enumerate(per_query):
                tab_scr[variant, rq * GRID_W:(rq + 1) * GRID_W, :] = jnp.concatenate(
                    [masked if r is None else blk_ref[0, r] for r in offsets], axis=1)

    kc = kc_ref[0]
    vc = vc_ref[0]
    nq = NA_GROUP * GRID_W
    nk = NA_KEY_ROWS * GRID_W
    for g in range(NA_GROUPS_PER_STEP):
        r0 = (step * NA_GROUPS_PER_STEP + g) * NA_GROUP
        key_row0 = jnp.clip(r0 - NA_WIN_H // 2, 0, rows - NA_KEY_ROWS)
        k_off = pl.multiple_of(key_row0 * GRID_W, GRID_W)
        variant = jnp.where(r0 == 0, 0, jnp.where(r0 == rows - NA_GROUP, 2, 1))
        q = q_ref[0, g * nq:(g + 1) * nq, :]
        kr = k_ref[0, pl.ds(k_off, nk), :]
        vr = v_ref[0, pl.ds(k_off, nk), :]
        s_loc = lax.dot_general(q, kr, nt, preferred_element_type=F32) + tab_scr[variant]
        s_ctx = lax.dot_general(q, kc, nt, preferred_element_type=F32)
        m = jnp.maximum(jnp.max(s_loc, axis=-1, keepdims=True), jnp.max(s_ctx, axis=-1, keepdims=True))
        p_loc = jnp.exp(s_loc - m)
        p_ctx = jnp.exp(s_ctx - m)
        denom = jnp.sum(p_loc, axis=-1, keepdims=True) + jnp.sum(p_ctx, axis=-1, keepdims=True)
        o = (jnp.dot(p_loc.astype(BF16), vr, preferred_element_type=F32)
             + jnp.dot(p_ctx.astype(BF16), vc, preferred_element_type=F32))
        o_ref[0, g * nq:(g + 1) * nq, :] = (o / denom).astype(o_ref.dtype)


def _na(p3, pc3, off, bias_blocks, layer):
    bsz, t, _ = p3.shape
    ct = pc3.shape[1]
    rows = t // GRID_W
    tq = NA_GROUPS_PER_STEP * NA_GROUP * GRID_W
    nq, nk = NA_GROUP * GRID_W, NA_KEY_ROWS * GRID_W
    n_off = 2 * NA_WIN_H - 1
    q0, k0, v0 = off["nq"] // NA_DH, off["nk"] // NA_DH, off["nv"] // NA_DH
    blocks = (2 * (2 * _nbytes((tq, NA_DH), F32) + 2 * _nbytes((t, NA_DH), BF16) + 2 * _nbytes((ct, NA_DH), BF16)
                   + _nbytes((n_off, GRID_W, LANE), F32)) + _nbytes((3, nq, nk), F32)
              + 3 * NA_GROUPS_PER_STEP * _nbytes((nq, nk + ct), F32))
    return pl.pallas_call(
        functools.partial(_na_kernel, rows=rows),
        grid=(bsz, NA_HEADS, rows // (NA_GROUPS_PER_STEP * NA_GROUP)),
        in_specs=[pl.BlockSpec((1, tq, NA_DH), lambda b, h, r: (b, r, q0 + h)),
                  pl.BlockSpec((1, t, NA_DH), lambda b, h, r: (b, 0, k0 + h)),
                  pl.BlockSpec((1, t, NA_DH), lambda b, h, r: (b, 0, v0 + h)),
                  pl.BlockSpec((1, ct, NA_DH), lambda b, h, r: (b, 0, k0 + h)),
                  pl.BlockSpec((1, ct, NA_DH), lambda b, h, r: (b, 0, v0 + h)),
                  pl.BlockSpec((None, 1, n_off, GRID_W, GRID_W), lambda b, h, r: (layer, h, 0, 0, 0))],
        out_specs=pl.BlockSpec((1, tq, NA_DH), lambda b, h, r: (b, r, h)),
        out_shape=jax.ShapeDtypeStruct((bsz, t, NA_D), BF16),
        scratch_shapes=[pltpu.VMEM((3, nq, nk), F32)],
        compiler_params=_params(("arbitrary", "arbitrary", "arbitrary"), blocks),
        name="natten",
    )(p3, p3, p3, pc3, pc3, bias_blocks)


def _na_row_offsets(rows):
    table = []
    for r0 in (0, NA_GROUP, rows - NA_GROUP):
        key_row0 = int(np.clip(r0 - NA_WIN_H // 2, 0, rows - NA_KEY_ROWS))
        per_query = []
        for rq in range(NA_GROUP):
            r = r0 + rq
            r_start = int(np.clip(r - NA_WIN_H // 2, 0, rows - NA_WIN_H))
            per_query.append([key_row0 + ki - r + (NA_WIN_H - 1) if r_start <= key_row0 + ki < r_start + NA_WIN_H
                              else None for ki in range(NA_KEY_ROWS)])
        table.append(per_query)
    return table


def _na_bias_blocks(rpb):
    w = GRID_W
    col = np.arange(w)
    col_start = np.clip(col - NA_WIN_W // 2, 0, w - NA_WIN_W)
    valid_col = (col[None, :] >= col_start[:, None]) & (col[None, :] < col_start[:, None] + NA_WIN_W)
    col_off = col[None, :] - col[:, None] + (NA_WIN_W - 1)
    col_sel = np.zeros((2 * NA_WIN_W - 1, w, w), np.float32)
    cc, jj = np.nonzero(valid_col)
    col_sel[col_off[cc, jj], cc, jj] = 1.0
    blocks = jnp.einsum("lhrx,xcj->lhrcj", rpb.astype(F32), col_sel, precision=lax.Precision.HIGHEST)
    return jnp.where(valid_col, blocks, MASK_VALUE)


def _ctx_attn_kernel(q_ref, k_ref, v_ref, o_ref):
    s = lax.dot_general(q_ref[0], k_ref[0], (((1,), (1,)), ((), ())), preferred_element_type=F32)
    p = jnp.exp(s - jnp.max(s, axis=-1, keepdims=True))
    o = jnp.dot(p.astype(BF16), v_ref[0], preferred_element_type=F32)
    o_ref[0] = (o / jnp.sum(p, axis=-1, keepdims=True)).astype(o_ref.dtype)


def _ctx_attn(pc3, off):
    bsz, ct, _ = pc3.shape
    q0, k0, v0 = off["nq"] // NA_DH, off["nk"] // NA_DH, off["nv"] // NA_DH
    spec = lambda c0: pl.BlockSpec((1, ct, NA_DH), lambda b, h: (b, 0, c0 + h))
    return pl.pallas_call(
        _ctx_attn_kernel,
        grid=(bsz, NA_HEADS),
        in_specs=[spec(q0), spec(k0), spec(v0)],
        out_specs=spec(0),
        out_shape=jax.ShapeDtypeStruct((bsz, ct, NA_D), BF16),
        compiler_params=_params(("arbitrary", "arbitrary"), 8 * _nbytes((ct, NA_DH), F32)),
        name="ctx_attn",
    )(pc3, pc3, pc3)


def _branch_kernel(gate_ref, ain_ref, ab_ref, ac_ref, ainp_ref, acp_ref, ainn_ref, acn_ref,
                   of_ref, ob_ref, gr_ref, na_ref, cw_ref, gn_ref, wa_ref, wg_ref, wn_ref, y_ref, *, tiles_per_seq):
    tm = ain_ref.shape[0]
    d = y_ref.shape[1]
    i = pl.program_id(0)
    tile_in_seq = i % tiles_per_seq
    u = ac_ref[...].astype(F32) * ain_ref[...].astype(F32)
    u_before = jnp.where(tile_in_seq == 0, 0.0, (acp_ref[...].astype(F32) * ainp_ref[...].astype(F32))[HALO_ROWS - 1:HALO_ROWS])
    u_after = jnp.where(tile_in_seq == tiles_per_seq - 1, 0.0, (acn_ref[...].astype(F32) * ainn_ref[...].astype(F32))[0:1])
    row = lax.broadcasted_iota(jnp.int32, u.shape, 0)
    u_prev = jnp.where(row == 0, u_before, pltpu.roll(u, 1, axis=0))
    u_next = jnp.where(row == tm - 1, u_after, pltpu.roll(u, tm - 1, axis=0))
    cw = cw_ref[...]
    a = ab_ref[...].astype(F32) * (cw[0:1] * u_prev + cw[1:2] * u + cw[2:3] * u_next)
    o = of_ref[...] + ob_ref[...]
    gn = gn_ref[...]
    normed = []
    for h in range(GLA_HEADS):
        oh = o[:, h * GLA_DV:(h + 1) * GLA_DV]
        normed.append(oh * lax.rsqrt(jnp.mean(oh * oh, axis=-1, keepdims=True) + RMS_EPS) * gn)
    r = gr_ref[...].astype(F32)
    g = jnp.concatenate(normed, axis=1) * (r * _sigmoid(r))
    y = _sigmoid(gate_ref[:, 0:d].astype(F32)) * jnp.dot(a.astype(BF16), wa_ref[...], preferred_element_type=F32)
    y = y + _sigmoid(gate_ref[:, d:2 * d].astype(F32)) * jnp.dot(g.astype(BF16), wg_ref[...], preferred_element_type=F32)
    y = y + _sigmoid(gate_ref[:, 2 * d:3 * d].astype(F32)) * jnp.dot(na_ref[...], wn_ref[...],
                                                                    preferred_element_type=F32)
    y_ref[...] = y.astype(BF16)


def _branch(p, off, og_f, og_b, na, conv_w, g_norm, wa, wg, wn, layer, seq_len, tm):
    m_rows = p.shape[0]
    d = wa.shape[2]
    tiles_per_seq = seq_len // tm
    halo = tm // HALO_ROWS
    n_halo = m_rows // HALO_ROWS
    cur = lambda blk: (lambda i: (i, blk))
    prev = lambda blk: (lambda i: (jnp.maximum(i * halo - 1, 0), blk))
    nxt = lambda blk: (lambda i: (jnp.minimum((i + 1) * halo, n_halo - 1), blk))
    a_in, a_b, a_c = off["a_in"] // A_WIDTH, off["a_b"] // A_WIDTH, off["a_c"] // A_WIDTH
    const = lambda i: (0, 0)
    lay = lambda i: (layer, 0, 0)
    in_specs = [pl.BlockSpec((tm, N_BRANCH * d), cur(off["gate"] // (N_BRANCH * d))),
                pl.BlockSpec((tm, A_WIDTH), cur(a_in)), pl.BlockSpec((tm, A_WIDTH), cur(a_b)),
                pl.BlockSpec((tm, A_WIDTH), cur(a_c)),
                pl.BlockSpec((HALO_ROWS, A_WIDTH), prev(a_in)), pl.BlockSpec((HALO_ROWS, A_WIDTH), prev(a_c)),
                pl.BlockSpec((HALO_ROWS, A_WIDTH), nxt(a_in)), pl.BlockSpec((HALO_ROWS, A_WIDTH), nxt(a_c)),
                pl.BlockSpec((tm, GLA_VD), cur(0)), pl.BlockSpec((tm, GLA_VD), cur(0)),
                pl.BlockSpec((tm, GLA_VD), cur(off["gr"] // GLA_VD)),
                pl.BlockSpec((tm, NA_D), cur(0)),
                pl.BlockSpec((3, A_WIDTH), const), pl.BlockSpec((1, GLA_DV), const),
                pl.BlockSpec((None, A_WIDTH, d), lay), pl.BlockSpec((None, GLA_VD, d), lay),
                pl.BlockSpec((None, NA_D, d), lay)]
    blocks = 2 * (_nbytes((tm, N_BRANCH * d), BF16) + 3 * _nbytes((tm, A_WIDTH), BF16) + 4 * _nbytes((tm, GLA_VD), F32)
                  + _nbytes((A_WIDTH + GLA_VD + NA_D, d), BF16) + _nbytes((tm, d), BF16)) + 6 * _nbytes((tm, d), F32)
    return pl.pallas_call(
        functools.partial(_branch_kernel, tiles_per_seq=tiles_per_seq),
        grid=(m_rows // tm,),
        in_specs=in_specs,
        out_specs=pl.BlockSpec((tm, d), cur(0)),
        out_shape=jax.ShapeDtypeStruct((m_rows, d), BF16),
        compiler_params=_params(("arbitrary",), blocks),
        name="branch_merge",
    )(p, p, p, p, p, p, p, p, og_f, og_b, p, na, conv_w, g_norm.reshape(1, GLA_DV), wa, wg, wn)


def _proj_kernel(y_ref, w_ref, h_ref, gm_ref, o_ref):
    o_ref[...] = h_ref[...] + gm_ref[0] * jnp.dot(y_ref[...], w_ref[...], preferred_element_type=F32)


def _proj_residual(y, w, layer, h, gate, group_of_tile, tm, tn):
    m_rows, d = h.shape
    blocks = 2 * (_nbytes((tm, d), BF16) + _nbytes((d, tn), BF16) + 2 * _nbytes((tm, tn), F32)) + _nbytes((tm, tn), F32)
    return pl.pallas_call(
        _proj_kernel,
        grid=(m_rows // tm, d // tn),
        in_specs=[pl.BlockSpec((tm, d), lambda i, j: (i, 0)),
                  pl.BlockSpec((None, d, tn), lambda i, j: (layer, 0, _serpentine(i, j, d // tn))),
                  pl.BlockSpec((tm, tn), lambda i, j: (i, _serpentine(i, j, d // tn))),
                  pl.BlockSpec((1, 1, tn), lambda i, j: (group_of_tile(i), 0, _serpentine(i, j, d // tn)))],
        out_specs=pl.BlockSpec((tm, tn), lambda i, j: (i, _serpentine(i, j, d // tn))),
        out_shape=jax.ShapeDtypeStruct((m_rows, d), F32),
        compiler_params=_params(("parallel", "arbitrary"), blocks),
        name="out_proj",
    )(y, w, h, gate)


def _ffn_kernel(x_ref, mod_ref, g_ref, w1_ref, w3_ref, w2_ref, gfin_ref, o_ref, n_scr, acc_scr, *, final_norm):
    j = pl.program_id(1)
    last = pl.num_programs(1) - 1

    def hidden_slice(n):
        a = jnp.dot(n, w1_ref[...], preferred_element_type=F32)
        b = jnp.dot(n, w3_ref[...], preferred_element_type=F32)
        u = (a * _sigmoid(a)) * b
        return jnp.dot(u.astype(BF16), w2_ref[...], preferred_element_type=F32)

    @pl.when(j == 0)
    def _():
        m = mod_ref[0]
        n = _mod_norm(x_ref[...], g_ref[...], m[3:4], m[4:5]).astype(BF16)
        n_scr[...] = n
        acc_scr[...] = hidden_slice(n)

    @pl.when((j > 0) & (j < last))
    def _():
        acc_scr[...] += hidden_slice(n_scr[...])

    @pl.when(j == last)
    def _():
        out = x_ref[...] + mod_ref[0][5:6] * (acc_scr[...] + hidden_slice(n_scr[...]))
        if final_norm:
            out = out * lax.rsqrt(jnp.mean(out * out, axis=-1, keepdims=True) + RMS_EPS) * gfin_ref[...]
        o_ref[...] = out


def _ffn(h, mod_l, g, w1, w3, w2, layer, g_final, group_of_tile, tm, tf, final_norm):
    m_rows, d = h.shape
    f = w1.shape[2]
    blocks = (4 * _nbytes((tm, d), F32) + 6 * _nbytes((d, tf), BF16) + _nbytes((tm, d), BF16) + _nbytes((tm, d), F32)
              + 4 * _nbytes((tm, tf), F32))
    return pl.pallas_call(
        functools.partial(_ffn_kernel, final_norm=final_norm),
        grid=(m_rows // tm, f // tf),
        in_specs=[pl.BlockSpec((tm, d), lambda i, j: (i, 0)),
                  pl.BlockSpec((1, 6, d), lambda i, j: (group_of_tile(i), 0, 0)),
                  pl.BlockSpec((1, d), lambda i, j: (0, 0)),
                  pl.BlockSpec((None, d, tf), lambda i, j: (layer, 0, _serpentine(i, j, f // tf))),
                  pl.BlockSpec((None, d, tf), lambda i, j: (layer, 0, _serpentine(i, j, f // tf))),
                  pl.BlockSpec((None, tf, d), lambda i, j: (layer, _serpentine(i, j, f // tf), 0)),
                  pl.BlockSpec((1, d), lambda i, j: (0, 0))],
        out_specs=pl.BlockSpec((tm, d), lambda i, j: (i, 0)),
        out_shape=jax.ShapeDtypeStruct((m_rows, d), F32),
        scratch_shapes=[pltpu.VMEM((tm, d), BF16), pltpu.VMEM((tm, d), F32)],
        compiler_params=_params(("parallel", "arbitrary"), blocks),
        name="ffn",
    )(h, mod_l, g.reshape(1, d), w1, w3, w2, g_final.reshape(1, d))


def _largest_tile(total, cap, align):
    t = min(total, cap)
    while total % t or t % align:
        t -= align
    return t


def _rope_tables(seq_len):
    pos = jnp.arange(seq_len, dtype=jnp.int32)
    rowf = (pos // GRID_W).astype(F32)
    colf = (pos % GRID_W).astype(F32)
    per_axis = GLA_DK // 2
    inv = ROPE_BASE ** (-jnp.arange(0, per_axis, 2, dtype=F32) / per_axis)
    ang = jnp.concatenate([rowf[:, None] * inv, colf[:, None] * inv], axis=-1)
    cos, sin = jnp.cos(ang), jnp.sin(ang)
    reps = LANE // GLA_DK
    return jnp.tile(cos, (1, 2 * reps)), jnp.tile(jnp.concatenate([-sin, sin], axis=-1), (1, reps))


def _pack_kernel(w_ref, o_ref, *, d_model, n_pad):
    sizes = (A_WIDTH, A_WIDTH, A_WIDTH, GLA_KD, GLA_KD, GLA_VD, GLA_VD, GLA_LOWRANK, GLA_LOWRANK, NA_D, NA_D, NA_D)
    start = dict(zip(("a_in", "a_b", "a_c", "gq", "gk", "gv", "gr", "gaf", "gab", "nq", "nk", "nv"),
                     np.concatenate([[0], np.cumsum(sizes)[:-1]]).tolist()))
    gate0 = sum(sizes)
    cols = w_ref.shape[2]
    pieces = [w_ref[0, gate0:gate0 + N_BRANCH * d_model, :],
              w_ref[0, 0:start["gq"], :],
              w_ref[0, start["gq"]:start["gk"], :] * (GLA_DK ** -0.5),
              w_ref[0, start["gk"]:start["nq"], :],
              jnp.zeros((LANE - 2 * GLA_LOWRANK, cols), F32),
              w_ref[0, start["nq"]:start["nk"], :] * (NA_DH ** -0.5),
              w_ref[0, start["nk"]:gate0, :]]
    used = sum(p.shape[0] for p in pieces)
    pieces.append(jnp.zeros((n_pad - used, cols), F32))
    o_ref[0] = jnp.concatenate(pieces, axis=0).astype(BF16)


def _pack_w_in(w_in, n_pad):
    depth, d, n_in = w_in.shape
    w_t = jnp.swapaxes(w_in, 1, 2)
    tc = LANE
    blocks = 2 * (_nbytes((n_in, tc), F32) + _nbytes((n_pad, tc), BF16)) + 2 * _nbytes((n_pad, tc), F32)
    return pl.pallas_call(
        functools.partial(_pack_kernel, d_model=d, n_pad=n_pad),
        grid=(depth, d // tc),
        in_specs=[pl.BlockSpec((1, n_in, tc), lambda l, i: (l, 0, i))],
        out_specs=pl.BlockSpec((1, n_pad, tc), lambda l, i: (l, 0, i)),
        out_shape=jax.ShapeDtypeStruct((depth, n_pad, d), BF16),
        compiler_params=_params(("arbitrary", "arbitrary"), blocks),
        name="pack_w_in",
    )(w_t)


def _pad_lowrank(wa, col0):
    return jnp.zeros((LANE, GLA_KD), BF16).at[col0:col0 + GLA_LOWRANK].set(wa.astype(BF16))


def kernel(x, c, ctx, c_ctx, w_ada, b_ada, g_mix, g_ffn, w_in, conv_w, gla_wa_f, gla_ba_f, gla_wa_b, gla_ba_b,
           gla_g_norm, na_rpb, w_a_out, w_g_out, w_n_out, w_o, w_ffn1, w_ffn3, w_ffn2, g_final):
    bsz, seq, d = x.shape
    ct = ctx.shape[1]
    depth = w_ada.shape[0]
    f = w_ffn1.shape[2]
    gla_block = GLA_CHUNK * GLA_CHUNKS_PER_STEP
    assert bsz + 1 <= MOD_ROWS and seq % GRID_W == 0 and seq % gla_block == 0 and ct % gla_block == 0
    assert (seq // GRID_W) % (NA_GROUP * NA_GROUPS_PER_STEP) == 0 and seq // GRID_W > NA_KEY_ROWS
    off, n_used = _layout(d)
    tn_in = 10 * LANE
    n_pad = -(-n_used // tn_in) * tn_in

    tm_big = _largest_tile(seq, 1024, SUBLANE)
    tm_ffn = _largest_tile(seq, 512, SUBLANE)
    tm_branch = _largest_tile(seq, 256, SUBLANE)
    tn_proj = _largest_tile(d, 1024, LANE)
    tf = _largest_tile(f, 512, LANE)
    lat_group = lambda tm: (lambda i: i // (seq // tm))
    ctx_group = lambda i: bsz

    cvec = jnp.zeros((MOD_ROWS, d), F32).at[:bsz].set(c).at[bsz].set(c_ctx)
    mod = _adaln(cvec, w_ada, b_ada).reshape(depth, MOD_ROWS, 6, d)
    rope = _rope_tables(seq)
    zero_state = jnp.zeros((bsz, GLA_PAIRS, LANE, 2 * GLA_DV), F32)

    w_in_p = _pack_w_in(w_in, n_pad)
    wa_o, wg_o, wn_o = w_a_out.astype(BF16), w_g_out.astype(BF16), w_n_out.astype(BF16)
    wo, w1, w3, w2 = w_o.astype(BF16), w_ffn1.astype(BF16), w_ffn3.astype(BF16), w_ffn2.astype(BF16)
    bias_blocks = _na_bias_blocks(na_rpb)

    h = x.reshape(bsz * seq, d)
    hc = ctx.reshape(bsz * ct, d)
    for l in range(depth):
        need_ctx = l < depth - 1
        last = l == depth - 1
        mod_l = mod[l]
        gm = mod_l[:, 2, :].reshape(MOD_ROWS, 1, d)
        waf, wab = _pad_lowrank(gla_wa_f[l], 0), _pad_lowrank(gla_wa_b[l], GLA_LOWRANK)
        baf, bab = gla_ba_f[l].reshape(1, GLA_KD), gla_ba_b[l].reshape(1, GLA_KD)

        p = _inproj(h, mod_l, g_mix[l], w_in_p, l, lat_group(tm_big), tm_big, tn_in)
        pc = _inproj(hc, mod_l, g_mix[l], w_in_p, l, ctx_group, ct, tn_in)
        p3 = p.reshape(bsz, seq, n_pad)
        pc3 = pc.reshape(bsz, ct, n_pad)

        ogc_f, ogc_b, s_f, s_b = _gla(pc3, off, waf, baf, wab, bab, zero_state, zero_state, None)
        og_f, og_b, _, _ = _gla(p3, off, waf, baf, wab, bab, s_f, s_b, rope)
        na = _na(p3, pc3, off, bias_blocks, l)

        y = _branch(p, off, og_f.reshape(bsz * seq, GLA_VD), og_b.reshape(bsz * seq, GLA_VD),
                    na.reshape(bsz * seq, NA_D), conv_w[l], gla_g_norm[l], wa_o, wg_o, wn_o, l, seq, tm_branch)
        h = _proj_residual(y, wo, l, h, gm, lat_group(tm_big), tm_big, tn_proj)
        h = _ffn(h, mod_l, g_ffn[l], w1, w3, w2, l, g_final, lat_group(tm_ffn), tm_ffn, tf, last)
        if need_ctx:
            na_c = _ctx_attn(pc3, off)
            yc = _branch(pc, off, ogc_f.reshape(bsz * ct, GLA_VD), ogc_b.reshape(bsz * ct, GLA_VD),
                         na_c.reshape(bsz * ct, NA_D), conv_w[l], gla_g_norm[l], wa_o, wg_o, wn_o, l, ct, ct)
            hc = _proj_residual(yc, wo, l, hc, gm, ctx_group, ct, tn_proj)
            hc = _ffn(hc, mod_l, g_ffn[l], w1, w3, w2, l, g_final, ctx_group, ct, tf, False)
    return h.reshape(bsz, seq, d)
```

```python
import functools

import numpy as np
import jax
import jax.numpy as jnp
from jax import lax
from jax.experimental import pallas as pl
from jax.experimental.pallas import tpu as pltpu

GRID_W = 64
RMS_EPS = 1e-6
ROPE_BASE = 10000.0
N_BRANCH = 3
A_WIDTH = 512
GLA_HEADS = 6
GLA_DK = 64
GLA_DV = 128
GLA_KD = GLA_HEADS * GLA_DK
GLA_VD = GLA_HEADS * GLA_DV
GLA_LOWRANK = 16
GLA_TAU = 16.0
GLA_CHUNK = 64
GLA_CHUNKS_PER_STEP = 4
GLA_SAFE_EXPONENT = 80.0
GLA_PAIRS = GLA_HEADS // 2
NA_HEADS = 6
NA_DH = 128
NA_D = NA_HEADS * NA_DH
NA_WIN_H = 8
NA_WIN_W = 16
NA_GROUP = 4
NA_KEY_ROWS = NA_GROUP + NA_WIN_H - 1
NA_GROUPS_PER_STEP = 4
MASK_VALUE = -1e30

LANE = 128
SUBLANE = 8
HALO_ROWS = 16
VMEM_PHYSICAL = 64 * 1024 * 1024
VMEM_INTERNAL = 12 * 1024 * 1024
MOD_ROWS = 8

BF16 = jnp.bfloat16
F32 = jnp.float32


def _params(semantics, block_bytes):
    limit = min(int(block_bytes) + VMEM_INTERNAL, VMEM_PHYSICAL - 4 * 1024 * 1024)
    return pltpu.CompilerParams(dimension_semantics=semantics, vmem_limit_bytes=limit)


def _nbytes(shape, dtype):
    return int(np.prod(shape)) * jnp.dtype(dtype).itemsize


def _sigmoid(x):
    return 0.5 * jnp.tanh(0.5 * x) + 0.5


def _serpentine(i, j, n):
    return jnp.where(i % 2 == 0, j, n - 1 - j)


def _layout(d_model):
    off = {}
    pos = 0
    for name, width, block in (("gate", N_BRANCH * d_model, N_BRANCH * d_model),
                               ("a_in", A_WIDTH, A_WIDTH), ("a_b", A_WIDTH, A_WIDTH), ("a_c", A_WIDTH, A_WIDTH),
                               ("gq", GLA_KD, GLA_KD), ("gk", GLA_KD, GLA_KD), ("gv", GLA_VD, GLA_VD),
                               ("gr", GLA_VD, GLA_VD), ("gab", LANE, LANE),
                               ("nq", NA_D, NA_DH), ("nk", NA_D, NA_DH), ("nv", NA_D, NA_DH)):
        assert pos % block == 0, (name, pos, block)
        off[name] = pos
        pos += width
    return off, pos


def _adaln_kernel(c_ref, w_ref, b_ref, o_ref):
    cv = c_ref[...]
    s = cv * jax.nn.sigmoid(cv)
    o_ref[0] = jnp.dot(s, w_ref[0], preferred_element_type=F32) + b_ref[0]


def _adaln(cvec, w_ada, b_ada):
    depth, d, n = w_ada.shape
    tn = 1024 if n % 1024 == 0 else n
    blocks = 2 * (_nbytes((d, tn), F32) + 2 * _nbytes((MOD_ROWS, tn), F32)) + _nbytes((MOD_ROWS, d), F32)
    return pl.pallas_call(
        _adaln_kernel,
        grid=(depth, n // tn),
        in_specs=[pl.BlockSpec((MOD_ROWS, d), lambda l, j: (0, 0)),
                  pl.BlockSpec((1, d, tn), lambda l, j: (l, 0, j)),
                  pl.BlockSpec((1, 1, tn), lambda l, j: (l, 0, j))],
        out_specs=pl.BlockSpec((1, MOD_ROWS, tn), lambda l, j: (l, 0, j)),
        out_shape=jax.ShapeDtypeStruct((depth, MOD_ROWS, n), F32),
        compiler_params=_params(("arbitrary", "arbitrary"), blocks),
        name="adaln",
    )(cvec, w_ada, b_ada.reshape(depth, 1, n))


def _mod_norm(x, g, shift, scale):
    y = x * lax.rsqrt(jnp.mean(x * x, axis=-1, keepdims=True) + RMS_EPS)
    return (y * g) * (1.0 + scale) + shift


def _inproj_kernel(x_ref, mod_ref, g_ref, w_ref, o_ref, n_scr):
    def project(n):
        o_ref[...] = lax.dot_general(n, w_ref[...], (((1,), (1,)), ((), ())),
                                     preferred_element_type=F32).astype(o_ref.dtype)

    @pl.when(pl.program_id(1) == 0)
    def _():
        m = mod_ref[0]
        n = _mod_norm(x_ref[...], g_ref[...], m[0:1], m[1:2]).astype(BF16)
        n_scr[...] = n
        project(n)

    @pl.when(pl.program_id(1) != 0)
    def _():
        project(n_scr[...])


def _inproj(h, mod_l, g, w, layer, group_of_tile, tm, tn):
    m_rows, d = h.shape
    n = w.shape[1]
    blocks = (2 * (_nbytes((tm, d), F32) + _nbytes((d, tn), BF16) + _nbytes((tm, tn), BF16))
              + _nbytes((tm, d), BF16) + 2 * _nbytes((8, d), F32) + _nbytes((tm, tn), F32))
    return pl.pallas_call(
        _inproj_kernel,
        grid=(m_rows // tm, n // tn),
        in_specs=[pl.BlockSpec((tm, d), lambda i, j: (i, 0)),
                  pl.BlockSpec((1, 6, d), lambda i, j: (group_of_tile(i), 0, 0)),
                  pl.BlockSpec((1, d), lambda i, j: (0, 0)),
                  pl.BlockSpec((None, tn, d), lambda i, j: (layer, _serpentine(i, j, n // tn), 0))],
        out_specs=pl.BlockSpec((tm, tn), lambda i, j: (i, _serpentine(i, j, n // tn))),
        out_shape=jax.ShapeDtypeStruct((m_rows, n), BF16),
        scratch_shapes=[pltpu.VMEM((tm, d), BF16)],
        compiler_params=_params(("parallel", "arbitrary"), blocks),
        name="inproj",
    )(h, mod_l, g.reshape(1, d), w)


def _rope_pairs(x, cos, sin):
    lane = lax.broadcasted_iota(jnp.int32, cos.shape, 1)
    first_half = (lane % GLA_DK) < (GLA_DK // 2)
    out = []
    for p in range(GLA_PAIRS):
        xs = x[:, p * LANE:(p + 1) * LANE]
        swapped = jnp.where(first_half, pltpu.roll(xs, LANE - GLA_DK // 2, axis=1), pltpu.roll(xs, GLA_DK // 2, axis=1))
        out.append(xs * cos + swapped * sin)
    return jnp.concatenate(out, axis=1)


def _minus_chunk_row(x, offset, sign=1.0):
    parts = []
    for c in range(x.shape[0] // GLA_CHUNK):
        chunk = x[c * GLA_CHUNK:(c + 1) * GLA_CHUNK]
        ref_row = x[c * GLA_CHUNK + offset:c * GLA_CHUNK + offset + 1]
        parts.append(chunk - ref_row if sign > 0 else ref_row - chunk)
    return jnp.concatenate(parts, axis=0)


def _gla_prepare(q, k, ab, wa, ba, cos, sin, reverse):
    rows = q.shape[0]
    z = jnp.dot(ab, wa, preferred_element_type=F32) + ba
    la = (jnp.minimum(z, 0.0) - jnp.log(1.0 + jnp.exp(-jnp.abs(z)))) * (1.0 / GLA_TAU)
    r = lax.broadcasted_iota(jnp.int32, (rows, rows), 0)
    c = lax.broadcasted_iota(jnp.int32, (rows, rows), 1)
    tri = ((r // GLA_CHUNK == c // GLA_CHUNK) & ((c >= r) if reverse else (c <= r))).astype(BF16)
    la_hi = la.astype(BF16)
    la_lo = (la - la_hi.astype(F32)).astype(BF16)
    b = jnp.dot(tri, la_hi, preferred_element_type=F32) + jnp.dot(tri, la_lo, preferred_element_type=F32)
    q = q.astype(F32)
    k = k.astype(F32)
    if cos is not None:
        q = _rope_pairs(q, cos, sin)
        k = _rope_pairs(k, cos, sin)
    return q, k, b


def _gla_block(q, k, b, v, s_scr, o_ref, inter_scr, reverse):
    rows = q.shape[0]
    n_chunks = rows // GLA_CHUNK
    half = GLA_CHUNK // 2
    last_row = 0 if reverse else GLA_CHUNK - 1
    rel = _minus_chunk_row(b, half if reverse else half - 1)
    q_mid = (q * jnp.exp(rel))
    k_mid = (k * jnp.exp(-rel)).astype(BF16)
    q_state = (q * jnp.exp(b)).astype(BF16)
    k_last = (k * jnp.exp(_minus_chunk_row(b, last_row, sign=-1.0))).astype(BF16)
    row = lax.broadcasted_iota(jnp.int32, (2 * GLA_CHUNK, GLA_CHUNK), 0) % GLA_CHUNK
    col = lax.broadcasted_iota(jnp.int32, (2 * GLA_CHUNK, GLA_CHUNK), 1)
    visible = (row <= col) if reverse else (row >= col)
    first_head = lax.broadcasted_iota(jnp.int32, (rows, LANE), 1) < GLA_DK
    blk_r = lax.broadcasted_iota(jnp.int32, (LANE, 2 * GLA_DV), 0) < GLA_DK
    blk_c = lax.broadcasted_iota(jnp.int32, (LANE, 2 * GLA_DV), 1) < GLA_DV
    same_head = blk_r == blk_c
    nt = (((1,), (1,)), ((), ()))
    tn = (((0,), (0,)), ((), ()))
    q_first = [jnp.where(first_head, q_mid[:, p * LANE:(p + 1) * LANE], 0.0).astype(BF16) for p in range(GLA_PAIRS)]
    q_second = [jnp.where(first_head, 0.0, q_mid[:, p * LANE:(p + 1) * LANE]).astype(BF16) for p in range(GLA_PAIRS)]
    order = range(n_chunks - 1, -1, -1) if reverse else range(n_chunks)
    for c in order:
        rs = slice(c * GLA_CHUNK, (c + 1) * GLA_CHUNK)
        intra, inter = [], []
        for p in range(GLA_PAIRS):
            sl = slice(p * LANE, (p + 1) * LANE)
            q_stack = jnp.concatenate([q_first[p][rs], q_second[p][rs]], axis=0)
            att = lax.dot_general(q_stack, k_mid[rs, sl], nt, preferred_element_type=F32)
            att = jnp.where(visible, att, 0.0).astype(BF16)
            v_pair = v[rs, 2 * p * GLA_DV:(2 * p + 2) * GLA_DV]
            intra.append(jnp.dot(att[:GLA_CHUNK], v_pair[:, :GLA_DV], preferred_element_type=F32))
            intra.append(jnp.dot(att[GLA_CHUNK:], v_pair[:, GLA_DV:], preferred_element_type=F32))
            s_p = s_scr[p]
            inter.append(jnp.dot(q_state[rs, sl], s_p.astype(BF16), preferred_element_type=F32))
            update = lax.dot_general(k_last[rs, sl], v_pair, tn, preferred_element_type=F32)
            decay_row = jnp.exp(b[c * GLA_CHUNK + last_row:c * GLA_CHUNK + last_row + 1, sl])
            decay_col = jnp.transpose(jnp.broadcast_to(decay_row, (SUBLANE, LANE)))[:, 0:1]
            s_scr[p] = s_p * decay_col + jnp.where(same_head, update, 0.0)
        inter = jnp.concatenate(inter, axis=1)
        o_ref[0, rs, :] = jnp.concatenate(intra, axis=1) + inter
        inter_scr[rs, :] = inter
    return jnp.max(jnp.abs(rel))


def _gla_chunk_exact(q_scr, k_scr, b_scr, v_scr, o_ref, row0, reverse):
    length = GLA_CHUNK
    rows = pl.ds(row0, length)
    q = q_scr[rows, :]
    b = b_scr[rows, :]
    row = lax.broadcasted_iota(jnp.int32, (length, 1), 0)
    first_head = lax.broadcasted_iota(jnp.int32, (length, LANE), 1) < GLA_DK

    def body(s, carry):
        src = pl.ds(row0 + s, 1)
        w = q * k_scr[src, :] * jnp.exp(jnp.minimum(b - b_scr[src, :], 0.0))
        w = jnp.where((row <= s) if reverse else (row >= s), w, 0.0)
        v_row = v_scr[src, :]
        for p in range(GLA_PAIRS):
            wp = w[:, p * LANE:(p + 1) * LANE]
            for hh in range(2):
                a = jnp.sum(jnp.where(first_head == (hh == 0), wp, 0.0), axis=1, keepdims=True)
                lanes = slice((2 * p + hh) * GLA_DV, (2 * p + hh + 1) * GLA_DV)
                o_ref[0, rows, lanes] += a * v_row[:, lanes]
        return carry

    lax.fori_loop(0, length, body, 0)


def _gla_kernel(*refs, use_rope):
    qf, kf, vf, abf, qb, kb, vb, abb = refs[:8]
    pos = 8
    if use_rope:
        cosf, sinf, cosb, sinb = refs[pos:pos + 4]
        pos += 4
    (waf, baf, wab, bab, s0f, s0b, of_ref, ob_ref, sf_ref, sb_ref,
     sf_scr, sb_scr, inter_f, inter_b, q_scr, k_scr, b_scr, v_scr) = refs[pos:]
    step = pl.program_id(1)

    @pl.when(step == 0)
    def _():
        sf_scr[...] = s0f[0]
        sb_scr[...] = s0b[0]

    directions = ((False, (qf, kf, vf, abf), (cosf, sinf) if use_rope else None, waf, baf, sf_scr, of_ref, inter_f),
                  (True, (qb, kb, vb, abb), (cosb, sinb) if use_rope else None, wab, bab, sb_scr, ob_ref, inter_b))

    def prepared(reverse, refs4, rope_refs, w_ref, bias_ref):
        q_ref, k_ref, _, ab_ref = refs4
        cos = rope_refs[0][...] if use_rope else None
        sin = rope_refs[1][...] if use_rope else None
        return _gla_prepare(q_ref[0], k_ref[0], ab_ref[0], w_ref[...], bias_ref[...], cos, sin, reverse)

    margin = jnp.float32(0.0)
    for reverse, refs4, rope_refs, w_ref, bias_ref, s_scr, o_ref, inter_scr in directions:
        q, k, b = prepared(reverse, refs4, rope_refs, w_ref, bias_ref)
        m = _gla_block(q, k, b, refs4[2][0], s_scr, o_ref, inter_scr, reverse)
        margin = jnp.maximum(margin, m)

    @pl.when(margin > GLA_SAFE_EXPONENT)
    def _():
        for reverse, refs4, rope_refs, w_ref, bias_ref, s_scr, o_ref, inter_scr in directions:
            q, k, b = prepared(reverse, refs4, rope_refs, w_ref, bias_ref)
            q_scr[...] = q
            k_scr[...] = k
            b_scr[...] = b
            v_scr[...] = refs4[2][0].astype(F32)
            o_ref[0] = inter_scr[...]
            for c in range(GLA_CHUNKS_PER_STEP):
                _gla_chunk_exact(q_scr, k_scr, b_scr, v_scr, o_ref, c * GLA_CHUNK, reverse)

    @pl.when(step == pl.num_programs(1) - 1)
    def _():
        sf_ref[0] = sf_scr[...]
        sb_ref[0] = sb_scr[...]


def _gla(p3, off, waf, baf, wab, bab, s0f, s0b, rope):
    bsz, t, _ = p3.shape
    length = GLA_CHUNK * GLA_CHUNKS_PER_STEP
    n = t // length
    fwd = lambda blk: (lambda b, i: (b, i, blk))
    bwd = lambda blk: (lambda b, i: (b, n - 1 - i, blk))
    q_blk, k_blk = off["gq"] // GLA_KD, off["gk"] // GLA_KD
    v_blk, ab_blk = off["gv"] // GLA_VD, off["gab"] // LANE
    seq_specs = lambda m: [pl.BlockSpec((1, length, GLA_KD), m(q_blk)), pl.BlockSpec((1, length, GLA_KD), m(k_blk)),
                           pl.BlockSpec((1, length, GLA_VD), m(v_blk)), pl.BlockSpec((1, length, LANE), m(ab_blk))]
    in_specs = seq_specs(fwd) + seq_specs(bwd)
    args = [p3] * 8
    if rope is not None:
        in_specs += [pl.BlockSpec((length, LANE), lambda b, i: (i, 0))] * 2
        in_specs += [pl.BlockSpec((length, LANE), lambda b, i: (n - 1 - i, 0))] * 2
        args += [rope[0], rope[1], rope[0], rope[1]]
    const2 = lambda b, i: (0, 0)
    state_dims = (GLA_PAIRS, LANE, 2 * GLA_DV)
    state_spec = pl.BlockSpec((1,) + state_dims, lambda b, i: (b, 0, 0, 0))
    in_specs += [pl.BlockSpec((LANE, GLA_KD), const2), pl.BlockSpec((1, GLA_KD), const2),
                 pl.BlockSpec((LANE, GLA_KD), const2), pl.BlockSpec((1, GLA_KD), const2),
                 state_spec, state_spec]
    args += [waf, baf, wab, bab, s0f, s0b]
    state_shape = jax.ShapeDtypeStruct((bsz,) + state_dims, F32)
    out_shape = jax.ShapeDtypeStruct((bsz, t, GLA_VD), F32)
    blocks = 2 * (2 * (2 * _nbytes((length, GLA_KD), F32) + 2 * _nbytes((length, GLA_VD), F32) + 5 * _nbytes((length, LANE), F32))
                  + 4 * _nbytes(state_dims, F32) + 2 * _nbytes((LANE, GLA_KD), F32))
    blocks += 2 * _nbytes(state_dims, F32) + 2 * _nbytes((length, GLA_VD), F32)
    return pl.pallas_call(
        functools.partial(_gla_kernel, use_rope=rope is not None),
        grid=(bsz, n),
        in_specs=in_specs,
        out_specs=[pl.BlockSpec((1, length, GLA_VD), fwd(0)), pl.BlockSpec((1, length, GLA_VD), bwd(0)),
                   state_spec, state_spec],
        out_shape=[out_shape, out_shape, state_shape, state_shape],
        scratch_shapes=[pltpu.VMEM(state_dims, F32), pltpu.VMEM(state_dims, F32),
                        pltpu.VMEM((length, GLA_VD), F32), pltpu.VMEM((length, GLA_VD), F32),
                        pltpu.VMEM((length, GLA_KD), F32), pltpu.VMEM((length, GLA_KD), F32),
                        pltpu.VMEM((length, GLA_KD), F32), pltpu.VMEM((length, GLA_VD), F32)],
        compiler_params=_params(("arbitrary", "arbitrary"), blocks),
        name="gla",
    )(*args)


def _na_kernel(q_ref, k_ref, v_ref, kc_ref, vc_ref, blk_ref, o_ref, tab_scr, *, rows):
    nt = (((1,), (1,)), ((), ()))
    step = pl.program_id(2)

    @pl.when(step == 0)
    def _():
        masked = jnp.full((GRID_W, GRID_W), MASK_VALUE, F32)
        for variant, per_query in enumerate(_na_row_offsets(rows)):
            for rq, offsets in enumerate(per_query):
                tab_scr[variant, rq * GRID_W:(rq + 1) * GRID_W, :] = jnp.concatenate(
                    [masked if r is None else blk_ref[0, r] for r in offsets], axis=1)

    kc = kc_ref[0]
    vc = vc_ref[0]
    nq = NA_GROUP * GRID_W
    nk = NA_KEY_ROWS * GRID_W
    for g in range(NA_GROUPS_PER_STEP):
        r0 = (step * NA_GROUPS_PER_STEP + g) * NA_GROUP
        key_row0 = jnp.clip(r0 - NA_WIN_H // 2, 0, rows - NA_KEY_ROWS)
        k_off = pl.multiple_of(key_row0 * GRID_W, GRID_W)
        variant = jnp.where(r0 == 0, 0, jnp.where(r0 == rows - NA_GROUP, 2, 1))
        q = q_ref[0, g * nq:(g + 1) * nq, :]
        kr = k_ref[0, pl.ds(k_off, nk), :]
        vr = v_ref[0, pl.ds(k_off, nk), :]
        s_loc = lax.dot_general(q, kr, nt, preferred_element_type=F32) + tab_scr[variant]
        s_ctx = lax.dot_general(q, kc, nt, preferred_element_type=F32)
        m = jnp.maximum(jnp.max(s_loc, axis=-1, keepdims=True), jnp.max(s_ctx, axis=-1, keepdims=True))
        p_loc = jnp.exp(s_loc - m)
        p_ctx = jnp.exp(s_ctx - m)
        denom = jnp.sum(p_loc, axis=-1, keepdims=True) + jnp.sum(p_ctx, axis=-1, keepdims=True)
        o = (jnp.dot(p_loc.astype(BF16), vr, preferred_element_type=F32)
             + jnp.dot(p_ctx.astype(BF16), vc, preferred_element_type=F32))
        o_ref[0, g * nq:(g + 1) * nq, :] = (o / denom).astype(o_ref.dtype)


def _na(p3, pc3, off, bias_blocks, layer):
    bsz, t, _ = p3.shape
    ct = pc3.shape[1]
    rows = t // GRID_W
    tq = NA_GROUPS_PER_STEP * NA_GROUP * GRID_W
    nq, nk = NA_GROUP * GRID_W, NA_KEY_ROWS * GRID_W
    n_off = 2 * NA_WIN_H - 1
    q0, k0, v0 = off["nq"] // NA_DH, off["nk"] // NA_DH, off["nv"] // NA_DH
    blocks = (2 * (2 * _nbytes((tq, NA_DH), F32) + 2 * _nbytes((t, NA_DH), BF16) + 2 * _nbytes((ct, NA_DH), BF16)
                   + _nbytes((n_off, GRID_W, LANE), F32)) + _nbytes((3, nq, nk), F32)
              + 3 * NA_GROUPS_PER_STEP * _nbytes((nq, nk + ct), F32))
    return pl.pallas_call(
        functools.partial(_na_kernel, rows=rows),
        grid=(bsz, NA_HEADS, rows // (NA_GROUPS_PER_STEP * NA_GROUP)),
        in_specs=[pl.BlockSpec((1, tq, NA_DH), lambda b, h, r: (b, r, q0 + h)),
                  pl.BlockSpec((1, t, NA_DH), lambda b, h, r: (b, 0, k0 + h)),
                  pl.BlockSpec((1, t, NA_DH), lambda b, h, r: (b, 0, v0 + h)),
                  pl.BlockSpec((1, ct, NA_DH), lambda b, h, r: (b, 0, k0 + h)),
                  pl.BlockSpec((1, ct, NA_DH), lambda b, h, r: (b, 0, v0 + h)),
                  pl.BlockSpec((None, 1, n_off, GRID_W, GRID_W), lambda b, h, r: (layer, h, 0, 0, 0))],
        out_specs=pl.BlockSpec((1, tq, NA_DH), lambda b, h, r: (b, r, h)),
        out_shape=jax.ShapeDtypeStruct((bsz, t, NA_D), BF16),
        scratch_shapes=[pltpu.VMEM((3, nq, nk), F32)],
        compiler_params=_params(("arbitrary", "arbitrary", "arbitrary"), blocks),
        name="natten",
    )(p3, p3, p3, pc3, pc3, bias_blocks)


def _na_row_offsets(rows):
    table = []
    for r0 in (0, NA_GROUP, rows - NA_GROUP):
        key_row0 = int(np.clip(r0 - NA_WIN_H // 2, 0, rows - NA_KEY_ROWS))
        per_query = []
        for rq in range(NA_GROUP):
            r = r0 + rq
            r_start = int(np.clip(r - NA_WIN_H // 2, 0, rows - NA_WIN_H))
            per_query.append([key_row0 + ki - r + (NA_WIN_H - 1) if r_start <= key_row0 + ki < r_start + NA_WIN_H
                              else None for ki in range(NA_KEY_ROWS)])
        table.append(per_query)
    return table


def _na_bias_blocks(rpb):
    w = GRID_W
    col = np.arange(w)
    col_start = np.clip(col - NA_WIN_W // 2, 0, w - NA_WIN_W)
    valid_col = (col[None, :] >= col_start[:, None]) & (col[None, :] < col_start[:, None] + NA_WIN_W)
    col_off = col[None, :] - col[:, None] + (NA_WIN_W - 1)
    col_sel = np.zeros((2 * NA_WIN_W - 1, w, w), np.float32)
    cc, jj = np.nonzero(valid_col)
    col_sel[col_off[cc, jj], cc, jj] = 1.0
    blocks = jnp.einsum("lhrx,xcj->lhrcj", rpb.astype(F32), col_sel, precision=lax.Precision.HIGHEST)
    return jnp.where(valid_col, blocks, MASK_VALUE)


def _ctx_attn_kernel(q_ref, k_ref, v_ref, o_ref):
    s = lax.dot_general(q_ref[0], k_ref[0], (((1,), (1,)), ((), ())), preferred_element_type=F32)
    p = jnp.exp(s - jnp.max(s, axis=-1, keepdims=True))
    o = jnp.dot(p.astype(BF16), v_ref[0], preferred_element_type=F32)
    o_ref[0] = (o / jnp.sum(p, axis=-1, keepdims=True)).astype(o_ref.dtype)


def _ctx_attn(pc3, off):
    bsz, ct, _ = pc3.shape
    q0, k0, v0 = off["nq"] // NA_DH, off["nk"] // NA_DH, off["nv"] // NA_DH
    spec = lambda c0: pl.BlockSpec((1, ct, NA_DH), lambda b, h: (b, 0, c0 + h))
    return pl.pallas_call(
        _ctx_attn_kernel,
        grid=(bsz, NA_HEADS),
        in_specs=[spec(q0), spec(k0), spec(v0)],
        out_specs=spec(0),
        out_shape=jax.ShapeDtypeStruct((bsz, ct, NA_D), BF16),
        compiler_params=_params(("arbitrary", "arbitrary"), 8 * _nbytes((ct, NA_DH), F32)),
        name="ctx_attn",
    )(pc3, pc3, pc3)


def _branch_kernel(gate_ref, ain_ref, ab_ref, ac_ref, ainp_ref, acp_ref, ainn_ref, acn_ref,
                   of_ref, ob_ref, gr_ref, na_ref, cw_ref, gn_ref, wa_ref, wg_ref, wn_ref, wo_ref, h_ref, gm_ref,
                   o_ref, *, tiles_per_seq):
    tm = ain_ref.shape[0]
    d = o_ref.shape[1]
    i = pl.program_id(0)
    tile_in_seq = i % tiles_per_seq
    u = ac_ref[...].astype(F32) * ain_ref[...].astype(F32)
    u_before = jnp.where(tile_in_seq == 0, 0.0, (acp_ref[...].astype(F32) * ainp_ref[...].astype(F32))[HALO_ROWS - 1:HALO_ROWS])
    u_after = jnp.where(tile_in_seq == tiles_per_seq - 1, 0.0, (acn_ref[...].astype(F32) * ainn_ref[...].astype(F32))[0:1])
    row = lax.broadcasted_iota(jnp.int32, u.shape, 0)
    u_prev = jnp.where(row == 0, u_before, pltpu.roll(u, 1, axis=0))
    u_next = jnp.where(row == tm - 1, u_after, pltpu.roll(u, tm - 1, axis=0))
    cw = cw_ref[...]
    a = ab_ref[...].astype(F32) * (cw[0:1] * u_prev + cw[1:2] * u + cw[2:3] * u_next)
    o = of_ref[...] + ob_ref[...]
    gn = gn_ref[...]
    normed = []
    for h in range(GLA_HEADS):
        oh = o[:, h * GLA_DV:(h + 1) * GLA_DV]
        normed.append(oh * lax.rsqrt(jnp.mean(oh * oh, axis=-1, keepdims=True) + RMS_EPS) * gn)
    r = gr_ref[...].astype(F32)
    g = jnp.concatenate(normed, axis=1) * (r * _sigmoid(r))
    y = _sigmoid(gate_ref[:, 0:d].astype(F32)) * jnp.dot(a.astype(BF16), wa_ref[...], preferred_element_type=F32)
    y = y + _sigmoid(gate_ref[:, d:2 * d].astype(F32)) * jnp.dot(g.astype(BF16), wg_ref[...], preferred_element_type=F32)
    y = y + _sigmoid(gate_ref[:, 2 * d:3 * d].astype(F32)) * jnp.dot(na_ref[...], wn_ref[...],
                                                                    preferred_element_type=F32)
    o_ref[...] = h_ref[...] + gm_ref[0] * jnp.dot(y.astype(BF16), wo_ref[...], preferred_element_type=F32)


def _branch(p, off, og_f, og_b, na, conv_w, g_norm, wa, wg, wn, wo, layer, h, gate, group_of_tile, seq_len, tm):
    m_rows = p.shape[0]
    d = wa.shape[2]
    tiles_per_seq = seq_len // tm
    halo = tm // HALO_ROWS
    n_halo = m_rows // HALO_ROWS
    cur = lambda blk: (lambda i: (i, blk))
    prev = lambda blk: (lambda i: (jnp.maximum(i * halo - 1, 0), blk))
    nxt = lambda blk: (lambda i: (jnp.minimum((i + 1) * halo, n_halo - 1), blk))
    a_in, a_b, a_c = off["a_in"] // A_WIDTH, off["a_b"] // A_WIDTH, off["a_c"] // A_WIDTH
    const = lambda i: (0, 0)
    lay = lambda i: (layer, 0, 0)
    in_specs = [pl.BlockSpec((tm, N_BRANCH * d), cur(off["gate"] // (N_BRANCH * d))),
                pl.BlockSpec((tm, A_WIDTH), cur(a_in)), pl.BlockSpec((tm, A_WIDTH), cur(a_b)),
                pl.BlockSpec((tm, A_WIDTH), cur(a_c)),
                pl.BlockSpec((HALO_ROWS, A_WIDTH), prev(a_in)), pl.BlockSpec((HALO_ROWS, A_WIDTH), prev(a_c)),
                pl.BlockSpec((HALO_ROWS, A_WIDTH), nxt(a_in)), pl.BlockSpec((HALO_ROWS, A_WIDTH), nxt(a_c)),
                pl.BlockSpec((tm, GLA_VD), cur(0)), pl.BlockSpec((tm, GLA_VD), cur(0)),
                pl.BlockSpec((tm, GLA_VD), cur(off["gr"] // GLA_VD)),
                pl.BlockSpec((tm, NA_D), cur(0)),
                pl.BlockSpec((3, A_WIDTH), const), pl.BlockSpec((1, GLA_DV), const),
                pl.BlockSpec((None, A_WIDTH, d), lay, pipeline_mode=pl.Buffered(1)),
                pl.BlockSpec((None, GLA_VD, d), lay, pipeline_mode=pl.Buffered(1)),
                pl.BlockSpec((None, NA_D, d), lay, pipeline_mode=pl.Buffered(1)),
                pl.BlockSpec((None, d, d), lay, pipeline_mode=pl.Buffered(1)),
                pl.BlockSpec((tm, d), cur(0)),
                pl.BlockSpec((1, 1, d), lambda i: (group_of_tile(i), 0, 0))]
    blocks = (2 * (_nbytes((tm, N_BRANCH * d), BF16) + 3 * _nbytes((tm, A_WIDTH), BF16) + 4 * _nbytes((tm, GLA_VD), F32)
                   + 2 * _nbytes((tm, d), F32)) + _nbytes((A_WIDTH + GLA_VD + NA_D + d, d), BF16) + 6 * _nbytes((tm, d), F32))
    return pl.pallas_call(
        functools.partial(_branch_kernel, tiles_per_seq=tiles_per_seq),
        grid=(m_rows // tm,),
        in_specs=in_specs,
        out_specs=pl.BlockSpec((tm, d), cur(0)),
        out_shape=jax.ShapeDtypeStruct((m_rows, d), F32),
        compiler_params=_params(("arbitrary",), blocks),
        name="branch_merge",
    )(p, p, p, p, p, p, p, p, og_f, og_b, p, na, conv_w, g_norm.reshape(1, GLA_DV), wa, wg, wn, wo, h, gate)


def _ffn_kernel(x_ref, mod_ref, g_ref, w1_ref, w3_ref, w2_ref, gfin_ref, o_ref, n_scr, acc_scr, *, final_norm):
    j = pl.program_id(1)
    last = pl.num_programs(1) - 1

    def hidden_slice(n):
        a = jnp.dot(n, w1_ref[...], preferred_element_type=F32)
        b = jnp.dot(n, w3_ref[...], preferred_element_type=F32)
        u = (a * _sigmoid(a)) * b
        return jnp.dot(u.astype(BF16), w2_ref[...], preferred_element_type=F32)

    @pl.when(j == 0)
    def _():
        m = mod_ref[0]
        n = _mod_norm(x_ref[...], g_ref[...], m[3:4], m[4:5]).astype(BF16)
        n_scr[...] = n
        acc_scr[...] = hidden_slice(n)

    @pl.when((j > 0) & (j < last))
    def _():
        acc_scr[...] += hidden_slice(n_scr[...])

    @pl.when(j == last)
    def _():
        out = x_ref[...] + mod_ref[0][5:6] * (acc_scr[...] + hidden_slice(n_scr[...]))
        if final_norm:
            out = out * lax.rsqrt(jnp.mean(out * out, axis=-1, keepdims=True) + RMS_EPS) * gfin_ref[...]
        o_ref[...] = out


def _ffn(h, mod_l, g, w1, w3, w2, layer, g_final, group_of_tile, tm, tf, final_norm):
    m_rows, d = h.shape
    f = w1.shape[2]
    blocks = (4 * _nbytes((tm, d), F32) + 6 * _nbytes((d, tf), BF16) + _nbytes((tm, d), BF16) + _nbytes((tm, d), F32)
              + 4 * _nbytes((tm, tf), F32))
    return pl.pallas_call(
        functools.partial(_ffn_kernel, final_norm=final_norm),
        grid=(m_rows // tm, f // tf),
        in_specs=[pl.BlockSpec((tm, d), lambda i, j: (i, 0)),
                  pl.BlockSpec((1, 6, d), lambda i, j: (group_of_tile(i), 0, 0)),
                  pl.BlockSpec((1, d), lambda i, j: (0, 0)),
                  pl.BlockSpec((None, d, tf), lambda i, j: (layer, 0, _serpentine(i, j, f // tf))),
                  pl.BlockSpec((None, d, tf), lambda i, j: (layer, 0, _serpentine(i, j, f // tf))),
                  pl.BlockSpec((None, tf, d), lambda i, j: (layer, _serpentine(i, j, f // tf), 0)),
                  pl.BlockSpec((1, d), lambda i, j: (0, 0))],
        out_specs=pl.BlockSpec((tm, d), lambda i, j: (i, 0)),
        out_shape=jax.ShapeDtypeStruct((m_rows, d), F32),
        scratch_shapes=[pltpu.VMEM((tm, d), BF16), pltpu.VMEM((tm, d), F32)],
        compiler_params=_params(("parallel", "arbitrary"), blocks),
        name="ffn",
    )(h, mod_l, g.reshape(1, d), w1, w3, w2, g_final.reshape(1, d))


def _largest_tile(total, cap, align):
    t = min(total, cap)
    while total % t or t % align:
        t -= align
    return t


def _rope_tables(seq_len):
    pos = jnp.arange(seq_len, dtype=jnp.int32)
    rowf = (pos // GRID_W).astype(F32)
    colf = (pos % GRID_W).astype(F32)
    per_axis = GLA_DK // 2
    inv = ROPE_BASE ** (-jnp.arange(0, per_axis, 2, dtype=F32) / per_axis)
    ang = jnp.concatenate([rowf[:, None] * inv, colf[:, None] * inv], axis=-1)
    cos, sin = jnp.cos(ang), jnp.sin(ang)
    reps = LANE // GLA_DK
    return jnp.tile(cos, (1, 2 * reps)), jnp.tile(jnp.concatenate([-sin, sin], axis=-1), (1, reps))


def _pack_kernel(w_ref, o_ref, *, d_model, n_pad):
    sizes = (A_WIDTH, A_WIDTH, A_WIDTH, GLA_KD, GLA_KD, GLA_VD, GLA_VD, GLA_LOWRANK, GLA_LOWRANK, NA_D, NA_D, NA_D)
    start = dict(zip(("a_in", "a_b", "a_c", "gq", "gk", "gv", "gr", "gaf", "gab", "nq", "nk", "nv"),
                     np.concatenate([[0], np.cumsum(sizes)[:-1]]).tolist()))
    gate0 = sum(sizes)
    cols = w_ref.shape[2]
    pieces = [w_ref[0, gate0:gate0 + N_BRANCH * d_model, :],
              w_ref[0, 0:start["gq"], :],
              w_ref[0, start["gq"]:start["gk"], :] * (GLA_DK ** -0.5),
              w_ref[0, start["gk"]:start["nq"], :],
              jnp.zeros((LANE - 2 * GLA_LOWRANK, cols), F32),
              w_ref[0, start["nq"]:start["nk"], :] * (NA_DH ** -0.5),
              w_ref[0, start["nk"]:gate0, :]]
    used = sum(p.shape[0] for p in pieces)
    pieces.append(jnp.zeros((n_pad - used, cols), F32))
    o_ref[0] = jnp.concatenate(pieces, axis=0).astype(BF16)


def _pack_w_in(w_in, n_pad):
    depth, d, n_in = w_in.shape
    w_t = jnp.swapaxes(w_in, 1, 2)
    tc = LANE
    blocks = 2 * (_nbytes((n_in, tc), F32) + _nbytes((n_pad, tc), BF16)) + 2 * _nbytes((n_pad, tc), F32)
    return pl.pallas_call(
        functools.partial(_pack_kernel, d_model=d, n_pad=n_pad),
        grid=(depth, d // tc),
        in_specs=[pl.BlockSpec((1, n_in, tc), lambda l, i: (l, 0, i))],
        out_specs=pl.BlockSpec((1, n_pad, tc), lambda l, i: (l, 0, i)),
        out_shape=jax.ShapeDtypeStruct((depth, n_pad, d), BF16),
        compiler_params=_params(("arbitrary", "arbitrary"), blocks),
        name="pack_w_in",
    )(w_t)


def _pad_lowrank(wa, col0):
    return jnp.zeros((LANE, GLA_KD), BF16).at[col0:col0 + GLA_LOWRANK].set(wa.astype(BF16))


def kernel(x, c, ctx, c_ctx, w_ada, b_ada, g_mix, g_ffn, w_in, conv_w, gla_wa_f, gla_ba_f, gla_wa_b, gla_ba_b,
           gla_g_norm, na_rpb, w_a_out, w_g_out, w_n_out, w_o, w_ffn1, w_ffn3, w_ffn2, g_final):
    bsz, seq, d = x.shape
    ct = ctx.shape[1]
    depth = w_ada.shape[0]
    f = w_ffn1.shape[2]
    gla_block = GLA_CHUNK * GLA_CHUNKS_PER_STEP
    assert bsz + 1 <= MOD_ROWS and seq % GRID_W == 0 and seq % gla_block == 0 and ct % gla_block == 0
    assert (seq // GRID_W) % (NA_GROUP * NA_GROUPS_PER_STEP) == 0 and seq // GRID_W > NA_KEY_ROWS
    off, n_used = _layout(d)
    tn_in = 10 * LANE
    n_pad = -(-n_used // tn_in) * tn_in

    tm_big = _largest_tile(seq, 1024, SUBLANE)
    tm_ffn = _largest_tile(seq, 512, SUBLANE)
    tm_branch = _largest_tile(seq, 256, SUBLANE)
    tf = _largest_tile(f, 512, LANE)
    lat_group = lambda tm: (lambda i: i // (seq // tm))
    ctx_group = lambda i: bsz

    cvec = jnp.zeros((MOD_ROWS, d), F32).at[:bsz].set(c).at[bsz].set(c_ctx)
    mod = _adaln(cvec, w_ada, b_ada).reshape(depth, MOD_ROWS, 6, d)
    rope = _rope_tables(seq)
    zero_state = jnp.zeros((bsz, GLA_PAIRS, LANE, 2 * GLA_DV), F32)

    w_in_p = _pack_w_in(w_in, n_pad)
    wa_o, wg_o, wn_o = w_a_out.astype(BF16), w_g_out.astype(BF16), w_n_out.astype(BF16)
    wo, w1, w3, w2 = w_o.astype(BF16), w_ffn1.astype(BF16), w_ffn3.astype(BF16), w_ffn2.astype(BF16)
    bias_blocks = _na_bias_blocks(na_rpb)

    h = x.reshape(bsz * seq, d)
    hc = ctx.reshape(bsz * ct, d)
    for l in range(depth):
        need_ctx = l < depth - 1
        last = l == depth - 1
        mod_l = mod[l]
        gm = mod_l[:, 2, :].reshape(MOD_ROWS, 1, d)
        waf, wab = _pad_lowrank(gla_wa_f[l], 0), _pad_lowrank(gla_wa_b[l], GLA_LOWRANK)
        baf, bab = gla_ba_f[l].reshape(1, GLA_KD), gla_ba_b[l].reshape(1, GLA_KD)

        p = _inproj(h, mod_l, g_mix[l], w_in_p, l, lat_group(tm_big), tm_big, tn_in)
        pc = _inproj(hc, mod_l, g_mix[l], w_in_p, l, ctx_group, ct, tn_in)
        p3 = p.reshape(bsz, seq, n_pad)
        pc3 = pc.reshape(bsz, ct, n_pad)

        ogc_f, ogc_b, s_f, s_b = _gla(pc3, off, waf, baf, wab, bab, zero_state, zero_state, None)
        og_f, og_b, _, _ = _gla(p3, off, waf, baf, wab, bab, s_f, s_b, rope)
        na = _na(p3, pc3, off, bias_blocks, l)

        h = _branch(p, off, og_f.reshape(bsz * seq, GLA_VD), og_b.reshape(bsz * seq, GLA_VD),
                    na.reshape(bsz * seq, NA_D), conv_w[l], gla_g_norm[l], wa_o, wg_o, wn_o, wo, l,
                    h, gm, lat_group(tm_branch), seq, tm_branch)
        h = _ffn(h, mod_l, g_ffn[l], w1, w3, w2, l, g_final, lat_group(tm_ffn), tm_ffn, tf, last)
        if need_ctx:
            na_c = _ctx_attn(pc3, off)
            hc = _branch(pc, off, ogc_f.reshape(bsz * ct, GLA_VD), ogc_b.reshape(bsz * ct, GLA_VD),
                         na_c.reshape(bsz * ct, NA_D), conv_w[l], gla_g_norm[l], wa_o, wg_o, wn_o, wo, l,
                         hc, gm, ctx_group, ct, ct)
            hc = _ffn(hc, mod_l, g_ffn[l], w1, w3, w2, l, g_final, ctx_group, ct, tf, False)
    return h.reshape(bsz, seq, d)
```

```python
import functools

import numpy as np
import jax
import jax.numpy as jnp
from jax import lax
from jax.experimental import pallas as pl
from jax.experimental.pallas import tpu as pltpu

GRID_W = 64
RMS_EPS = 1e-6
ROPE_BASE = 10000.0
N_BRANCH = 3
A_WIDTH = 512
GLA_HEADS = 6
GLA_DK = 64
GLA_DV = 128
GLA_KD = GLA_HEADS * GLA_DK
GLA_VD = GLA_HEADS * GLA_DV
GLA_LOWRANK = 16
GLA_TAU = 16.0
GLA_CHUNK = 64
GLA_MAX_CHUNKS_PER_STEP = 8
GLA_SAFE_EXPONENT = 80.0
GLA_PAIRS = GLA_HEADS // 2
NA_HEADS = 6
NA_DH = 128
NA_D = NA_HEADS * NA_DH
NA_WIN_H = 8
NA_WIN_W = 16
NA_GROUP = 4
NA_KEY_ROWS = NA_GROUP + NA_WIN_H - 1
NA_GROUPS_PER_STEP = 16
MASK_VALUE = -1e30

LANE = 128
SUBLANE = 8
HALO_ROWS = 16
VMEM_PHYSICAL = 64 * 1024 * 1024
VMEM_INTERNAL = 12 * 1024 * 1024
MOD_ROWS = 8

BF16 = jnp.bfloat16
F32 = jnp.float32


def _params(semantics, block_bytes):
    limit = min(int(block_bytes) + VMEM_INTERNAL, VMEM_PHYSICAL - 4 * 1024 * 1024)
    return pltpu.CompilerParams(dimension_semantics=semantics, vmem_limit_bytes=limit)


def _nbytes(shape, dtype):
    return int(np.prod(shape)) * jnp.dtype(dtype).itemsize


def _sigmoid(x):
    return 0.5 * jnp.tanh(0.5 * x) + 0.5


def _serpentine(i, j, n):
    return jnp.where(i % 2 == 0, j, n - 1 - j)


def _layout(d_model):
    off = {}
    pos = 0
    for name, width, block in (("gate", N_BRANCH * d_model, N_BRANCH * d_model),
                               ("a_in", A_WIDTH, A_WIDTH), ("a_b", A_WIDTH, A_WIDTH), ("a_c", A_WIDTH, A_WIDTH),
                               ("gq", GLA_KD, GLA_KD), ("gk", GLA_KD, GLA_KD), ("gv", GLA_VD, GLA_VD),
                               ("gr", GLA_VD, GLA_VD), ("gab", LANE, LANE),
                               ("nq", NA_D, NA_DH), ("nk", NA_D, NA_DH), ("nv", NA_D, NA_DH)):
        assert pos % block == 0, (name, pos, block)
        off[name] = pos
        pos += width
    return off, pos


def _adaln_kernel(c_ref, w_ref, b_ref, o_ref):
    cv = c_ref[...]
    s = cv * jax.nn.sigmoid(cv)
    o_ref[0] = jnp.dot(s, w_ref[0], preferred_element_type=F32) + b_ref[0]


def _adaln(cvec, w_ada, b_ada):
    depth, d, n = w_ada.shape
    tn = 1024 if n % 1024 == 0 else n
    blocks = 2 * (_nbytes((d, tn), F32) + 2 * _nbytes((MOD_ROWS, tn), F32)) + _nbytes((MOD_ROWS, d), F32)
    return pl.pallas_call(
        _adaln_kernel,
        grid=(depth, n // tn),
        in_specs=[pl.BlockSpec((MOD_ROWS, d), lambda l, j: (0, 0)),
                  pl.BlockSpec((1, d, tn), lambda l, j: (l, 0, j)),
                  pl.BlockSpec((1, 1, tn), lambda l, j: (l, 0, j))],
        out_specs=pl.BlockSpec((1, MOD_ROWS, tn), lambda l, j: (l, 0, j)),
        out_shape=jax.ShapeDtypeStruct((depth, MOD_ROWS, n), F32),
        compiler_params=_params(("arbitrary", "arbitrary"), blocks),
        name="adaln",
    )(cvec, w_ada, b_ada.reshape(depth, 1, n))


def _mod_norm(x, g, shift, scale):
    y = x * lax.rsqrt(jnp.mean(x * x, axis=-1, keepdims=True) + RMS_EPS)
    return (y * g) * (1.0 + scale) + shift


def _inproj_kernel(x_ref, mod_ref, g_ref, w_ref, o_ref, n_scr):
    def project(n):
        o_ref[...] = lax.dot_general(n, w_ref[...], (((1,), (1,)), ((), ())),
                                     preferred_element_type=F32).astype(o_ref.dtype)

    @pl.when(pl.program_id(1) == 0)
    def _():
        m = mod_ref[0]
        n = _mod_norm(x_ref[...], g_ref[...], m[0:1], m[1:2]).astype(BF16)
        n_scr[...] = n
        project(n)

    @pl.when(pl.program_id(1) != 0)
    def _():
        project(n_scr[...])


def _inproj(h, mod_l, g, w, layer, group_of_tile, tm, tn):
    m_rows, d = h.shape
    n = w.shape[1]
    blocks = (2 * (_nbytes((tm, d), F32) + _nbytes((d, tn), BF16) + _nbytes((tm, tn), BF16))
              + _nbytes((tm, d), BF16) + 2 * _nbytes((8, d), F32) + _nbytes((tm, tn), F32))
    return pl.pallas_call(
        _inproj_kernel,
        grid=(m_rows // tm, n // tn),
        in_specs=[pl.BlockSpec((tm, d), lambda i, j: (i, 0)),
                  pl.BlockSpec((1, 6, d), lambda i, j: (group_of_tile(i), 0, 0)),
                  pl.BlockSpec((1, d), lambda i, j: (0, 0)),
                  pl.BlockSpec((None, tn, d), lambda i, j: (layer, _serpentine(i, j, n // tn), 0))],
        out_specs=pl.BlockSpec((tm, tn), lambda i, j: (i, _serpentine(i, j, n // tn))),
        out_shape=jax.ShapeDtypeStruct((m_rows, n), BF16),
        scratch_shapes=[pltpu.VMEM((tm, d), BF16)],
        compiler_params=_params(("parallel", "arbitrary"), blocks),
        name="inproj",
    )(h, mod_l, g.reshape(1, d), w)


def _rope_pairs(x, cos, sin):
    lane = lax.broadcasted_iota(jnp.int32, cos.shape, 1)
    first_half = (lane % GLA_DK) < (GLA_DK // 2)
    out = []
    for p in range(GLA_PAIRS):
        xs = x[:, p * LANE:(p + 1) * LANE]
        swapped = jnp.where(first_half, pltpu.roll(xs, LANE - GLA_DK // 2, axis=1), pltpu.roll(xs, GLA_DK // 2, axis=1))
        out.append(xs * cos + swapped * sin)
    return jnp.concatenate(out, axis=1)


def _minus_chunk_row(x, offset, sign=1.0):
    parts = []
    for c in range(x.shape[0] // GLA_CHUNK):
        chunk = x[c * GLA_CHUNK:(c + 1) * GLA_CHUNK]
        ref_row = x[c * GLA_CHUNK + offset:c * GLA_CHUNK + offset + 1]
        parts.append(chunk - ref_row if sign > 0 else ref_row - chunk)
    return jnp.concatenate(parts, axis=0)


def _gla_prepare(q, k, ab, wa, ba, cos, sin, reverse):
    rows = q.shape[0]
    z = jnp.dot(ab, wa, preferred_element_type=F32) + ba
    la = (jnp.minimum(z, 0.0) - jnp.log(1.0 + jnp.exp(-jnp.abs(z)))) * (1.0 / GLA_TAU)
    r = lax.broadcasted_iota(jnp.int32, (rows, rows), 0)
    c = lax.broadcasted_iota(jnp.int32, (rows, rows), 1)
    tri = ((r // GLA_CHUNK == c // GLA_CHUNK) & ((c >= r) if reverse else (c <= r))).astype(BF16)
    la_hi = la.astype(BF16)
    la_lo = (la - la_hi.astype(F32)).astype(BF16)
    b = jnp.dot(tri, la_hi, preferred_element_type=F32) + jnp.dot(tri, la_lo, preferred_element_type=F32)
    q = q.astype(F32)
    k = k.astype(F32)
    if cos is not None:
        q = _rope_pairs(q, cos, sin)
        k = _rope_pairs(k, cos, sin)
    return q, k, b


def _gla_block(q, k, b, v, s_scr, o_ref, inter_scr, reverse):
    rows = q.shape[0]
    n_chunks = rows // GLA_CHUNK
    half = GLA_CHUNK // 2
    last_row = 0 if reverse else GLA_CHUNK - 1
    rel = _minus_chunk_row(b, half if reverse else half - 1)
    q_mid = (q * jnp.exp(rel))
    k_mid = (k * jnp.exp(-rel)).astype(BF16)
    q_state = (q * jnp.exp(b)).astype(BF16)
    k_last = (k * jnp.exp(_minus_chunk_row(b, last_row, sign=-1.0))).astype(BF16)
    row = lax.broadcasted_iota(jnp.int32, (2 * GLA_CHUNK, GLA_CHUNK), 0) % GLA_CHUNK
    col = lax.broadcasted_iota(jnp.int32, (2 * GLA_CHUNK, GLA_CHUNK), 1)
    visible = (row <= col) if reverse else (row >= col)
    first_head = lax.broadcasted_iota(jnp.int32, (rows, LANE), 1) < GLA_DK
    blk_r = lax.broadcasted_iota(jnp.int32, (LANE, 2 * GLA_DV), 0) < GLA_DK
    blk_c = lax.broadcasted_iota(jnp.int32, (LANE, 2 * GLA_DV), 1) < GLA_DV
    same_head = blk_r == blk_c
    nt = (((1,), (1,)), ((), ()))
    tn = (((0,), (0,)), ((), ()))
    q_first = [jnp.where(first_head, q_mid[:, p * LANE:(p + 1) * LANE], 0.0).astype(BF16) for p in range(GLA_PAIRS)]
    q_second = [jnp.where(first_head, 0.0, q_mid[:, p * LANE:(p + 1) * LANE]).astype(BF16) for p in range(GLA_PAIRS)]
    order = range(n_chunks - 1, -1, -1) if reverse else range(n_chunks)
    for c in order:
        rs = slice(c * GLA_CHUNK, (c + 1) * GLA_CHUNK)
        intra, inter = [], []
        for p in range(GLA_PAIRS):
            sl = slice(p * LANE, (p + 1) * LANE)
            q_stack = jnp.concatenate([q_first[p][rs], q_second[p][rs]], axis=0)
            att = lax.dot_general(q_stack, k_mid[rs, sl], nt, preferred_element_type=F32)
            att = jnp.where(visible, att, 0.0).astype(BF16)
            v_pair = v[rs, 2 * p * GLA_DV:(2 * p + 2) * GLA_DV]
            intra.append(jnp.dot(att[:GLA_CHUNK], v_pair[:, :GLA_DV], preferred_element_type=F32))
            intra.append(jnp.dot(att[GLA_CHUNK:], v_pair[:, GLA_DV:], preferred_element_type=F32))
            s_p = s_scr[p]
            inter.append(jnp.dot(q_state[rs, sl], s_p.astype(BF16), preferred_element_type=F32))
            update = lax.dot_general(k_last[rs, sl], v_pair, tn, preferred_element_type=F32)
            decay_row = jnp.exp(b[c * GLA_CHUNK + last_row:c * GLA_CHUNK + last_row + 1, sl])
            decay_col = jnp.transpose(jnp.broadcast_to(decay_row, (SUBLANE, LANE)))[:, 0:1]
            s_scr[p] = s_p * decay_col + jnp.where(same_head, update, 0.0)
        inter = jnp.concatenate(inter, axis=1)
        o_ref[0, rs, :] = jnp.concatenate(intra, axis=1) + inter
        inter_scr[rs, :] = inter
    return jnp.max(jnp.abs(rel))


def _gla_chunk_exact(q_scr, k_scr, b_scr, v_scr, o_ref, row0, reverse):
    length = GLA_CHUNK
    rows = pl.ds(row0, length)
    q = q_scr[rows, :]
    b = b_scr[rows, :]
    row = lax.broadcasted_iota(jnp.int32, (length, 1), 0)
    first_head = lax.broadcasted_iota(jnp.int32, (length, LANE), 1) < GLA_DK

    def body(s, carry):
        src = pl.ds(row0 + s, 1)
        w = q * k_scr[src, :] * jnp.exp(jnp.minimum(b - b_scr[src, :], 0.0))
        w = jnp.where((row <= s) if reverse else (row >= s), w, 0.0)
        v_row = v_scr[src, :]
        for p in range(GLA_PAIRS):
            wp = w[:, p * LANE:(p + 1) * LANE]
            for hh in range(2):
                a = jnp.sum(jnp.where(first_head == (hh == 0), wp, 0.0), axis=1, keepdims=True)
                lanes = slice((2 * p + hh) * GLA_DV, (2 * p + hh + 1) * GLA_DV)
                o_ref[0, rows, lanes] += a * v_row[:, lanes]
        return carry

    lax.fori_loop(0, length, body, 0)


def _gla_kernel(*refs, use_rope):
    qf, kf, vf, abf, qb, kb, vb, abb = refs[:8]
    pos = 8
    if use_rope:
        cosf, sinf, cosb, sinb = refs[pos:pos + 4]
        pos += 4
    (waf, baf, wab, bab, s0f, s0b, of_ref, ob_ref, sf_ref, sb_ref,
     sf_scr, sb_scr, inter_f, inter_b, q_scr, k_scr, b_scr, v_scr) = refs[pos:]
    step = pl.program_id(1)

    @pl.when(step == 0)
    def _():
        sf_scr[...] = s0f[0]
        sb_scr[...] = s0b[0]

    directions = ((False, (qf, kf, vf, abf), (cosf, sinf) if use_rope else None, waf, baf, sf_scr, of_ref, inter_f),
                  (True, (qb, kb, vb, abb), (cosb, sinb) if use_rope else None, wab, bab, sb_scr, ob_ref, inter_b))

    def prepared(reverse, refs4, rope_refs, w_ref, bias_ref):
        q_ref, k_ref, _, ab_ref = refs4
        cos = rope_refs[0][...] if use_rope else None
        sin = rope_refs[1][...] if use_rope else None
        return _gla_prepare(q_ref[0], k_ref[0], ab_ref[0], w_ref[...], bias_ref[...], cos, sin, reverse)

    margin = jnp.float32(0.0)
    for reverse, refs4, rope_refs, w_ref, bias_ref, s_scr, o_ref, inter_scr in directions:
        q, k, b = prepared(reverse, refs4, rope_refs, w_ref, bias_ref)
        m = _gla_block(q, k, b, refs4[2][0], s_scr, o_ref, inter_scr, reverse)
        margin = jnp.maximum(margin, m)

    @pl.when(margin > GLA_SAFE_EXPONENT)
    def _():
        for reverse, refs4, rope_refs, w_ref, bias_ref, s_scr, o_ref, inter_scr in directions:
            q, k, b = prepared(reverse, refs4, rope_refs, w_ref, bias_ref)
            q_scr[...] = q
            k_scr[...] = k
            b_scr[...] = b
            v_scr[...] = refs4[2][0].astype(F32)
            o_ref[0] = inter_scr[...]
            for c in range(o_ref.shape[1] // GLA_CHUNK):
                _gla_chunk_exact(q_scr, k_scr, b_scr, v_scr, o_ref, c * GLA_CHUNK, reverse)

    @pl.when(step == pl.num_programs(1) - 1)
    def _():
        sf_ref[0] = sf_scr[...]
        sb_ref[0] = sb_scr[...]


def _gla(p3, off, waf, baf, wab, bab, s0f, s0b, rope):
    bsz, t, _ = p3.shape
    length = GLA_CHUNK * _largest_tile(t // GLA_CHUNK, GLA_MAX_CHUNKS_PER_STEP, 1)
    n = t // length
    fwd = lambda blk: (lambda b, i: (b, i, blk))
    bwd = lambda blk: (lambda b, i: (b, n - 1 - i, blk))
    q_blk, k_blk = off["gq"] // GLA_KD, off["gk"] // GLA_KD
    v_blk, ab_blk = off["gv"] // GLA_VD, off["gab"] // LANE
    seq_specs = lambda m: [pl.BlockSpec((1, length, GLA_KD), m(q_blk)), pl.BlockSpec((1, length, GLA_KD), m(k_blk)),
                           pl.BlockSpec((1, length, GLA_VD), m(v_blk)), pl.BlockSpec((1, length, LANE), m(ab_blk))]
    in_specs = seq_specs(fwd) + seq_specs(bwd)
    args = [p3] * 8
    if rope is not None:
        in_specs += [pl.BlockSpec((length, LANE), lambda b, i: (i, 0))] * 2
        in_specs += [pl.BlockSpec((length, LANE), lambda b, i: (n - 1 - i, 0))] * 2
        args += [rope[0], rope[1], rope[0], rope[1]]
    const2 = lambda b, i: (0, 0)
    state_dims = (GLA_PAIRS, LANE, 2 * GLA_DV)
    state_spec = pl.BlockSpec((1,) + state_dims, lambda b, i: (b, 0, 0, 0))
    in_specs += [pl.BlockSpec((LANE, GLA_KD), const2), pl.BlockSpec((1, GLA_KD), const2),
                 pl.BlockSpec((LANE, GLA_KD), const2), pl.BlockSpec((1, GLA_KD), const2),
                 state_spec, state_spec]
    args += [waf, baf, wab, bab, s0f, s0b]
    state_shape = jax.ShapeDtypeStruct((bsz,) + state_dims, F32)
    out_shape = jax.ShapeDtypeStruct((bsz, t, GLA_VD), F32)
    blocks = 2 * (2 * (2 * _nbytes((length, GLA_KD), F32) + 2 * _nbytes((length, GLA_VD), F32) + 5 * _nbytes((length, LANE), F32))
                  + 4 * _nbytes(state_dims, F32) + 2 * _nbytes((LANE, GLA_KD), F32))
    blocks += 2 * _nbytes(state_dims, F32) + 2 * _nbytes((length, GLA_VD), F32)
    return pl.pallas_call(
        functools.partial(_gla_kernel, use_rope=rope is not None),
        grid=(bsz, n),
        in_specs=in_specs,
        out_specs=[pl.BlockSpec((1, length, GLA_VD), fwd(0)), pl.BlockSpec((1, length, GLA_VD), bwd(0)),
                   state_spec, state_spec],
        out_shape=[out_shape, out_shape, state_shape, state_shape],
        scratch_shapes=[pltpu.VMEM(state_dims, F32), pltpu.VMEM(state_dims, F32),
                        pltpu.VMEM((length, GLA_VD), F32), pltpu.VMEM((length, GLA_VD), F32),
                        pltpu.VMEM((length, GLA_KD), F32), pltpu.VMEM((length, GLA_KD), F32),
                        pltpu.VMEM((length, GLA_KD), F32), pltpu.VMEM((length, GLA_VD), F32)],
        compiler_params=_params(("arbitrary", "arbitrary"), blocks),
        name="gla",
    )(*args)


def _na_kernel(q_ref, k_ref, v_ref, kc_ref, vc_ref, blk_ref, o_ref, tab_scr, *, rows):
    nt = (((1,), (1,)), ((), ()))
    step = pl.program_id(2)

    @pl.when(step == 0)
    def _():
        masked = jnp.full((GRID_W, GRID_W), MASK_VALUE, F32)
        for variant, per_query in enumerate(_na_row_offsets(rows)):
            for rq, offsets in enumerate(per_query):
                tab_scr[variant, rq * GRID_W:(rq + 1) * GRID_W, :] = jnp.concatenate(
                    [masked if r is None else blk_ref[0, r] for r in offsets], axis=1)

    kc = kc_ref[0]
    vc = vc_ref[0]
    nq = NA_GROUP * GRID_W
    nk = NA_KEY_ROWS * GRID_W
    for g in range(NA_GROUPS_PER_STEP):
        r0 = (step * NA_GROUPS_PER_STEP + g) * NA_GROUP
        key_row0 = jnp.clip(r0 - NA_WIN_H // 2, 0, rows - NA_KEY_ROWS)
        k_off = pl.multiple_of(key_row0 * GRID_W, GRID_W)
        variant = jnp.where(r0 == 0, 0, jnp.where(r0 == rows - NA_GROUP, 2, 1))
        q = q_ref[0, g * nq:(g + 1) * nq, :]
        kr = k_ref[0, pl.ds(k_off, nk), :]
        vr = v_ref[0, pl.ds(k_off, nk), :]
        s_loc = lax.dot_general(q, kr, nt, preferred_element_type=F32) + tab_scr[variant]
        s_ctx = lax.dot_general(q, kc, nt, preferred_element_type=F32)
        m = jnp.maximum(jnp.max(s_loc, axis=-1, keepdims=True), jnp.max(s_ctx, axis=-1, keepdims=True))
        p_loc = jnp.exp(s_loc - m)
        p_ctx = jnp.exp(s_ctx - m)
        denom = jnp.sum(p_loc, axis=-1, keepdims=True) + jnp.sum(p_ctx, axis=-1, keepdims=True)
        o = (jnp.dot(p_loc.astype(BF16), vr, preferred_element_type=F32)
             + jnp.dot(p_ctx.astype(BF16), vc, preferred_element_type=F32))
        o_ref[0, g * nq:(g + 1) * nq, :] = (o / denom).astype(o_ref.dtype)


def _na(p3, pc3, off, bias_blocks, layer):
    bsz, t, _ = p3.shape
    ct = pc3.shape[1]
    rows = t // GRID_W
    tq = NA_GROUPS_PER_STEP * NA_GROUP * GRID_W
    nq, nk = NA_GROUP * GRID_W, NA_KEY_ROWS * GRID_W
    n_off = 2 * NA_WIN_H - 1
    q0, k0, v0 = off["nq"] // NA_DH, off["nk"] // NA_DH, off["nv"] // NA_DH
    blocks = (2 * (2 * _nbytes((tq, NA_DH), F32) + 2 * _nbytes((t, NA_DH), BF16) + 2 * _nbytes((ct, NA_DH), BF16)
                   + _nbytes((n_off, GRID_W, LANE), F32)) + _nbytes((3, nq, nk), F32)
              + 3 * NA_GROUPS_PER_STEP * _nbytes((nq, nk + ct), F32))
    return pl.pallas_call(
        functools.partial(_na_kernel, rows=rows),
        grid=(bsz, NA_HEADS, rows // (NA_GROUPS_PER_STEP * NA_GROUP)),
        in_specs=[pl.BlockSpec((1, tq, NA_DH), lambda b, h, r: (b, r, q0 + h)),
                  pl.BlockSpec((1, t, NA_DH), lambda b, h, r: (b, 0, k0 + h)),
                  pl.BlockSpec((1, t, NA_DH), lambda b, h, r: (b, 0, v0 + h)),
                  pl.BlockSpec((1, ct, NA_DH), lambda b, h, r: (b, 0, k0 + h)),
                  pl.BlockSpec((1, ct, NA_DH), lambda b, h, r: (b, 0, v0 + h)),
                  pl.BlockSpec((None, 1, n_off, GRID_W, GRID_W), lambda b, h, r: (layer, h, 0, 0, 0))],
        out_specs=pl.BlockSpec((1, tq, NA_DH), lambda b, h, r: (b, r, h)),
        out_shape=jax.ShapeDtypeStruct((bsz, t, NA_D), BF16),
        scratch_shapes=[pltpu.VMEM((3, nq, nk), F32)],
        compiler_params=_params(("arbitrary", "arbitrary", "arbitrary"), blocks),
        name="natten",
    )(p3, p3, p3, pc3, pc3, bias_blocks)


def _na_row_offsets(rows):
    table = []
    for r0 in (0, NA_GROUP, rows - NA_GROUP):
        key_row0 = int(np.clip(r0 - NA_WIN_H // 2, 0, rows - NA_KEY_ROWS))
        per_query = []
        for rq in range(NA_GROUP):
            r = r0 + rq
            r_start = int(np.clip(r - NA_WIN_H // 2, 0, rows - NA_WIN_H))
            per_query.append([key_row0 + ki - r + (NA_WIN_H - 1) if r_start <= key_row0 + ki < r_start + NA_WIN_H
                              else None for ki in range(NA_KEY_ROWS)])
        table.append(per_query)
    return table


def _na_bias_blocks(rpb):
    w = GRID_W
    col = np.arange(w)
    col_start = np.clip(col - NA_WIN_W // 2, 0, w - NA_WIN_W)
    valid_col = (col[None, :] >= col_start[:, None]) & (col[None, :] < col_start[:, None] + NA_WIN_W)
    col_off = col[None, :] - col[:, None] + (NA_WIN_W - 1)
    col_sel = np.zeros((2 * NA_WIN_W - 1, w, w), np.float32)
    cc, jj = np.nonzero(valid_col)
    col_sel[col_off[cc, jj], cc, jj] = 1.0
    blocks = jnp.einsum("lhrx,xcj->lhrcj", rpb.astype(F32), col_sel, precision=lax.Precision.HIGHEST)
    return jnp.where(valid_col, blocks, MASK_VALUE)


def _ctx_attn_kernel(q_ref, k_ref, v_ref, o_ref):
    s = lax.dot_general(q_ref[0], k_ref[0], (((1,), (1,)), ((), ())), preferred_element_type=F32)
    p = jnp.exp(s - jnp.max(s, axis=-1, keepdims=True))
    o = jnp.dot(p.astype(BF16), v_ref[0], preferred_element_type=F32)
    o_ref[0] = (o / jnp.sum(p, axis=-1, keepdims=True)).astype(o_ref.dtype)


def _ctx_attn(pc3, off):
    bsz, ct, _ = pc3.shape
    q0, k0, v0 = off["nq"] // NA_DH, off["nk"] // NA_DH, off["nv"] // NA_DH
    spec = lambda c0: pl.BlockSpec((1, ct, NA_DH), lambda b, h: (b, 0, c0 + h))
    return pl.pallas_call(
        _ctx_attn_kernel,
        grid=(bsz, NA_HEADS),
        in_specs=[spec(q0), spec(k0), spec(v0)],
        out_specs=spec(0),
        out_shape=jax.ShapeDtypeStruct((bsz, ct, NA_D), BF16),
        compiler_params=_params(("arbitrary", "arbitrary"), 8 * _nbytes((ct, NA_DH), F32)),
        name="ctx_attn",
    )(pc3, pc3, pc3)


def _branch_kernel(gate_ref, ain_ref, ab_ref, ac_ref, ainp_ref, acp_ref, ainn_ref, acn_ref,
                   of_ref, ob_ref, gr_ref, na_ref, cw_ref, gn_ref, wa_ref, wg_ref, wn_ref, wo_ref, h_ref, gm_ref,
                   o_ref, *, tiles_per_seq):
    tm = ain_ref.shape[0]
    d = o_ref.shape[1]
    i = pl.program_id(0)
    tile_in_seq = i % tiles_per_seq
    u = ac_ref[...].astype(F32) * ain_ref[...].astype(F32)
    u_before = jnp.where(tile_in_seq == 0, 0.0, (acp_ref[...].astype(F32) * ainp_ref[...].astype(F32))[HALO_ROWS - 1:HALO_ROWS])
    u_after = jnp.where(tile_in_seq == tiles_per_seq - 1, 0.0, (acn_ref[...].astype(F32) * ainn_ref[...].astype(F32))[0:1])
    row = lax.broadcasted_iota(jnp.int32, u.shape, 0)
    u_prev = jnp.where(row == 0, u_before, pltpu.roll(u, 1, axis=0))
    u_next = jnp.where(row == tm - 1, u_after, pltpu.roll(u, tm - 1, axis=0))
    cw = cw_ref[...]
    a = ab_ref[...].astype(F32) * (cw[0:1] * u_prev + cw[1:2] * u + cw[2:3] * u_next)
    o = of_ref[...] + ob_ref[...]
    gn = gn_ref[...]
    normed = []
    for h in range(GLA_HEADS):
        oh = o[:, h * GLA_DV:(h + 1) * GLA_DV]
        normed.append(oh * lax.rsqrt(jnp.mean(oh * oh, axis=-1, keepdims=True) + RMS_EPS) * gn)
    r = gr_ref[...].astype(F32)
    g = jnp.concatenate(normed, axis=1) * (r * _sigmoid(r))
    y = _sigmoid(gate_ref[:, 0:d].astype(F32)) * jnp.dot(a.astype(BF16), wa_ref[...], preferred_element_type=F32)
    y = y + _sigmoid(gate_ref[:, d:2 * d].astype(F32)) * jnp.dot(g.astype(BF16), wg_ref[...], preferred_element_type=F32)
    y = y + _sigmoid(gate_ref[:, 2 * d:3 * d].astype(F32)) * jnp.dot(na_ref[...], wn_ref[...],
                                                                    preferred_element_type=F32)
    o_ref[...] = h_ref[...] + gm_ref[0] * jnp.dot(y.astype(BF16), wo_ref[...], preferred_element_type=F32)


def _branch(p, off, og_f, og_b, na, conv_w, g_norm, wa, wg, wn, wo, layer, h, gate, group_of_tile, seq_len, tm):
    m_rows = p.shape[0]
    d = wa.shape[2]
    tiles_per_seq = seq_len // tm
    halo = tm // HALO_ROWS
    n_halo = m_rows // HALO_ROWS
    cur = lambda blk: (lambda i: (i, blk))
    prev = lambda blk: (lambda i: (jnp.maximum(i * halo - 1, 0), blk))
    nxt = lambda blk: (lambda i: (jnp.minimum((i + 1) * halo, n_halo - 1), blk))
    a_in, a_b, a_c = off["a_in"] // A_WIDTH, off["a_b"] // A_WIDTH, off["a_c"] // A_WIDTH
    const = lambda i: (0, 0)
    lay = lambda i: (layer, 0, 0)
    in_specs = [pl.BlockSpec((tm, N_BRANCH * d), cur(off["gate"] // (N_BRANCH * d))),
                pl.BlockSpec((tm, A_WIDTH), cur(a_in)), pl.BlockSpec((tm, A_WIDTH), cur(a_b)),
                pl.BlockSpec((tm, A_WIDTH), cur(a_c)),
                pl.BlockSpec((HALO_ROWS, A_WIDTH), prev(a_in)), pl.BlockSpec((HALO_ROWS, A_WIDTH), prev(a_c)),
                pl.BlockSpec((HALO_ROWS, A_WIDTH), nxt(a_in)), pl.BlockSpec((HALO_ROWS, A_WIDTH), nxt(a_c)),
                pl.BlockSpec((tm, GLA_VD), cur(0)), pl.BlockSpec((tm, GLA_VD), cur(0)),
                pl.BlockSpec((tm, GLA_VD), cur(off["gr"] // GLA_VD)),
                pl.BlockSpec((tm, NA_D), cur(0)),
                pl.BlockSpec((3, A_WIDTH), const), pl.BlockSpec((1, GLA_DV), const),
                pl.BlockSpec((None, A_WIDTH, d), lay, pipeline_mode=pl.Buffered(1)),
                pl.BlockSpec((None, GLA_VD, d), lay, pipeline_mode=pl.Buffered(1)),
                pl.BlockSpec((None, NA_D, d), lay, pipeline_mode=pl.Buffered(1)),
                pl.BlockSpec((None, d, d), lay, pipeline_mode=pl.Buffered(1)),
                pl.BlockSpec((tm, d), cur(0)),
                pl.BlockSpec((1, 1, d), lambda i: (group_of_tile(i), 0, 0))]
    blocks = (2 * (_nbytes((tm, N_BRANCH * d), BF16) + 3 * _nbytes((tm, A_WIDTH), BF16) + 4 * _nbytes((tm, GLA_VD), F32)
                   + 2 * _nbytes((tm, d), F32)) + _nbytes((A_WIDTH + GLA_VD + NA_D + d, d), BF16) + 6 * _nbytes((tm, d), F32))
    return pl.pallas_call(
        functools.partial(_branch_kernel, tiles_per_seq=tiles_per_seq),
        grid=(m_rows // tm,),
        in_specs=in_specs,
        out_specs=pl.BlockSpec((tm, d), cur(0)),
        out_shape=jax.ShapeDtypeStruct((m_rows, d), F32),
        compiler_params=_params(("arbitrary",), blocks),
        name="branch_merge",
    )(p, p, p, p, p, p, p, p, og_f, og_b, p, na, conv_w, g_norm.reshape(1, GLA_DV), wa, wg, wn, wo, h, gate)


def _ffn_kernel(x_ref, mod_ref, g_ref, w1_ref, w3_ref, w2_ref, gfin_ref, o_ref, n_scr, acc_scr, *, final_norm):
    j = pl.program_id(1)
    last = pl.num_programs(1) - 1

    def hidden_slice(n):
        a = jnp.dot(n, w1_ref[...], preferred_element_type=F32)
        b = jnp.dot(n, w3_ref[...], preferred_element_type=F32)
        u = (a * _sigmoid(a)) * b
        return jnp.dot(u.astype(BF16), w2_ref[...], preferred_element_type=F32)

    @pl.when(j == 0)
    def _():
        m = mod_ref[0]
        n = _mod_norm(x_ref[...], g_ref[...], m[3:4], m[4:5]).astype(BF16)
        n_scr[...] = n
        acc_scr[...] = hidden_slice(n)

    @pl.when((j > 0) & (j < last))
    def _():
        acc_scr[...] += hidden_slice(n_scr[...])

    @pl.when(j == last)
    def _():
        out = x_ref[...] + mod_ref[0][5:6] * (acc_scr[...] + hidden_slice(n_scr[...]))
        if final_norm:
            out = out * lax.rsqrt(jnp.mean(out * out, axis=-1, keepdims=True) + RMS_EPS) * gfin_ref[...]
        o_ref[...] = out


def _ffn(h, mod_l, g, w1, w3, w2, layer, g_final, group_of_tile, tm, tf, final_norm):
    m_rows, d = h.shape
    f = w1.shape[2]
    blocks = (4 * _nbytes((tm, d), F32) + 6 * _nbytes((d, tf), BF16) + _nbytes((tm, d), BF16) + _nbytes((tm, d), F32)
              + 4 * _nbytes((tm, tf), F32))
    return pl.pallas_call(
        functools.partial(_ffn_kernel, final_norm=final_norm),
        grid=(m_rows // tm, f // tf),
        in_specs=[pl.BlockSpec((tm, d), lambda i, j: (i, 0)),
                  pl.BlockSpec((1, 6, d), lambda i, j: (group_of_tile(i), 0, 0)),
                  pl.BlockSpec((1, d), lambda i, j: (0, 0)),
                  pl.BlockSpec((None, d, tf), lambda i, j: (layer, 0, _serpentine(i, j, f // tf))),
                  pl.BlockSpec((None, d, tf), lambda i, j: (layer, 0, _serpentine(i, j, f // tf))),
                  pl.BlockSpec((None, tf, d), lambda i, j: (layer, _serpentine(i, j, f // tf), 0)),
                  pl.BlockSpec((1, d), lambda i, j: (0, 0))],
        out_specs=pl.BlockSpec((tm, d), lambda i, j: (i, 0)),
        out_shape=jax.ShapeDtypeStruct((m_rows, d), F32),
        scratch_shapes=[pltpu.VMEM((tm, d), BF16), pltpu.VMEM((tm, d), F32)],
        compiler_params=_params(("parallel", "arbitrary"), blocks),
        name="ffn",
    )(h, mod_l, g.reshape(1, d), w1, w3, w2, g_final.reshape(1, d))


def _largest_tile(total, cap, align):
    t = min(total, cap)
    while total % t or t % align:
        t -= align
    return t


def _rope_tables(seq_len):
    pos = jnp.arange(seq_len, dtype=jnp.int32)
    rowf = (pos // GRID_W).astype(F32)
    colf = (pos % GRID_W).astype(F32)
    per_axis = GLA_DK // 2
    inv = ROPE_BASE ** (-jnp.arange(0, per_axis, 2, dtype=F32) / per_axis)
    ang = jnp.concatenate([rowf[:, None] * inv, colf[:, None] * inv], axis=-1)
    cos, sin = jnp.cos(ang), jnp.sin(ang)
    reps = LANE // GLA_DK
    return jnp.tile(cos, (1, 2 * reps)), jnp.tile(jnp.concatenate([-sin, sin], axis=-1), (1, reps))


def _pack_kernel(w_ref, o_ref, *, d_model, n_pad):
    sizes = (A_WIDTH, A_WIDTH, A_WIDTH, GLA_KD, GLA_KD, GLA_VD, GLA_VD, GLA_LOWRANK, GLA_LOWRANK, NA_D, NA_D, NA_D)
    start = dict(zip(("a_in", "a_b", "a_c", "gq", "gk", "gv", "gr", "gaf", "gab", "nq", "nk", "nv"),
                     np.concatenate([[0], np.cumsum(sizes)[:-1]]).tolist()))
    gate0 = sum(sizes)
    cols = w_ref.shape[2]
    pieces = [w_ref[0, gate0:gate0 + N_BRANCH * d_model, :],
              w_ref[0, 0:start["gq"], :],
              w_ref[0, start["gq"]:start["gk"], :] * (GLA_DK ** -0.5),
              w_ref[0, start["gk"]:start["nq"], :],
              jnp.zeros((LANE - 2 * GLA_LOWRANK, cols), F32),
              w_ref[0, start["nq"]:start["nk"], :] * (NA_DH ** -0.5),
              w_ref[0, start["nk"]:gate0, :]]
    used = sum(p.shape[0] for p in pieces)
    pieces.append(jnp.zeros((n_pad - used, cols), F32))
    o_ref[0] = jnp.concatenate(pieces, axis=0).astype(BF16)


def _pack_w_in(w_in, n_pad):
    depth, d, n_in = w_in.shape
    w_t = jnp.swapaxes(w_in, 1, 2)
    tc = LANE
    blocks = 2 * (_nbytes((n_in, tc), F32) + _nbytes((n_pad, tc), BF16)) + 2 * _nbytes((n_pad, tc), F32)
    return pl.pallas_call(
        functools.partial(_pack_kernel, d_model=d, n_pad=n_pad),
        grid=(depth, d // tc),
        in_specs=[pl.BlockSpec((1, n_in, tc), lambda l, i: (l, 0, i))],
        out_specs=pl.BlockSpec((1, n_pad, tc), lambda l, i: (l, 0, i)),
        out_shape=jax.ShapeDtypeStruct((depth, n_pad, d), BF16),
        compiler_params=_params(("arbitrary", "arbitrary"), blocks),
        name="pack_w_in",
    )(w_t)


def _pad_lowrank(wa, col0):
    return jnp.zeros((LANE, GLA_KD), BF16).at[col0:col0 + GLA_LOWRANK].set(wa.astype(BF16))


def kernel(x, c, ctx, c_ctx, w_ada, b_ada, g_mix, g_ffn, w_in, conv_w, gla_wa_f, gla_ba_f, gla_wa_b, gla_ba_b,
           gla_g_norm, na_rpb, w_a_out, w_g_out, w_n_out, w_o, w_ffn1, w_ffn3, w_ffn2, g_final):
    bsz, seq, d = x.shape
    ct = ctx.shape[1]
    depth = w_ada.shape[0]
    f = w_ffn1.shape[2]
    assert bsz + 1 <= MOD_ROWS and seq % GRID_W == 0 and seq % GLA_CHUNK == 0 and ct % GLA_CHUNK == 0
    assert (seq // GRID_W) % (NA_GROUP * NA_GROUPS_PER_STEP) == 0 and seq // GRID_W > NA_KEY_ROWS
    off, n_used = _layout(d)
    tn_in = 10 * LANE
    n_pad = -(-n_used // tn_in) * tn_in

    tm_big = _largest_tile(seq, 1024, SUBLANE)
    tm_ffn = _largest_tile(seq, 512, SUBLANE)
    tm_branch = _largest_tile(seq, 256, SUBLANE)
    tf = _largest_tile(f, 512, LANE)
    lat_group = lambda tm: (lambda i: i // (seq // tm))
    ctx_group = lambda i: bsz

    cvec = jnp.zeros((MOD_ROWS, d), F32).at[:bsz].set(c).at[bsz].set(c_ctx)
    mod = _adaln(cvec, w_ada, b_ada).reshape(depth, MOD_ROWS, 6, d)
    rope = _rope_tables(seq)
    zero_state = jnp.zeros((bsz, GLA_PAIRS, LANE, 2 * GLA_DV), F32)

    w_in_p = _pack_w_in(w_in, n_pad)
    wa_o, wg_o, wn_o = w_a_out.astype(BF16), w_g_out.astype(BF16), w_n_out.astype(BF16)
    wo, w1, w3, w2 = w_o.astype(BF16), w_ffn1.astype(BF16), w_ffn3.astype(BF16), w_ffn2.astype(BF16)
    bias_blocks = _na_bias_blocks(na_rpb)

    h = x.reshape(bsz * seq, d)
    hc = ctx.reshape(bsz * ct, d)
    for l in range(depth):
        need_ctx = l < depth - 1
        last = l == depth - 1
        mod_l = mod[l]
        gm = mod_l[:, 2, :].reshape(MOD_ROWS, 1, d)
        waf, wab = _pad_lowrank(gla_wa_f[l], 0), _pad_lowrank(gla_wa_b[l], GLA_LOWRANK)
        baf, bab = gla_ba_f[l].reshape(1, GLA_KD), gla_ba_b[l].reshape(1, GLA_KD)

        p = _inproj(h, mod_l, g_mix[l], w_in_p, l, lat_group(tm_big), tm_big, tn_in)
        pc = _inproj(hc, mod_l, g_mix[l], w_in_p, l, ctx_group, ct, tn_in)
        p3 = p.reshape(bsz, seq, n_pad)
        pc3 = pc.reshape(bsz, ct, n_pad)

        ogc_f, ogc_b, s_f, s_b = _gla(pc3, off, waf, baf, wab, bab, zero_state, zero_state, None)
        og_f, og_b, _, _ = _gla(p3, off, waf, baf, wab, bab, s_f, s_b, rope)
        na = _na(p3, pc3, off, bias_blocks, l)

        h = _branch(p, off, og_f.reshape(bsz * seq, GLA_VD), og_b.reshape(bsz * seq, GLA_VD),
                    na.reshape(bsz * seq, NA_D), conv_w[l], gla_g_norm[l], wa_o, wg_o, wn_o, wo, l,
                    h, gm, lat_group(tm_branch), seq, tm_branch)
        h = _ffn(h, mod_l, g_ffn[l], w1, w3, w2, l, g_final, lat_group(tm_ffn), tm_ffn, tf, last)
        if need_ctx:
            na_c = _ctx_attn(pc3, off)
            hc = _branch(pc, off, ogc_f.reshape(bsz * ct, GLA_VD), ogc_b.reshape(bsz * ct, GLA_VD),
                         na_c.reshape(bsz * ct, NA_D), conv_w[l], gla_g_norm[l], wa_o, wg_o, wn_o, wo, l,
                         hc, gm, ctx_group, ct, ct)
            hc = _ffn(hc, mod_l, g_ffn[l], w1, w3, w2, l, g_final, ctx_group, ct, tf, False)
    return h.reshape(bsz, seq, d)
```

```python
import functools

import numpy as np
import jax
import jax.numpy as jnp
from jax import lax
from jax.experimental import pallas as pl
from jax.experimental.pallas import tpu as pltpu

GRID_W = 64
RMS_EPS = 1e-6
ROPE_BASE = 10000.0
N_BRANCH = 3
A_WIDTH = 512
GLA_HEADS = 6
GLA_DK = 64
GLA_DV = 128
GLA_KD = GLA_HEADS * GLA_DK
GLA_VD = GLA_HEADS * GLA_DV
GLA_LOWRANK = 16
GLA_TAU = 16.0
GLA_CHUNK = 64
GLA_MAX_CHUNKS_PER_STEP = 8
GLA_SAFE_EXPONENT = 80.0
GLA_PAIRS = GLA_HEADS // 2
NA_HEADS = 6
NA_DH = 128
NA_D = NA_HEADS * NA_DH
NA_WIN_H = 8
NA_WIN_W = 16
NA_GROUP = 4
NA_KEY_ROWS = NA_GROUP + NA_WIN_H - 1
NA_GROUPS_PER_STEP = 16
MASK_VALUE = -1e30

LANE = 128
SUBLANE = 8
HALO_ROWS = 16
VMEM_PHYSICAL = 64 * 1024 * 1024
VMEM_INTERNAL = 12 * 1024 * 1024
MOD_ROWS = 8

BF16 = jnp.bfloat16
F32 = jnp.float32


def _params(semantics, block_bytes):
    limit = min(int(block_bytes) + VMEM_INTERNAL, VMEM_PHYSICAL - 4 * 1024 * 1024)
    return pltpu.CompilerParams(dimension_semantics=semantics, vmem_limit_bytes=limit)


def _nbytes(shape, dtype):
    return int(np.prod(shape)) * jnp.dtype(dtype).itemsize


def _sigmoid(x):
    return 0.5 * jnp.tanh(0.5 * x) + 0.5


def _serpentine(i, j, n):
    return jnp.where(i % 2 == 0, j, n - 1 - j)


def _layout(d_model):
    off = {}
    pos = 0
    for name, width, block in (("gate", N_BRANCH * d_model, N_BRANCH * d_model),
                               ("a_in", A_WIDTH, A_WIDTH), ("a_b", A_WIDTH, A_WIDTH), ("a_c", A_WIDTH, A_WIDTH),
                               ("gq", GLA_KD, GLA_KD), ("gk", GLA_KD, GLA_KD), ("gv", GLA_VD, GLA_VD),
                               ("gr", GLA_VD, GLA_VD), ("gab", LANE, LANE),
                               ("nq", NA_D, NA_DH), ("nk", NA_D, NA_DH), ("nv", NA_D, NA_DH)):
        assert pos % block == 0, (name, pos, block)
        off[name] = pos
        pos += width
    return off, pos


def _adaln_kernel(c_ref, w_ref, b_ref, o_ref):
    cv = c_ref[...]
    s = cv * jax.nn.sigmoid(cv)
    o_ref[0] = jnp.dot(s, w_ref[0], preferred_element_type=F32) + b_ref[0]


def _adaln(cvec, w_ada, b_ada):
    depth, d, n = w_ada.shape
    tn = 1024 if n % 1024 == 0 else n
    blocks = 2 * (_nbytes((d, tn), F32) + 2 * _nbytes((MOD_ROWS, tn), F32)) + _nbytes((MOD_ROWS, d), F32)
    return pl.pallas_call(
        _adaln_kernel,
        grid=(depth, n // tn),
        in_specs=[pl.BlockSpec((MOD_ROWS, d), lambda l, j: (0, 0)),
                  pl.BlockSpec((1, d, tn), lambda l, j: (l, 0, j)),
                  pl.BlockSpec((1, 1, tn), lambda l, j: (l, 0, j))],
        out_specs=pl.BlockSpec((1, MOD_ROWS, tn), lambda l, j: (l, 0, j)),
        out_shape=jax.ShapeDtypeStruct((depth, MOD_ROWS, n), F32),
        compiler_params=_params(("arbitrary", "arbitrary"), blocks),
        name="adaln",
    )(cvec, w_ada, b_ada.reshape(depth, 1, n))


def _mod_norm(x, g, shift, scale):
    y = x * lax.rsqrt(jnp.mean(x * x, axis=-1, keepdims=True) + RMS_EPS)
    return (y * g) * (1.0 + scale) + shift


def _inproj_kernel(x_ref, mod_ref, g_ref, w_ref, o_ref, n_scr):
    def project(n):
        o_ref[...] = lax.dot_general(n, w_ref[...], (((1,), (1,)), ((), ())),
                                     preferred_element_type=F32).astype(o_ref.dtype)

    @pl.when(pl.program_id(1) == 0)
    def _():
        m = mod_ref[0]
        n = _mod_norm(x_ref[...], g_ref[...], m[0:1], m[1:2]).astype(BF16)
        n_scr[...] = n
        project(n)

    @pl.when(pl.program_id(1) != 0)
    def _():
        project(n_scr[...])


def _inproj(h, mod_l, g, w, layer, group_of_tile, tm, tn):
    m_rows, d = h.shape
    n = w.shape[1]
    blocks = (2 * (_nbytes((tm, d), F32) + _nbytes((d, tn), BF16) + _nbytes((tm, tn), BF16))
              + _nbytes((tm, d), BF16) + 2 * _nbytes((8, d), F32) + _nbytes((tm, tn), F32))
    return pl.pallas_call(
        _inproj_kernel,
        grid=(m_rows // tm, n // tn),
        in_specs=[pl.BlockSpec((tm, d), lambda i, j: (i, 0)),
                  pl.BlockSpec((1, 6, d), lambda i, j: (group_of_tile(i), 0, 0)),
                  pl.BlockSpec((1, d), lambda i, j: (0, 0)),
                  pl.BlockSpec((None, tn, d), lambda i, j: (layer, _serpentine(i, j, n // tn), 0))],
        out_specs=pl.BlockSpec((tm, tn), lambda i, j: (i, _serpentine(i, j, n // tn))),
        out_shape=jax.ShapeDtypeStruct((m_rows, n), BF16),
        scratch_shapes=[pltpu.VMEM((tm, d), BF16)],
        compiler_params=_params(("parallel", "arbitrary"), blocks),
        name="inproj",
    )(h, mod_l, g.reshape(1, d), w)


def _rope_pairs(x, cos, sin):
    lane = lax.broadcasted_iota(jnp.int32, cos.shape, 1)
    first_half = (lane % GLA_DK) < (GLA_DK // 2)
    out = []
    for p in range(GLA_PAIRS):
        xs = x[:, p * LANE:(p + 1) * LANE]
        swapped = jnp.where(first_half, pltpu.roll(xs, LANE - GLA_DK // 2, axis=1), pltpu.roll(xs, GLA_DK // 2, axis=1))
        out.append(xs * cos + swapped * sin)
    return jnp.concatenate(out, axis=1)


def _minus_chunk_row(x, offset, sign=1.0):
    parts = []
    for c in range(x.shape[0] // GLA_CHUNK):
        chunk = x[c * GLA_CHUNK:(c + 1) * GLA_CHUNK]
        ref_row = x[c * GLA_CHUNK + offset:c * GLA_CHUNK + offset + 1]
        parts.append(chunk - ref_row if sign > 0 else ref_row - chunk)
    return jnp.concatenate(parts, axis=0)


def _gla_prepare(q, k, ab, wa, ba, cos, sin, reverse):
    rows = q.shape[0]
    z = jnp.dot(ab, wa, preferred_element_type=F32) + ba
    la = (jnp.minimum(z, 0.0) - jnp.log(1.0 + jnp.exp(-jnp.abs(z)))) * (1.0 / GLA_TAU)
    r = lax.broadcasted_iota(jnp.int32, (rows, rows), 0)
    c = lax.broadcasted_iota(jnp.int32, (rows, rows), 1)
    tri = ((r // GLA_CHUNK == c // GLA_CHUNK) & ((c >= r) if reverse else (c <= r))).astype(BF16)
    la_hi = la.astype(BF16)
    la_lo = (la - la_hi.astype(F32)).astype(BF16)
    b = jnp.dot(tri, la_hi, preferred_element_type=F32) + jnp.dot(tri, la_lo, preferred_element_type=F32)
    q = q.astype(F32)
    k = k.astype(F32)
    if cos is not None:
        q = _rope_pairs(q, cos, sin)
        k = _rope_pairs(k, cos, sin)
    return q, k, b


def _gla_block(q, k, b, v, s_scr, o_ref, inter_scr, reverse):
    rows = q.shape[0]
    n_chunks = rows // GLA_CHUNK
    half = GLA_CHUNK // 2
    last_row = 0 if reverse else GLA_CHUNK - 1
    rel = _minus_chunk_row(b, half if reverse else half - 1)
    q_mid = (q * jnp.exp(rel))
    k_mid = (k * jnp.exp(-rel)).astype(BF16)
    q_state = (q * jnp.exp(b)).astype(BF16)
    k_last = (k * jnp.exp(_minus_chunk_row(b, last_row, sign=-1.0))).astype(BF16)
    row = lax.broadcasted_iota(jnp.int32, (2 * GLA_CHUNK, GLA_CHUNK), 0) % GLA_CHUNK
    col = lax.broadcasted_iota(jnp.int32, (2 * GLA_CHUNK, GLA_CHUNK), 1)
    visible = (row <= col) if reverse else (row >= col)
    first_head = lax.broadcasted_iota(jnp.int32, (rows, LANE), 1) < GLA_DK
    blk_r = lax.broadcasted_iota(jnp.int32, (LANE, 2 * GLA_DV), 0) < GLA_DK
    blk_c = lax.broadcasted_iota(jnp.int32, (LANE, 2 * GLA_DV), 1) < GLA_DV
    same_head = blk_r == blk_c
    nt = (((1,), (1,)), ((), ()))
    tn = (((0,), (0,)), ((), ()))
    q_first = [jnp.where(first_head, q_mid[:, p * LANE:(p + 1) * LANE], 0.0).astype(BF16) for p in range(GLA_PAIRS)]
    q_second = [jnp.where(first_head, 0.0, q_mid[:, p * LANE:(p + 1) * LANE]).astype(BF16) for p in range(GLA_PAIRS)]
    order = range(n_chunks - 1, -1, -1) if reverse else range(n_chunks)
    for c in order:
        rs = slice(c * GLA_CHUNK, (c + 1) * GLA_CHUNK)
        intra, inter = [], []
        for p in range(GLA_PAIRS):
            sl = slice(p * LANE, (p + 1) * LANE)
            q_stack = jnp.concatenate([q_first[p][rs], q_second[p][rs]], axis=0)
            att = lax.dot_general(q_stack, k_mid[rs, sl], nt, preferred_element_type=F32)
            att = jnp.where(visible, att, 0.0).astype(BF16)
            v_pair = v[rs, 2 * p * GLA_DV:(2 * p + 2) * GLA_DV]
            intra.append(jnp.dot(att[:GLA_CHUNK], v_pair[:, :GLA_DV], preferred_element_type=F32))
            intra.append(jnp.dot(att[GLA_CHUNK:], v_pair[:, GLA_DV:], preferred_element_type=F32))
            s_p = s_scr[p]
            inter.append(jnp.dot(q_state[rs, sl], s_p.astype(BF16), preferred_element_type=F32))
            update = lax.dot_general(k_last[rs, sl], v_pair, tn, preferred_element_type=F32)
            decay_row = jnp.exp(b[c * GLA_CHUNK + last_row:c * GLA_CHUNK + last_row + 1, sl])
            decay_col = jnp.transpose(jnp.broadcast_to(decay_row, (SUBLANE, LANE)))[:, 0:1]
            s_scr[p] = s_p * decay_col + jnp.where(same_head, update, 0.0)
        inter = jnp.concatenate(inter, axis=1)
        o_ref[0, rs, :] = jnp.concatenate(intra, axis=1) + inter
        inter_scr[rs, :] = inter
    return jnp.max(jnp.abs(rel))


def _gla_chunk_exact(q_scr, k_scr, b_scr, v_scr, o_ref, row0, reverse):
    length = GLA_CHUNK
    rows = pl.ds(row0, length)
    q = q_scr[rows, :]
    b = b_scr[rows, :]
    row = lax.broadcasted_iota(jnp.int32, (length, 1), 0)
    first_head = lax.broadcasted_iota(jnp.int32, (length, LANE), 1) < GLA_DK

    def body(s, carry):
        src = pl.ds(row0 + s, 1)
        w = q * k_scr[src, :] * jnp.exp(jnp.minimum(b - b_scr[src, :], 0.0))
        w = jnp.where((row <= s) if reverse else (row >= s), w, 0.0)
        v_row = v_scr[src, :]
        for p in range(GLA_PAIRS):
            wp = w[:, p * LANE:(p + 1) * LANE]
            for hh in range(2):
                a = jnp.sum(jnp.where(first_head == (hh == 0), wp, 0.0), axis=1, keepdims=True)
                lanes = slice((2 * p + hh) * GLA_DV, (2 * p + hh + 1) * GLA_DV)
                o_ref[0, rows, lanes] += a * v_row[:, lanes]
        return carry

    lax.fori_loop(0, length, body, 0)


def _gla_kernel(*refs, use_rope):
    qf, kf, vf, abf, qb, kb, vb, abb = refs[:8]
    pos = 8
    if use_rope:
        cosf, sinf, cosb, sinb = refs[pos:pos + 4]
        pos += 4
    (waf, baf, wab, bab, s0f, s0b, of_ref, ob_ref, sf_ref, sb_ref,
     sf_scr, sb_scr, inter_f, inter_b, q_scr, k_scr, b_scr, v_scr) = refs[pos:]
    step = pl.program_id(1)

    @pl.when(step == 0)
    def _():
        sf_scr[...] = s0f[0]
        sb_scr[...] = s0b[0]

    directions = ((False, (qf, kf, vf, abf), (cosf, sinf) if use_rope else None, waf, baf, sf_scr, of_ref, inter_f),
                  (True, (qb, kb, vb, abb), (cosb, sinb) if use_rope else None, wab, bab, sb_scr, ob_ref, inter_b))

    def prepared(reverse, refs4, rope_refs, w_ref, bias_ref):
        q_ref, k_ref, _, ab_ref = refs4
        cos = rope_refs[0][...] if use_rope else None
        sin = rope_refs[1][...] if use_rope else None
        return _gla_prepare(q_ref[0], k_ref[0], ab_ref[0], w_ref[...], bias_ref[...], cos, sin, reverse)

    margin = jnp.float32(0.0)
    for reverse, refs4, rope_refs, w_ref, bias_ref, s_scr, o_ref, inter_scr in directions:
        q, k, b = prepared(reverse, refs4, rope_refs, w_ref, bias_ref)
        m = _gla_block(q, k, b, refs4[2][0], s_scr, o_ref, inter_scr, reverse)
        margin = jnp.maximum(margin, m)

    @pl.when(margin > GLA_SAFE_EXPONENT)
    def _():
        for reverse, refs4, rope_refs, w_ref, bias_ref, s_scr, o_ref, inter_scr in directions:
            q, k, b = prepared(reverse, refs4, rope_refs, w_ref, bias_ref)
            q_scr[...] = q
            k_scr[...] = k
            b_scr[...] = b
            v_scr[...] = refs4[2][0].astype(F32)
            o_ref[0] = inter_scr[...]
            for c in range(o_ref.shape[1] // GLA_CHUNK):
                _gla_chunk_exact(q_scr, k_scr, b_scr, v_scr, o_ref, c * GLA_CHUNK, reverse)

    @pl.when(step == pl.num_programs(1) - 1)
    def _():
        sf_ref[0] = sf_scr[...]
        sb_ref[0] = sb_scr[...]


def _gla(p3, off, waf, baf, wab, bab, s0f, s0b, rope):
    bsz, t, _ = p3.shape
    length = GLA_CHUNK * _largest_tile(t // GLA_CHUNK, GLA_MAX_CHUNKS_PER_STEP, 1)
    n = t // length
    fwd = lambda blk: (lambda b, i: (b, i, blk))
    bwd = lambda blk: (lambda b, i: (b, n - 1 - i, blk))
    q_blk, k_blk = off["gq"] // GLA_KD, off["gk"] // GLA_KD
    v_blk, ab_blk = off["gv"] // GLA_VD, off["gab"] // LANE
    seq_specs = lambda m: [pl.BlockSpec((1, length, GLA_KD), m(q_blk)), pl.BlockSpec((1, length, GLA_KD), m(k_blk)),
                           pl.BlockSpec((1, length, GLA_VD), m(v_blk)), pl.BlockSpec((1, length, LANE), m(ab_blk))]
    in_specs = seq_specs(fwd) + seq_specs(bwd)
    args = [p3] * 8
    if rope is not None:
        in_specs += [pl.BlockSpec((length, LANE), lambda b, i: (i, 0))] * 2
        in_specs += [pl.BlockSpec((length, LANE), lambda b, i: (n - 1 - i, 0))] * 2
        args += [rope[0], rope[1], rope[0], rope[1]]
    const2 = lambda b, i: (0, 0)
    state_dims = (GLA_PAIRS, LANE, 2 * GLA_DV)
    state_spec = pl.BlockSpec((1,) + state_dims, lambda b, i: (b, 0, 0, 0))
    in_specs += [pl.BlockSpec((LANE, GLA_KD), const2), pl.BlockSpec((1, GLA_KD), const2),
                 pl.BlockSpec((LANE, GLA_KD), const2), pl.BlockSpec((1, GLA_KD), const2),
                 state_spec, state_spec]
    args += [waf, baf, wab, bab, s0f, s0b]
    state_shape = jax.ShapeDtypeStruct((bsz,) + state_dims, F32)
    out_shape = jax.ShapeDtypeStruct((bsz, t, GLA_VD), F32)
    blocks = 2 * (2 * (2 * _nbytes((length, GLA_KD), F32) + 2 * _nbytes((length, GLA_VD), F32) + 5 * _nbytes((length, LANE), F32))
                  + 4 * _nbytes(state_dims, F32) + 2 * _nbytes((LANE, GLA_KD), F32))
    blocks += 2 * _nbytes(state_dims, F32) + 2 * _nbytes((length, GLA_VD), F32)
    return pl.pallas_call(
        functools.partial(_gla_kernel, use_rope=rope is not None),
        grid=(bsz, n),
        in_specs=in_specs,
        out_specs=[pl.BlockSpec((1, length, GLA_VD), fwd(0)), pl.BlockSpec((1, length, GLA_VD), bwd(0)),
                   state_spec, state_spec],
        out_shape=[out_shape, out_shape, state_shape, state_shape],
        scratch_shapes=[pltpu.VMEM(state_dims, F32), pltpu.VMEM(state_dims, F32),
                        pltpu.VMEM((length, GLA_VD), F32), pltpu.VMEM((length, GLA_VD), F32),
                        pltpu.VMEM((length, GLA_KD), F32), pltpu.VMEM((length, GLA_KD), F32),
                        pltpu.VMEM((length, GLA_KD), F32), pltpu.VMEM((length, GLA_VD), F32)],
        compiler_params=_params(("arbitrary", "arbitrary"), blocks),
        name="gla",
    )(*args)


def _na_kernel(q_ref, k_ref, v_ref, kc_ref, vc_ref, blk_ref, o_ref, tab_scr, *, rows):
    nt = (((1,), (1,)), ((), ()))
    step = pl.program_id(2)

    @pl.when(step == 0)
    def _():
        masked = jnp.full((GRID_W, GRID_W), MASK_VALUE, F32)
        for variant, per_query in enumerate(_na_row_offsets(rows)):
            for rq, offsets in enumerate(per_query):
                tab_scr[variant, rq * GRID_W:(rq + 1) * GRID_W, :] = jnp.concatenate(
                    [masked if r is None else blk_ref[0, r] for r in offsets], axis=1)

    kc = kc_ref[0]
    vc = vc_ref[0]
    nq = NA_GROUP * GRID_W
    nk = NA_KEY_ROWS * GRID_W
    for g in range(NA_GROUPS_PER_STEP):
        r0 = (step * NA_GROUPS_PER_STEP + g) * NA_GROUP
        key_row0 = jnp.clip(r0 - NA_WIN_H // 2, 0, rows - NA_KEY_ROWS)
        k_off = pl.multiple_of(key_row0 * GRID_W, GRID_W)
        variant = jnp.where(r0 == 0, 0, jnp.where(r0 == rows - NA_GROUP, 2, 1))
        q = q_ref[0, g * nq:(g + 1) * nq, :]
        kr = k_ref[0, pl.ds(k_off, nk), :]
        vr = v_ref[0, pl.ds(k_off, nk), :]
        s_loc = lax.dot_general(q, kr, nt, preferred_element_type=F32) + tab_scr[variant]
        s_ctx = lax.dot_general(q, kc, nt, preferred_element_type=F32)
        m = jnp.maximum(jnp.max(s_loc, axis=-1, keepdims=True), jnp.max(s_ctx, axis=-1, keepdims=True))
        p_loc = jnp.exp(s_loc - m)
        p_ctx = jnp.exp(s_ctx - m)
        denom = jnp.sum(p_loc, axis=-1, keepdims=True) + jnp.sum(p_ctx, axis=-1, keepdims=True)
        o = (jnp.dot(p_loc.astype(BF16), vr, preferred_element_type=F32)
             + jnp.dot(p_ctx.astype(BF16), vc, preferred_element_type=F32))
        o_ref[0, g * nq:(g + 1) * nq, :] = (o / denom).astype(o_ref.dtype)


def _na(p3, pc3, off, bias_blocks, layer):
    bsz, t, _ = p3.shape
    ct = pc3.shape[1]
    rows = t // GRID_W
    tq = NA_GROUPS_PER_STEP * NA_GROUP * GRID_W
    nq, nk = NA_GROUP * GRID_W, NA_KEY_ROWS * GRID_W
    n_off = 2 * NA_WIN_H - 1
    q0, k0, v0 = off["nq"] // NA_DH, off["nk"] // NA_DH, off["nv"] // NA_DH
    blocks = (2 * (2 * _nbytes((tq, NA_DH), F32) + 2 * _nbytes((t, NA_DH), BF16) + 2 * _nbytes((ct, NA_DH), BF16)
                   + _nbytes((n_off, GRID_W, LANE), F32)) + _nbytes((3, nq, nk), F32)
              + 3 * NA_GROUPS_PER_STEP * _nbytes((nq, nk + ct), F32))
    return pl.pallas_call(
        functools.partial(_na_kernel, rows=rows),
        grid=(bsz, NA_HEADS, rows // (NA_GROUPS_PER_STEP * NA_GROUP)),
        in_specs=[pl.BlockSpec((1, tq, NA_DH), lambda b, h, r: (b, r, q0 + h)),
                  pl.BlockSpec((1, t, NA_DH), lambda b, h, r: (b, 0, k0 + h)),
                  pl.BlockSpec((1, t, NA_DH), lambda b, h, r: (b, 0, v0 + h)),
                  pl.BlockSpec((1, ct, NA_DH), lambda b, h, r: (b, 0, k0 + h)),
                  pl.BlockSpec((1, ct, NA_DH), lambda b, h, r: (b, 0, v0 + h)),
                  pl.BlockSpec((None, 1, n_off, GRID_W, GRID_W), lambda b, h, r: (layer, h, 0, 0, 0))],
        out_specs=pl.BlockSpec((1, tq, NA_DH), lambda b, h, r: (b, r, h)),
        out_shape=jax.ShapeDtypeStruct((bsz, t, NA_D), BF16),
        scratch_shapes=[pltpu.VMEM((3, nq, nk), F32)],
        compiler_params=_params(("arbitrary", "arbitrary", "arbitrary"), blocks),
        name="natten",
    )(p3, p3, p3, pc3, pc3, bias_blocks)


def _na_row_offsets(rows):
    table = []
    for r0 in (0, NA_GROUP, rows - NA_GROUP):
        key_row0 = int(np.clip(r0 - NA_WIN_H // 2, 0, rows - NA_KEY_ROWS))
        per_query = []
        for rq in range(NA_GROUP):
            r = r0 + rq
            r_start = int(np.clip(r - NA_WIN_H // 2, 0, rows - NA_WIN_H))
            per_query.append([key_row0 + ki - r + (NA_WIN_H - 1) if r_start <= key_row0 + ki < r_start + NA_WIN_H
                              else None for ki in range(NA_KEY_ROWS)])
        table.append(per_query)
    return table


def _na_bias_blocks(rpb):
    w = GRID_W
    col = np.arange(w)
    col_start = np.clip(col - NA_WIN_W // 2, 0, w - NA_WIN_W)
    valid_col = (col[None, :] >= col_start[:, None]) & (col[None, :] < col_start[:, None] + NA_WIN_W)
    col_off = col[None, :] - col[:, None] + (NA_WIN_W - 1)
    col_sel = np.zeros((2 * NA_WIN_W - 1, w, w), np.float32)
    cc, jj = np.nonzero(valid_col)
    col_sel[col_off[cc, jj], cc, jj] = 1.0
    blocks = jnp.einsum("lhrx,xcj->lhrcj", rpb.astype(F32), col_sel, precision=lax.Precision.HIGHEST)
    return jnp.where(valid_col, blocks, MASK_VALUE)


def _ctx_attn_kernel(q_ref, k_ref, v_ref, o_ref):
    s = lax.dot_general(q_ref[0], k_ref[0], (((1,), (1,)), ((), ())), preferred_element_type=F32)
    p = jnp.exp(s - jnp.max(s, axis=-1, keepdims=True))
    o = jnp.dot(p.astype(BF16), v_ref[0], preferred_element_type=F32)
    o_ref[0] = (o / jnp.sum(p, axis=-1, keepdims=True)).astype(o_ref.dtype)


def _ctx_attn(pc3, off):
    bsz, ct, _ = pc3.shape
    q0, k0, v0 = off["nq"] // NA_DH, off["nk"] // NA_DH, off["nv"] // NA_DH
    spec = lambda c0: pl.BlockSpec((1, ct, NA_DH), lambda b, h: (b, 0, c0 + h))
    return pl.pallas_call(
        _ctx_attn_kernel,
        grid=(bsz, NA_HEADS),
        in_specs=[spec(q0), spec(k0), spec(v0)],
        out_specs=spec(0),
        out_shape=jax.ShapeDtypeStruct((bsz, ct, NA_D), BF16),
        compiler_params=_params(("arbitrary", "arbitrary"), 8 * _nbytes((ct, NA_DH), F32)),
        name="ctx_attn",
    )(pc3, pc3, pc3)


def _branch_kernel(gate_ref, ain_ref, ab_ref, ac_ref, ainp_ref, acp_ref, ainn_ref, acn_ref,
                   of_ref, ob_ref, gr_ref, na_ref, cw_ref, gn_ref, wa_ref, wg_ref, wn_ref, wo_ref, h_ref, gm_ref,
                   o_ref, *, tiles_per_seq):
    tm = ain_ref.shape[0]
    d = o_ref.shape[1]
    i = pl.program_id(0)
    tile_in_seq = i % tiles_per_seq
    u = ac_ref[...].astype(F32) * ain_ref[...].astype(F32)
    u_before = jnp.where(tile_in_seq == 0, 0.0, (acp_ref[...].astype(F32) * ainp_ref[...].astype(F32))[HALO_ROWS - 1:HALO_ROWS])
    u_after = jnp.where(tile_in_seq == tiles_per_seq - 1, 0.0, (acn_ref[...].astype(F32) * ainn_ref[...].astype(F32))[0:1])
    row = lax.broadcasted_iota(jnp.int32, u.shape, 0)
    u_prev = jnp.where(row == 0, u_before, pltpu.roll(u, 1, axis=0))
    u_next = jnp.where(row == tm - 1, u_after, pltpu.roll(u, tm - 1, axis=0))
    cw = cw_ref[...]
    a = ab_ref[...].astype(F32) * (cw[0:1] * u_prev + cw[1:2] * u + cw[2:3] * u_next)
    o = of_ref[...] + ob_ref[...]
    gn = gn_ref[...]
    normed = []
    for h in range(GLA_HEADS):
        oh = o[:, h * GLA_DV:(h + 1) * GLA_DV]
        normed.append(oh * lax.rsqrt(jnp.mean(oh * oh, axis=-1, keepdims=True) + RMS_EPS) * gn)
    r = gr_ref[...].astype(F32)
    g = jnp.concatenate(normed, axis=1) * (r * _sigmoid(r))
    y = _sigmoid(gate_ref[:, 0:d].astype(F32)) * jnp.dot(a.astype(BF16), wa_ref[...], preferred_element_type=F32)
    y = y + _sigmoid(gate_ref[:, d:2 * d].astype(F32)) * jnp.dot(g.astype(BF16), wg_ref[...], preferred_element_type=F32)
    y = y + _sigmoid(gate_ref[:, 2 * d:3 * d].astype(F32)) * jnp.dot(na_ref[...], wn_ref[...],
                                                                    preferred_element_type=F32)
    o_ref[...] = h_ref[...] + gm_ref[0] * jnp.dot(y.astype(BF16), wo_ref[...], preferred_element_type=F32)


def _branch(p, off, og_f, og_b, na, conv_w, g_norm, wa, wg, wn, wo, layer, h, gate, group_of_tile, seq_len, tm):
    m_rows = p.shape[0]
    d = wa.shape[2]
    tiles_per_seq = seq_len // tm
    halo = tm // HALO_ROWS
    n_halo = m_rows // HALO_ROWS
    cur = lambda blk: (lambda i: (i, blk))
    prev = lambda blk: (lambda i: (jnp.maximum(i * halo - 1, 0), blk))
    nxt = lambda blk: (lambda i: (jnp.minimum((i + 1) * halo, n_halo - 1), blk))
    a_in, a_b, a_c = off["a_in"] // A_WIDTH, off["a_b"] // A_WIDTH, off["a_c"] // A_WIDTH
    const = lambda i: (0, 0)
    lay = lambda i: (layer, 0, 0)
    in_specs = [pl.BlockSpec((tm, N_BRANCH * d), cur(off["gate"] // (N_BRANCH * d))),
                pl.BlockSpec((tm, A_WIDTH), cur(a_in)), pl.BlockSpec((tm, A_WIDTH), cur(a_b)),
                pl.BlockSpec((tm, A_WIDTH), cur(a_c)),
                pl.BlockSpec((HALO_ROWS, A_WIDTH), prev(a_in)), pl.BlockSpec((HALO_ROWS, A_WIDTH), prev(a_c)),
                pl.BlockSpec((HALO_ROWS, A_WIDTH), nxt(a_in)), pl.BlockSpec((HALO_ROWS, A_WIDTH), nxt(a_c)),
                pl.BlockSpec((tm, GLA_VD), cur(0)), pl.BlockSpec((tm, GLA_VD), cur(0)),
                pl.BlockSpec((tm, GLA_VD), cur(off["gr"] // GLA_VD)),
                pl.BlockSpec((tm, NA_D), cur(0)),
                pl.BlockSpec((3, A_WIDTH), const), pl.BlockSpec((1, GLA_DV), const),
                pl.BlockSpec((None, A_WIDTH, d), lay, pipeline_mode=pl.Buffered(1)),
                pl.BlockSpec((None, GLA_VD, d), lay, pipeline_mode=pl.Buffered(1)),
                pl.BlockSpec((None, NA_D, d), lay, pipeline_mode=pl.Buffered(1)),
                pl.BlockSpec((None, d, d), lay, pipeline_mode=pl.Buffered(1)),
                pl.BlockSpec((tm, d), cur(0)),
                pl.BlockSpec((1, 1, d), lambda i: (group_of_tile(i), 0, 0))]
    blocks = (2 * (_nbytes((tm, N_BRANCH * d), BF16) + 3 * _nbytes((tm, A_WIDTH), BF16) + 4 * _nbytes((tm, GLA_VD), F32)
                   + 2 * _nbytes((tm, d), F32)) + _nbytes((A_WIDTH + GLA_VD + NA_D + d, d), BF16) + 6 * _nbytes((tm, d), F32))
    return pl.pallas_call(
        functools.partial(_branch_kernel, tiles_per_seq=tiles_per_seq),
        grid=(m_rows // tm,),
        in_specs=in_specs,
        out_specs=pl.BlockSpec((tm, d), cur(0)),
        out_shape=jax.ShapeDtypeStruct((m_rows, d), F32),
        compiler_params=_params(("arbitrary",), blocks),
        name="branch_merge",
    )(p, p, p, p, p, p, p, p, og_f, og_b, p, na, conv_w, g_norm.reshape(1, GLA_DV), wa, wg, wn, wo, h, gate)


def _ffn_kernel(x_ref, mod_ref, g_ref, w1_ref, w3_ref, w2_ref, gfin_ref, o_ref, n_scr, acc_scr, *, final_norm):
    j = pl.program_id(1)
    last = pl.num_programs(1) - 1

    def hidden_slice(n):
        a = jnp.dot(n, w1_ref[...], preferred_element_type=F32)
        b = jnp.dot(n, w3_ref[...], preferred_element_type=F32)
        u = (a * _sigmoid(a)) * b
        return jnp.dot(u.astype(BF16), w2_ref[...], preferred_element_type=F32)

    @pl.when(j == 0)
    def _():
        m = mod_ref[0]
        n = _mod_norm(x_ref[...], g_ref[...], m[3:4], m[4:5]).astype(BF16)
        n_scr[...] = n
        acc_scr[...] = hidden_slice(n)

    @pl.when((j > 0) & (j < last))
    def _():
        acc_scr[...] += hidden_slice(n_scr[...])

    @pl.when(j == last)
    def _():
        out = x_ref[...] + mod_ref[0][5:6] * (acc_scr[...] + hidden_slice(n_scr[...]))
        if final_norm:
            out = out * lax.rsqrt(jnp.mean(out * out, axis=-1, keepdims=True) + RMS_EPS) * gfin_ref[...]
        o_ref[...] = out


def _ffn(h, mod_l, g, w1, w3, w2, layer, g_final, group_of_tile, tm, tf, final_norm):
    m_rows, d = h.shape
    f = w1.shape[2]
    blocks = (4 * _nbytes((tm, d), F32) + 6 * _nbytes((d, tf), BF16) + _nbytes((tm, d), BF16) + _nbytes((tm, d), F32)
              + 4 * _nbytes((tm, tf), F32))
    return pl.pallas_call(
        functools.partial(_ffn_kernel, final_norm=final_norm),
        grid=(m_rows // tm, f // tf),
        in_specs=[pl.BlockSpec((tm, d), lambda i, j: (i, 0)),
                  pl.BlockSpec((1, 6, d), lambda i, j: (group_of_tile(i), 0, 0)),
                  pl.BlockSpec((1, d), lambda i, j: (0, 0)),
                  pl.BlockSpec((None, d, tf), lambda i, j: (layer, 0, _serpentine(i, j, f // tf))),
                  pl.BlockSpec((None, d, tf), lambda i, j: (layer, 0, _serpentine(i, j, f // tf))),
                  pl.BlockSpec((None, tf, d), lambda i, j: (layer, _serpentine(i, j, f // tf), 0)),
                  pl.BlockSpec((1, d), lambda i, j: (0, 0))],
        out_specs=pl.BlockSpec((tm, d), lambda i, j: (i, 0)),
        out_shape=jax.ShapeDtypeStruct((m_rows, d), F32),
        scratch_shapes=[pltpu.VMEM((tm, d), BF16), pltpu.VMEM((tm, d), F32)],
        compiler_params=_params(("parallel", "arbitrary"), blocks),
        name="ffn",
    )(h, mod_l, g.reshape(1, d), w1, w3, w2, g_final.reshape(1, d))


def _largest_tile(total, cap, align):
    t = min(total, cap)
    while total % t or t % align:
        t -= align
    return t


def _rope_tables(seq_len):
    pos = jnp.arange(seq_len, dtype=jnp.int32)
    rowf = (pos // GRID_W).astype(F32)
    colf = (pos % GRID_W).astype(F32)
    per_axis = GLA_DK // 2
    inv = ROPE_BASE ** (-jnp.arange(0, per_axis, 2, dtype=F32) / per_axis)
    ang = jnp.concatenate([rowf[:, None] * inv, colf[:, None] * inv], axis=-1)
    cos, sin = jnp.cos(ang), jnp.sin(ang)
    reps = LANE // GLA_DK
    return jnp.tile(cos, (1, 2 * reps)), jnp.tile(jnp.concatenate([-sin, sin], axis=-1), (1, reps))


def _pack_kernel(w_ref, o_ref, *, d_model, n_pad):
    sizes = (A_WIDTH, A_WIDTH, A_WIDTH, GLA_KD, GLA_KD, GLA_VD, GLA_VD, GLA_LOWRANK, GLA_LOWRANK, NA_D, NA_D, NA_D)
    start = dict(zip(("a_in", "a_b", "a_c", "gq", "gk", "gv", "gr", "gaf", "gab", "nq", "nk", "nv"),
                     np.concatenate([[0], np.cumsum(sizes)[:-1]]).tolist()))
    gate0 = sum(sizes)
    cols = w_ref.shape[2]
    pieces = [w_ref[0, gate0:gate0 + N_BRANCH * d_model, :],
              w_ref[0, 0:start["gq"], :],
              w_ref[0, start["gq"]:start["gk"], :] * (GLA_DK ** -0.5),
              w_ref[0, start["gk"]:start["nq"], :],
              jnp.zeros((LANE - 2 * GLA_LOWRANK, cols), F32),
              w_ref[0, start["nq"]:start["nk"], :] * (NA_DH ** -0.5),
              w_ref[0, start["nk"]:gate0, :]]
    used = sum(p.shape[0] for p in pieces)
    pieces.append(jnp.zeros((n_pad - used, cols), F32))
    o_ref[0] = jnp.concatenate(pieces, axis=0).astype(BF16)


def _pack_w_in(w_in, n_pad):
    depth, d, n_in = w_in.shape
    w_t = jnp.swapaxes(w_in, 1, 2)
    tc = LANE
    blocks = 2 * (_nbytes((n_in, tc), F32) + _nbytes((n_pad, tc), BF16)) + 2 * _nbytes((n_pad, tc), F32)
    return pl.pallas_call(
        functools.partial(_pack_kernel, d_model=d, n_pad=n_pad),
        grid=(depth, d // tc),
        in_specs=[pl.BlockSpec((1, n_in, tc), lambda l, i: (l, 0, i))],
        out_specs=pl.BlockSpec((1, n_pad, tc), lambda l, i: (l, 0, i)),
        out_shape=jax.ShapeDtypeStruct((depth, n_pad, d), BF16),
        compiler_params=_params(("arbitrary", "arbitrary"), blocks),
        name="pack_w_in",
    )(w_t)


def _pad_lowrank(wa, col0):
    return jnp.zeros((LANE, GLA_KD), BF16).at[col0:col0 + GLA_LOWRANK].set(wa.astype(BF16))


def kernel(x, c, ctx, c_ctx, w_ada, b_ada, g_mix, g_ffn, w_in, conv_w, gla_wa_f, gla_ba_f, gla_wa_b, gla_ba_b,
           gla_g_norm, na_rpb, w_a_out, w_g_out, w_n_out, w_o, w_ffn1, w_ffn3, w_ffn2, g_final):
    bsz, seq, d = x.shape
    ct = ctx.shape[1]
    depth = w_ada.shape[0]
    f = w_ffn1.shape[2]
    assert bsz + 1 <= MOD_ROWS and seq % GRID_W == 0 and seq % GLA_CHUNK == 0 and ct % GLA_CHUNK == 0
    assert (seq // GRID_W) % (NA_GROUP * NA_GROUPS_PER_STEP) == 0 and seq // GRID_W > NA_KEY_ROWS
    off, n_used = _layout(d)
    tn_in = 10 * LANE
    n_pad = -(-n_used // tn_in) * tn_in

    tm_big = _largest_tile(seq, 1024, SUBLANE)
    tm_ffn = _largest_tile(seq, 512, SUBLANE)
    tm_branch = _largest_tile(seq, 256, SUBLANE)
    tm_ctx = _largest_tile(bsz * ct, 1024, SUBLANE)
    tf = _largest_tile(f, 512, LANE)
    lat_group = lambda tm: (lambda i: i // (seq // tm))
    ctx_group = lambda i: bsz

    cvec = jnp.zeros((MOD_ROWS, d), F32).at[:bsz].set(c).at[bsz].set(c_ctx)
    mod = _adaln(cvec, w_ada, b_ada).reshape(depth, MOD_ROWS, 6, d)
    rope = _rope_tables(seq)
    zero_state = jnp.zeros((bsz, GLA_PAIRS, LANE, 2 * GLA_DV), F32)

    w_in_p = _pack_w_in(w_in, n_pad)
    wa_o, wg_o, wn_o = w_a_out.astype(BF16), w_g_out.astype(BF16), w_n_out.astype(BF16)
    wo, w1, w3, w2 = w_o.astype(BF16), w_ffn1.astype(BF16), w_ffn3.astype(BF16), w_ffn2.astype(BF16)
    bias_blocks = _na_bias_blocks(na_rpb)

    h = x.reshape(bsz * seq, d)
    hc = ctx.reshape(bsz * ct, d)
    for l in range(depth):
        need_ctx = l < depth - 1
        last = l == depth - 1
        mod_l = mod[l]
        gm = mod_l[:, 2, :].reshape(MOD_ROWS, 1, d)
        waf, wab = _pad_lowrank(gla_wa_f[l], 0), _pad_lowrank(gla_wa_b[l], GLA_LOWRANK)
        baf, bab = gla_ba_f[l].reshape(1, GLA_KD), gla_ba_b[l].reshape(1, GLA_KD)

        p = _inproj(h, mod_l, g_mix[l], w_in_p, l, lat_group(tm_big), tm_big, tn_in)
        pc = _inproj(hc, mod_l, g_mix[l], w_in_p, l, ctx_group, tm_ctx, tn_in)
        p3 = p.reshape(bsz, seq, n_pad)
        pc3 = pc.reshape(bsz, ct, n_pad)

        ogc_f, ogc_b, s_f, s_b = _gla(pc3, off, waf, baf, wab, bab, zero_state, zero_state, None)
        og_f, og_b, _, _ = _gla(p3, off, waf, baf, wab, bab, s_f, s_b, rope)
        na = _na(p3, pc3, off, bias_blocks, l)

        h = _branch(p, off, og_f.reshape(bsz * seq, GLA_VD), og_b.reshape(bsz * seq, GLA_VD),
                    na.reshape(bsz * seq, NA_D), conv_w[l], gla_g_norm[l], wa_o, wg_o, wn_o, wo, l,
                    h, gm, lat_group(tm_branch), seq, tm_branch)
        h = _ffn(h, mod_l, g_ffn[l], w1, w3, w2, l, g_final, lat_group(tm_ffn), tm_ffn, tf, last)
        if need_ctx:
            na_c = _ctx_attn(pc3, off)
            hc = _branch(pc, off, ogc_f.reshape(bsz * ct, GLA_VD), ogc_b.reshape(bsz * ct, GLA_VD),
                         na_c.reshape(bsz * ct, NA_D), conv_w[l], gla_g_norm[l], wa_o, wg_o, wn_o, wo, l,
                         hc, gm, ctx_group, ct, ct)
            hc = _ffn(hc, mod_l, g_ffn[l], w1, w3, w2, l, g_final, ctx_group, tm_ctx, tf, False)
    return h.reshape(bsz, seq, d)
```
